```python
import math
import jax, jax.numpy as jnp
from jax import lax
import numpy as np

D_MODEL = 1024
BATCH = 8
SEQ = 2048
DEPTH = 1

ATT_HEADS = 16
ATT_HEAD_DIM = 64
KV_LATENT = 128
IDX_HEADS = 8
IDX_DIM = 64
TOPK_MAX = 256
Q_BLOCK = 128
REL_BUCKETS = 32
REL_MAX_DIST = 128
SSM_INNER = 2 * D_MODEL
SSM_HEAD_DIM = 64
SSM_HEADS = SSM_INNER // SSM_HEAD_DIM
SSM_GROUPS = 4
SSM_HEADS_PER_GROUP = SSM_HEADS // SSM_GROUPS
SSM_STATE = 128
SSM_CONV = 4
SSM_CHUNK = 128
SSM_XBC = SSM_INNER + 2 * SSM_GROUPS * SSM_STATE
FFN_DIM = 2816
FFN_CONV = 3
EPS = 1e-6

IN_SPLITS = (
    ATT_HEADS * ATT_HEAD_DIM,
    KV_LATENT,
    IDX_HEADS * IDX_DIM,
    IDX_DIM,
    IDX_HEADS,
    SSM_INNER,
    SSM_XBC,
    SSM_HEADS,
    D_MODEL,
    D_MODEL,
)
IN_COLS = sum(IN_SPLITS)

kernel_name = "hybrid_dsa_ssd_convffn_block"


def split_cols(a, sizes):
    idx, acc = [], 0
    for sz in sizes[:-1]:
        acc += sz
        idx.append(acc)
    return jnp.split(a, idx, axis=-1)


def rmsnorm(x, g):
    x32 = x.astype(jnp.float32)
    y = x32 * lax.rsqrt(jnp.mean(x32 * x32, axis=-1, keepdims=True) + EPS)
    return (y * g.astype(jnp.float32)).astype(x.dtype)


def causal_depthwise_conv(u, w, bias):
    k, ch = w.shape
    out = lax.conv_general_dilated(
        u, w[:, None, :].astype(u.dtype), window_strides=(1,), padding=[(k - 1, 0)],
        dimension_numbers=("NWC", "WIO", "NWC"), feature_group_count=ch)
    return out + bias.astype(u.dtype)


def t5_bucket(dist):
    n = jnp.maximum(dist, 0)
    max_exact = REL_BUCKETS // 2
    large = max_exact + (jnp.log(jnp.maximum(n, 1).astype(jnp.float32) / max_exact)
                         / math.log(REL_MAX_DIST / max_exact)
                         * (REL_BUCKETS - max_exact)).astype(jnp.int32)
    large = jnp.minimum(large, REL_BUCKETS - 1)
    return jnp.where(n < max_exact, n, large)


def dsa_attention(q_lat, c_kv, q_idx, k_idx, w_idx, rel_bias):
    b, s, h, c = q_lat.shape
    n_sel = min(TOPK_MAX, s // 4)
    nblk = s // Q_BLOCK
    scale = ATT_HEAD_DIM ** -0.5
    key_pos = jnp.arange(s)
    gather = jax.vmap(lambda table, idx: table[idx])

    def to_blocks(a):
        return a.reshape(b, nblk, Q_BLOCK, *a.shape[2:]).swapaxes(0, 1)

    def block(args):
        qb, qib, wib, blk = args
        q_pos = blk * Q_BLOCK + jnp.arange(Q_BLOCK)
        causal = key_pos[None, :] <= q_pos[:, None]
        idx_logits = jnp.einsum("bqhd,bsd->bqhs", qib, k_idx,
                                preferred_element_type=jnp.float32)
        score = jnp.einsum("bqhs,bqh->bqs", jax.nn.relu(idx_logits),
                           wib.astype(jnp.float32))
        score = jnp.where(causal[None], score, -jnp.inf)
        _, sel = lax.top_k(score, n_sel)
        c_sel = gather(c_kv, sel)
        logits = jnp.einsum("bqhc,bqkc->bhqk", qb, c_sel,
                            preferred_element_type=jnp.float32) * scale
        dist = q_pos[None, :, None] - sel
        bias = rel_bias[t5_bucket(dist)]
        logits = logits + jnp.moveaxis(bias, -1, 1).astype(jnp.float32)
        logits = jnp.where((dist >= 0)[:, None], logits, -jnp.inf)
        p = jax.nn.softmax(logits, axis=-1).astype(c_sel.dtype)
        return jnp.einsum("bhqk,bqkc->bqhc", p, c_sel)

    out = lax.map(block, (to_blocks(q_lat), to_blocks(q_idx), to_blocks(w_idx),
                          jnp.arange(nblk)))
    return out.swapaxes(0, 1).reshape(b, s, h, c)


def ssd_scan(xh, dt, a, bmat, cmat):
    b, l, g, r, p = xh.shape
    n = bmat.shape[-1]
    q = SSM_CHUNK
    nc = l // q
    f32 = jnp.float32
    x = (xh.astype(f32) * dt[..., None]).reshape(b, nc, q, g, r, p)
    adt = (dt * a).reshape(b, nc, q, g, r)
    bm = bmat.astype(f32).reshape(b, nc, q, g, n)
    cm = cmat.astype(f32).reshape(b, nc, q, g, n)
    a_cum = jnp.cumsum(adt, axis=2)
    tri = jnp.tril(jnp.ones((q, q), dtype=bool))
    seg = a_cum[:, :, :, None] - a_cum[:, :, None, :]
    decay = jnp.exp(jnp.where(tri[None, None, :, :, None, None], seg, -jnp.inf))
    cb = jnp.einsum("bclgn,bcsgn->bclsg", cm, bm)
    y_diag = jnp.einsum("bclsgr,bcsgrp->bclgrp", cb[..., None] * decay, x)
    decay_states = jnp.exp(a_cum[:, :, -1:] - a_cum)
    states = jnp.einsum("bclgn,bclgrp->bcgrpn", bm, x * decay_states[..., None])
    chunk_decay = jnp.exp(a_cum[:, :, -1])

    def step(hstate, inp):
        st, dec = inp
        return hstate * dec[..., None, None] + st, hstate

    h0 = jnp.zeros((b, g, r, p, n), f32)
    _, prev = lax.scan(step, h0, (states.swapaxes(0, 1), chunk_decay.swapaxes(0, 1)))
    prev = prev.swapaxes(0, 1)
    y_off = jnp.einsum("bclgn,bcgrpn->bclgrp", cm, prev) * jnp.exp(a_cum)[..., None]
    return (y_diag + y_off).reshape(b, l, g, r, p)


def gated_group_rmsnorm(y, z, g):
    u = y.astype(jnp.float32) * jax.nn.silu(z.astype(jnp.float32))
    b, l, c = u.shape
    u = u.reshape(b, l, SSM_GROUPS, c // SSM_GROUPS)
    u = u * lax.rsqrt(jnp.mean(u * u, axis=-1, keepdims=True) + EPS)
    return (u.reshape(b, l, c) * g.astype(jnp.float32)).astype(z.dtype)


def setup_inputs(seed: int = 0) -> dict:
    key = jax.random.key(seed)
    ks = jax.random.split(key, 24)
    f32 = jnp.float32

    def nrm(k, shape, scale):
        return jax.random.normal(k, shape, f32) * scale

    def gain(k, shape):
        return 1.0 + 0.05 * jax.random.normal(k, shape, f32)

    dt0 = jnp.exp(jax.random.uniform(ks[8], (DEPTH, SSM_HEADS), f32,
                                     math.log(1e-3), math.log(1e-1)))
    dt_bias = dt0 + jnp.log(-jnp.expm1(-dt0))
    return {
        "x": nrm(ks[0], (BATCH, SEQ, D_MODEL), 1.0),
        "rel_bias": nrm(ks[1], (REL_BUCKETS, ATT_HEADS), 0.5),
        "norm_mix": gain(ks[2], (DEPTH, D_MODEL)),
        "w_in": nrm(ks[3], (DEPTH, D_MODEL, IN_COLS), D_MODEL ** -0.5),
        "kv_norm": gain(ks[4], (DEPTH, KV_LATENT)),
        "w_uk": nrm(ks[5], (DEPTH, ATT_HEADS, ATT_HEAD_DIM, KV_LATENT), ATT_HEAD_DIM ** -0.5),
        "w_uv": nrm(ks[6], (DEPTH, ATT_HEADS, ATT_HEAD_DIM, KV_LATENT), KV_LATENT ** -0.5),
        "conv_ssm_w": nrm(ks[7], (DEPTH, SSM_CONV, SSM_XBC), SSM_CONV ** -0.5),
        "conv_ssm_b": nrm(ks[9], (DEPTH, SSM_XBC), 0.01),
        "dt_bias": dt_bias,
        "a_log": jnp.log(jax.random.uniform(ks[10], (DEPTH, SSM_HEADS), f32, 1.0, 16.0)),
        "d_skip": 1.0 + 0.1 * jax.random.normal(ks[11], (DEPTH, SSM_HEADS), f32),
        "ssm_norm": gain(ks[12], (DEPTH, SSM_INNER)),
        "w_att_out": nrm(ks[13], (DEPTH, ATT_HEADS * ATT_HEAD_DIM, D_MODEL),
                         (ATT_HEADS * ATT_HEAD_DIM) ** -0.5),
        "w_ssm_out": nrm(ks[14], (DEPTH, SSM_INNER, D_MODEL), SSM_INNER ** -0.5),
        "w_out": nrm(ks[15], (DEPTH, D_MODEL, D_MODEL), D_MODEL ** -0.5),
        "norm_ffn": gain(ks[16], (DEPTH, D_MODEL)),
        "w_ffn_up": nrm(ks[17], (DEPTH, D_MODEL, 2 * FFN_DIM), D_MODEL ** -0.5),
        "conv_ffn_w": nrm(ks[18], (DEPTH, FFN_CONV, 2 * FFN_DIM), FFN_CONV ** -0.5),
        "conv_ffn_b": nrm(ks[19], (DEPTH, 2 * FFN_DIM), 0.01),
        "w_ffn_down": nrm(ks[20], (DEPTH, FFN_DIM, D_MODEL), FFN_DIM ** -0.5),
        "norm_final": gain(ks[21], (D_MODEL,)),
    }


def reference(x, rel_bias, norm_mix, w_in, kv_norm, w_uk, w_uv, conv_ssm_w, conv_ssm_b,
              dt_bias, a_log, d_skip, ssm_norm, w_att_out, w_ssm_out, w_out, norm_ffn,
              w_ffn_up, conv_ffn_w, conv_ffn_b, w_ffn_down, norm_final):
    b, s, _ = x.shape
    for layer in range(DEPTH):
        h = rmsnorm(x, norm_mix[layer])
        proj = h @ w_in[layer]
        (q, c_raw, q_idx, k_idx, w_idx, z, xbc, dt_raw, g_att, g_ssm) = split_cols(proj, IN_SPLITS)

        q = q.reshape(b, s, ATT_HEADS, ATT_HEAD_DIM)
        q_lat = jnp.einsum("bshd,hdc->bshc", q, w_uk[layer])
        c_kv = rmsnorm(c_raw, kv_norm[layer])
        att_lat = dsa_attention(q_lat, c_kv,
                                q_idx.reshape(b, s, IDX_HEADS, IDX_DIM), k_idx, w_idx,
                                rel_bias)
        att = jnp.einsum("bshc,hdc->bshd", att_lat, w_uv[layer])
        y_att = att.reshape(b, s, ATT_HEADS * ATT_HEAD_DIM) @ w_att_out[layer]

        xbc = jax.nn.silu(causal_depthwise_conv(xbc, conv_ssm_w[layer], conv_ssm_b[layer]))
        xs, bm, cm = split_cols(xbc, (SSM_INNER, SSM_GROUPS * SSM_STATE, SSM_GROUPS * SSM_STATE))
        dt = jax.nn.softplus(dt_raw.astype(jnp.float32) + dt_bias[layer].astype(jnp.float32))
        a = -jnp.exp(a_log[layer].astype(jnp.float32))
        xh = xs.reshape(b, s, SSM_GROUPS, SSM_HEADS_PER_GROUP, SSM_HEAD_DIM)
        y = ssd_scan(xh,
                     dt.reshape(b, s, SSM_GROUPS, SSM_HEADS_PER_GROUP),
                     a.reshape(SSM_GROUPS, SSM_HEADS_PER_GROUP),
                     bm.reshape(b, s, SSM_GROUPS, SSM_STATE),
                     cm.reshape(b, s, SSM_GROUPS, SSM_STATE))
        y = y + d_skip[layer].astype(jnp.float32).reshape(
            SSM_GROUPS, SSM_HEADS_PER_GROUP)[..., None] * xh.astype(jnp.float32)
        y = gated_group_rmsnorm(y.reshape(b, s, SSM_INNER), z, ssm_norm[layer])
        y_ssm = y @ w_ssm_out[layer]

        merged = jax.nn.sigmoid(g_att) * y_att + jax.nn.sigmoid(g_ssm) * y_ssm
        x = x + merged @ w_out[layer]

        h = rmsnorm(x, norm_ffn[layer])
        u = causal_depthwise_conv(h @ w_ffn_up[layer], conv_ffn_w[layer], conv_ffn_b[layer])
        gate, val = jnp.split(u, 2, axis=-1)
        x = x + (jax.nn.silu(gate) * val) @ w_ffn_down[layer]
    return rmsnorm(x, norm_final)
```

```python
import functools
import math

import numpy as np
import jax
import jax.numpy as jnp
from jax import lax
from jax.experimental import pallas as pl
from jax.experimental.pallas import tpu as pltpu

F32 = jnp.float32
BF16 = jnp.bfloat16
I32 = jnp.int32

D_MODEL = 1024
ATT_HEADS = 16
ATT_HEAD_DIM = 64
KV_LATENT = 128
IDX_HEADS = 8
IDX_DIM = 64
TOPK_MAX = 256
Q_BLOCK = 128
REL_BUCKETS = 32
REL_MAX_DIST = 128
SSM_INNER = 2 * D_MODEL
SSM_HEAD_DIM = 64
SSM_HEADS = SSM_INNER // SSM_HEAD_DIM
SSM_GROUPS = 4
SSM_HEADS_PER_GROUP = SSM_HEADS // SSM_GROUPS
SSM_STATE = 128
SSM_CONV = 4
SSM_CHUNK = 128
SSM_BC = SSM_GROUPS * SSM_STATE
FFN_DIM = 2816
FFN_CONV = 3
EPS = 1e-6

V7X_LANES = 128
V7X_MXU_WIDTH = 256
V7X_VMEM_BYTES = 64 * 1024 * 1024
VMEM_LIMIT = 56 * 1024 * 1024

INT_MIN = -(2 ** 31)
NEG_INF = float("-inf")

COL_Z = 0
COL_XS = COL_Z + SSM_INNER
COL_Q = COL_XS + SSM_INNER
COL_GA = COL_Q + D_MODEL
COL_GS = COL_GA + D_MODEL
COL_QI = COL_GS + D_MODEL
COL_BM = COL_QI + IDX_HEADS * IDX_DIM
COL_CM = COL_BM + SSM_BC
COL_CR = COL_CM + SSM_BC
COL_MISC = COL_CR + KV_LATENT
PROJ_COLS = COL_MISC + V7X_LANES
MISC_KIDX = 0
MISC_WIDX = IDX_DIM
MISC_DT = IDX_DIM + IDX_HEADS


def _pack_w_in(w):
    sizes = (ATT_HEADS * ATT_HEAD_DIM, KV_LATENT, IDX_HEADS * IDX_DIM, IDX_DIM, IDX_HEADS,
             SSM_INNER, SSM_INNER, SSM_BC, SSM_BC, SSM_HEADS, D_MODEL, D_MODEL)
    offs = np.cumsum((0,) + sizes)
    q, c_raw, q_idx, k_idx, w_idx, z, xs, bm, cm, dt, g_att, g_ssm = (
        w[:, offs[k]:offs[k + 1]] for k in range(len(sizes)))
    pad = jnp.zeros((w.shape[0], V7X_LANES - IDX_DIM - IDX_HEADS - SSM_HEADS), w.dtype)
    packed = jnp.concatenate([z, xs, q, g_att, g_ssm, q_idx, bm, cm, c_raw, k_idx, w_idx, dt, pad],
                             axis=1)
    assert packed.shape[1] == PROJ_COLS
    return packed.astype(BF16)


def _rms(x, gain):
    return x * lax.rsqrt(jnp.mean(x * x, axis=-1, keepdims=True) + EPS) * gain


def _sigmoid(x):
    return 1.0 / (1.0 + jnp.exp(-x))


def _bias_tiles_kernel(rb_ref, o_ref):
    h = pl.program_id(0)
    far = rb_ref[REL_BUCKETS - 1, h]
    key = lax.broadcasted_iota(I32, (Q_BLOCK, Q_BLOCK), 0)
    qry = lax.broadcasted_iota(I32, (Q_BLOCK, Q_BLOCK), 1)
    max_exact = REL_BUCKETS // 2
    for delta in range(2):
        n = jnp.maximum(delta * Q_BLOCK + qry - key, 0)
        nf = jnp.maximum(n, 1).astype(F32)
        large = max_exact + (jnp.log(nf / max_exact) / math.log(REL_MAX_DIST / max_exact)
                             * (REL_BUCKETS - max_exact)).astype(I32)
        large = jnp.minimum(large, REL_BUCKETS - 1)
        bucket = jnp.where(n < max_exact, n, large)
        val = jnp.zeros((Q_BLOCK, Q_BLOCK), F32)
        for b in range(REL_BUCKETS):
            val = jnp.where(bucket == b, rb_ref[b, h] - far, val)
        o_ref[delta] = val


def _bias_tiles(rel_bias):
    return pl.pallas_call(
        _bias_tiles_kernel,
        grid=(ATT_HEADS,),
        in_specs=[pl.BlockSpec(memory_space=pltpu.SMEM)],
        out_specs=pl.BlockSpec((2, Q_BLOCK, Q_BLOCK), lambda h: (0, 0, h)),
        out_shape=jax.ShapeDtypeStruct((2, Q_BLOCK, ATT_HEADS * Q_BLOCK), F32),
        name="bias_tiles",
    )(rel_bias)


INPROJ_TM = 1024
INPROJ_TN = 1792


def _inproj_kernel(x_ref, g_ref, w_ref, o_ref, h_s):
    @pl.when(pl.program_id(1) == 0)
    def _():
        h_s[...] = _rms(x_ref[...], g_ref[...]).astype(BF16)

    o_ref[...] = jnp.dot(h_s[...], w_ref[...], preferred_element_type=F32)


def _in_proj(x2d, gain, w_packed):
    t = x2d.shape[0]
    return pl.pallas_call(
        _inproj_kernel,
        grid=(t // INPROJ_TM, PROJ_COLS // INPROJ_TN),
        in_specs=[
            pl.BlockSpec((INPROJ_TM, D_MODEL), lambda i, j: (i, 0)),
            pl.BlockSpec((1, D_MODEL), lambda i, j: (0, 0)),
            pl.BlockSpec((D_MODEL, INPROJ_TN), lambda i, j: (0, j)),
        ],
        out_specs=pl.BlockSpec((INPROJ_TM, INPROJ_TN), lambda i, j: (i, j)),
        out_shape=jax.ShapeDtypeStruct((t, PROJ_COLS), F32),
        scratch_shapes=[pltpu.VMEM((INPROJ_TM, D_MODEL), BF16)],
        compiler_params=pltpu.CompilerParams(
            dimension_semantics=("arbitrary", "arbitrary"), vmem_limit_bytes=VMEM_LIMIT),
        name="in_proj",
    )(x2d, gain, w_packed)


KEY_CLASS = 512
ROW_CHUNK = 256
PAIR_LANES = 2 * Q_BLOCK
assert PAIR_LANES == V7X_MXU_WIDTH


def _attn_block(nkp, i, qiT, wT, tab_ref, wuv_ref, ckv_s, ckvT_s, kidx_s, qT_s, key_s, mask_s,
                j_s, l_s, p_s, attT_s):
    n_sel = float(TOPK_MAX)
    q_pos = i * Q_BLOCK + lax.broadcasted_iota(I32, (ROW_CHUNK, Q_BLOCK), 1)
    row_in_chunk = lax.broadcasted_iota(I32, (ROW_CHUNK, Q_BLOCK), 0)
    n_chunks = nkp // ROW_CHUNK

    for rc in range(n_chunks):
        rows = slice(rc * ROW_CHUNK, (rc + 1) * ROW_CHUNK)
        kc = kidx_s[rows, :]
        acc = jnp.zeros((ROW_CHUNK, Q_BLOCK), F32)
        for hp in range(IDX_HEADS // 2):
            rhs = jnp.concatenate(
                [qiT[(2 * hp) * IDX_DIM:(2 * hp + 1) * IDX_DIM, :],
                 qiT[(2 * hp + 1) * IDX_DIM:(2 * hp + 2) * IDX_DIM, :]], axis=1)
            lg = jnp.dot(kc, rhs, preferred_element_type=F32)
            w0 = wT[MISC_WIDX + 2 * hp:MISC_WIDX + 2 * hp + 1, :]
            w1 = wT[MISC_WIDX + 2 * hp + 1:MISC_WIDX + 2 * hp + 2, :]
            acc = acc + jnp.maximum(lg[:, :Q_BLOCK], 0.0) * w0 + jnp.maximum(lg[:, Q_BLOCK:], 0.0) * w1
        bits = pltpu.bitcast(acc, I32)
        bits = jnp.where(bits == INT_MIN, 0, bits)
        key = jnp.where(bits < 0, bits ^ 0x7FFFFFFF, bits)
        key = jnp.where(rc * ROW_CHUNK + row_in_chunk <= q_pos, key, INT_MIN)
        key_s[rows, :] = key

    def count(pred):
        return jnp.sum(jnp.where(pred, 1.0, 0.0), axis=0, keepdims=True)

    def value_step(bit, thr):
        cand = thr ^ jnp.left_shift(jnp.int32(1), 31 - bit)
        return jnp.where(count(key_s[0:nkp, :] >= cand) >= n_sel, cand, thr)

    thr = lax.fori_loop(0, 32, value_step, jnp.full((1, Q_BLOCK), INT_MIN, I32))
    keys = key_s[0:nkp, :]
    need = n_sel - count(keys > thr)
    split = count(keys == thr) > need

    j_s[...] = jnp.full((1, Q_BLOCK), 2 * nkp, I32)

    @pl.when(jnp.max(jnp.where(split, 1.0, 0.0)) > 0.5)
    def _():
        row = lax.broadcasted_iota(I32, (nkp, Q_BLOCK), 0)
        nbits = (2 * nkp - 1).bit_length()

        def index_step(bit, jmax):
            cand = jmax | jnp.left_shift(jnp.int32(1), nbits - 1 - bit)
            k = key_s[0:nkp, :]
            f = jnp.sum(jnp.where(k == thr, jnp.where(row < cand, 1.0, 0.0), 0.0),
                        axis=0, keepdims=True)
            return jnp.where(f <= need, cand, jmax)

        j_s[...] = lax.fori_loop(0, nbits, index_step, jnp.zeros((1, Q_BLOCK), I32))

    jmax = j_s[...]
    for rc in range(n_chunks):
        rows = slice(rc * ROW_CHUNK, (rc + 1) * ROW_CHUNK)
        k = key_s[rows, :]
        row = rc * ROW_CHUNK + row_in_chunk
        add = jnp.where(k > thr, 0.0,
                        jnp.where(k == thr, jnp.where(row < jmax, 0.0, NEG_INF), NEG_INF))
        mask_s[rows, :] = jnp.where(row <= q_pos, add, NEG_INF)

    n_tiles = nkp // Q_BLOCK
    tiles_per_class = KEY_CLASS // Q_BLOCK

    def head_pair(hp, carry):
        off = pl.multiple_of(hp * PAIR_LANES, PAIR_LANES)
        qt2 = qT_s[:, pl.ds(off, PAIR_LANES)]
        m8 = jnp.full((8, PAIR_LANES), NEG_INF, F32)
        for kc in range(nkp // KEY_CLASS):
            lc = jnp.dot(ckv_s[kc * KEY_CLASS:(kc + 1) * KEY_CLASS, :], qt2,
                         preferred_element_type=F32)
            for tt in range(tiles_per_class):
                t = kc * tiles_per_class + tt
                rows = slice(t * Q_BLOCK, (t + 1) * Q_BLOCK)
                mk = mask_s[rows, :]
                lt = lc[tt * Q_BLOCK:(tt + 1) * Q_BLOCK, :] + jnp.concatenate([mk, mk], axis=1)
                if t >= n_tiles - tiles_per_class - 1:
                    tb0 = tab_ref[0, :, pl.ds(off, PAIR_LANES)]
                    tb1 = tab_ref[1, :, pl.ds(off, PAIR_LANES)]
                    lt = lt + jnp.where(i == t, tb0, jnp.where(i == t + 1, tb1, 0.0))
                l_s[rows, :] = lt
                m8 = jnp.maximum(m8, jnp.max(lt.reshape(Q_BLOCK // 8, 8, PAIR_LANES), axis=0))
        m = jnp.max(m8, axis=0, keepdims=True)
        s8 = jnp.zeros((8, PAIR_LANES), F32)
        for t in range(n_tiles):
            rows = slice(t * Q_BLOCK, (t + 1) * Q_BLOCK)
            p = jnp.exp(l_s[rows, :] - m)
            s8 = s8 + jnp.sum(p.reshape(Q_BLOCK // 8, 8, PAIR_LANES), axis=0)
            p_s[rows, :] = p.astype(BF16)
        denom = jnp.sum(s8, axis=0, keepdims=True)
        ot = jnp.dot(ckvT_s[:, 0:nkp], p_s[0:nkp, :], preferred_element_type=F32) / denom
        for j in range(2):
            oh = ot[:, j * Q_BLOCK:(j + 1) * Q_BLOCK].astype(BF16)
            a = jnp.dot(wuv_ref[2 * hp + j], oh, preferred_element_type=F32)
            r0 = pl.multiple_of((2 * hp + j) * ATT_HEAD_DIM, ATT_HEAD_DIM)
            attT_s[pl.ds(r0, ATT_HEAD_DIM), :] = a
        return carry

    lax.fori_loop(0, ATT_HEADS // 2, head_pair, 0)


def _attn_kernel(q_ref, qi_ref, mq_ref, cr_ref, mk_ref, wukT_ref, wuv_ref, kvn_ref, tab_ref, o_ref,
                 ckv_s, ckvT_s, kidx_s, qT_s, key_s, mask_s, j_s, l_s, p_s, attT_s):
    i = pl.program_id(1)

    @pl.when(i == 0)
    def _():
        c = _rms(cr_ref[...], kvn_ref[...])
        ckv_s[...] = c.astype(BF16)
        ckvT_s[...] = c.T.astype(BF16)
        kidx_s[...] = mk_ref[:, MISC_KIDX:MISC_KIDX + IDX_DIM].astype(BF16)

    scale = ATT_HEAD_DIM ** -0.5
    qb = q_ref[...].astype(BF16)
    for h in range(ATT_HEADS):
        qh = qb[:, h * ATT_HEAD_DIM:(h + 1) * ATT_HEAD_DIM]
        qt = lax.dot_general(wukT_ref[h], qh, (((1,), (1,)), ((), ())), preferred_element_type=F32)
        qT_s[:, h * Q_BLOCK:(h + 1) * Q_BLOCK] = (qt * scale).astype(BF16)

    qiT = qi_ref[...].T.astype(BF16)
    wT = mq_ref[...].T

    n_per_class = KEY_CLASS // Q_BLOCK
    cls = lax.shift_right_logical(i, int(math.log2(n_per_class)))
    n_blocks = pl.num_programs(1)
    for k in range(cr_ref.shape[0] // KEY_CLASS):
        @pl.when(cls == k)
        def _(k=k):
            _attn_block(KEY_CLASS * (k + 1), i, qiT, wT, tab_ref, wuv_ref, ckv_s, ckvT_s, kidx_s,
                        qT_s, key_s, mask_s, j_s, l_s, p_s, attT_s)
    del n_blocks

    o_ref[...] = attT_s[...].T.astype(BF16)


def _dsa_attention(proj, wukT, wuv, kv_norm, tab, batch, seq):
    nqb = seq // Q_BLOCK
    att_dim = ATT_HEADS * ATT_HEAD_DIM
    return pl.pallas_call(
        _attn_kernel,
        grid=(batch, nqb),
        in_specs=[
            pl.BlockSpec((Q_BLOCK, att_dim), lambda b, i: (b * nqb + i, COL_Q // att_dim)),
            pl.BlockSpec((Q_BLOCK, IDX_HEADS * IDX_DIM),
                         lambda b, i: (b * nqb + i, COL_QI // (IDX_HEADS * IDX_DIM))),
            pl.BlockSpec((Q_BLOCK, V7X_LANES), lambda b, i: (b * nqb + i, COL_MISC // V7X_LANES)),
            pl.BlockSpec((seq, KV_LATENT), lambda b, i: (b, COL_CR // KV_LATENT)),
            pl.BlockSpec((seq, V7X_LANES), lambda b, i: (b, COL_MISC // V7X_LANES)),
            pl.BlockSpec((ATT_HEADS, KV_LATENT, ATT_HEAD_DIM), lambda b, i: (0, 0, 0)),
            pl.BlockSpec((ATT_HEADS, ATT_HEAD_DIM, KV_LATENT), lambda b, i: (0, 0, 0)),
            pl.BlockSpec((1, KV_LATENT), lambda b, i: (0, 0)),
            pl.BlockSpec((2, Q_BLOCK, ATT_HEADS * Q_BLOCK), lambda b, i: (0, 0, 0)),
        ],
        out_specs=pl.BlockSpec((Q_BLOCK, att_dim), lambda b, i: (b * nqb + i, 0)),
        out_shape=jax.ShapeDtypeStruct((batch * seq, att_dim), BF16),
        scratch_shapes=[
            pltpu.VMEM((seq, KV_LATENT), BF16),
            pltpu.VMEM((KV_LATENT, seq), BF16),
            pltpu.VMEM((seq, IDX_DIM), BF16),
            pltpu.VMEM((KV_LATENT, ATT_HEADS * Q_BLOCK), BF16),
            pltpu.VMEM((seq, Q_BLOCK), I32),
            pltpu.VMEM((seq, Q_BLOCK), F32),
            pltpu.VMEM((1, Q_BLOCK), I32),
            pltpu.VMEM((seq, PAIR_LANES), F32),
            pltpu.VMEM((seq, PAIR_LANES), BF16),
            pltpu.VMEM((att_dim, Q_BLOCK), F32),
        ],
        compiler_params=pltpu.CompilerParams(
            dimension_semantics=("arbitrary", "arbitrary"), vmem_limit_bytes=VMEM_LIMIT),
        name="dsa_attn",
    )(proj, proj, proj, proj, proj, wukT, wuv, kv_norm, tab)


TAIL = 8
GROUP_LANES = SSM_HEADS_PER_GROUP * SSM_HEAD_DIM


def _expand_matrices():
    e64 = np.zeros((V7X_LANES, SSM_INNER), np.float32)
    e128 = np.zeros((V7X_LANES, SSM_HEADS * SSM_CHUNK), np.float32)
    for h in range(SSM_HEADS):
        for piece in range(3):
            lane = (MISC_DT + h + piece * SSM_HEADS) % V7X_LANES
            e64[lane, h * SSM_HEAD_DIM:(h + 1) * SSM_HEAD_DIM] = 1.0
            e128[lane, h * SSM_CHUNK:(h + 1) * SSM_CHUNK] = 1.0
    return jnp.asarray(e64, BF16), jnp.asarray(e128, BF16)


def _split3(v):
    hi = v.astype(BF16).astype(F32)
    r1 = v - hi
    mid = r1.astype(BF16).astype(F32)
    lo = r1 - mid
    packed = hi + pltpu.roll(mid, SSM_HEADS, axis=1) + pltpu.roll(lo, 2 * SSM_HEADS, axis=1)
    return packed.astype(BF16)


def _conv_silu(u_ref, ext_s, w_ref, b_ref):
    n = SSM_CHUNK
    u = u_ref[...]
    ext_s[TAIL:TAIL + n, :] = u
    out = b_ref[...] + w_ref[SSM_CONV - 1:SSM_CONV, :] * u
    for k in range(1, SSM_CONV):
        out = out + w_ref[SSM_CONV - 1 - k:SSM_CONV - k, :] * ext_s[TAIL - k:TAIL - k + n, :]
    ext_s[0:TAIL, :] = u[n - TAIL:n, :]
    return out * _sigmoid(out)


def _ssd_kernel(z_ref, xs_ref, bm_ref, cm_ref, misc_ref, cwx_ref, cwb_ref, cwc_ref, cbx_ref, cbb_ref,
                cbc_ref, dtb_ref, alog_ref, dsk_ref, gn_ref, e64_ref, e128_ref, o_ref,
                xext_s, bext_s, cext_s, state_s):
    n = SSM_CHUNK

    @pl.when(pl.program_id(1) == 0)
    def _():
        state_s[...] = jnp.zeros_like(state_s)
        xext_s[0:TAIL, :] = jnp.zeros((TAIL, SSM_INNER), F32)
        bext_s[0:TAIL, :] = jnp.zeros((TAIL, SSM_BC), F32)
        cext_s[0:TAIL, :] = jnp.zeros((TAIL, SSM_BC), F32)

    xs = _conv_silu(xs_ref, xext_s, cwx_ref, cbx_ref)
    bm = _conv_silu(bm_ref, bext_s, cwb_ref, cbb_ref)
    cm = _conv_silu(cm_ref, cext_s, cwc_ref, cbc_ref)

    lane = lax.broadcasted_iota(I32, (n, V7X_LANES), 1)
    row = lax.broadcasted_iota(I32, (n, V7X_LANES), 0)
    on_dt = (lane >= MISC_DT) & (lane < MISC_DT + SSM_HEADS)

    v = misc_ref[...] + dtb_ref[...]
    dt = jnp.maximum(v, 0.0) + jnp.log1p(jnp.exp(-jnp.abs(v)))
    dt = jnp.where(on_dt, dt, 0.0)
    a_neg = -jnp.exp(alog_ref[...])
    acum = dt * a_neg
    k = 1
    while k < n:
        acum = acum + jnp.where(row >= k, pltpu.roll(acum, k, axis=0), 0.0)
        k *= 2

    e64 = e64_ref[...]
    dt_e = jnp.dot(_split3(dt), e64, preferred_element_type=F32)
    acum_p = _split3(acum)
    acum_e = jnp.dot(acum_p, e64, preferred_element_type=F32)
    acum_cb = jnp.dot(acum_p, e128_ref[...], preferred_element_type=F32)
    acum_t = acum.T

    last = acum_e[n - 1:n, :]
    xdt = xs * dt_e
    xdt_b = xdt.astype(BF16)
    xds_b = (xdt * jnp.exp(last - acum_e)).astype(BF16)
    chunk_decay = jnp.exp(last)
    ea_e = jnp.exp(acum_e)

    tri = row >= lane
    first_head = lane < SSM_HEAD_DIM
    y_groups = []
    for g in range(SSM_GROUPS):
        gl = slice(g * GROUP_LANES, (g + 1) * GROUP_LANES)
        bmg = bm[:, g * SSM_STATE:(g + 1) * SSM_STATE]
        cmg_b = cm[:, g * SSM_STATE:(g + 1) * SSM_STATE].astype(BF16)
        cb = lax.dot_general(cmg_b, bmg.astype(BF16), (((1,), (1,)), ((), ())),
                             preferred_element_type=F32)
        y_pairs = []
        for pr in range(SSM_HEADS_PER_GROUP // 2):
            h0 = g * SSM_HEADS_PER_GROUP + 2 * pr
            mats = []
            for h in (h0, h0 + 1):
                seg = acum_cb[:, h * n:(h + 1) * n] - acum_t[MISC_DT + h:MISC_DT + h + 1, :]
                mats.append((cb * jnp.exp(jnp.where(tri, seg, NEG_INF))).astype(BF16))
            xp = xdt[:, h0 * SSM_HEAD_DIM:(h0 + 2) * SSM_HEAD_DIM]
            rhs = jnp.concatenate([jnp.where(first_head, xp, 0.0), jnp.where(first_head, 0.0, xp)],
                                  axis=0).astype(BF16)
            y_pairs.append(jnp.dot(jnp.concatenate(mats, axis=1), rhs, preferred_element_type=F32))
        prev = state_s[g]
        y_off = jnp.dot(cmg_b, prev.astype(BF16), preferred_element_type=F32) * ea_e[:, gl]
        state_s[g] = prev * chunk_decay[:, gl] + jnp.dot(bmg.T.astype(BF16), xds_b[:, gl],
                                                         preferred_element_type=F32)
        y_groups.append(jnp.concatenate(y_pairs, axis=1) + y_off)
    del xdt_b

    z = z_ref[...]
    gz = z * _sigmoid(z)
    outs = []
    for g in range(SSM_GROUPS):
        gl = slice(g * GROUP_LANES, (g + 1) * GROUP_LANES)
        u = (y_groups[g] + dsk_ref[:, gl] * xs[:, gl]) * gz[:, gl]
        outs.append(u * lax.rsqrt(jnp.mean(u * u, axis=-1, keepdims=True) + EPS) * gn_ref[:, gl])
    o_ref[...] = jnp.concatenate(outs, axis=1).astype(BF16)


def _ssd(proj, conv_w, conv_b, dt_bias, a_log, d_skip, ssm_norm, batch, seq):
    nch = seq // SSM_CHUNK
    n = SSM_CHUNK
    cwx, cwb, cwc = (conv_w[:, :SSM_INNER], conv_w[:, SSM_INNER:SSM_INNER + SSM_BC],
                     conv_w[:, SSM_INNER + SSM_BC:])
    cbx, cbb, cbc = (conv_b[None, :SSM_INNER], conv_b[None, SSM_INNER:SSM_INNER + SSM_BC],
                     conv_b[None, SSM_INNER + SSM_BC:])
    pad_l, pad_r = MISC_DT, V7X_LANES - MISC_DT - SSM_HEADS
    dtb = jnp.pad(dt_bias, (pad_l, pad_r))[None, :]
    alog = jnp.pad(a_log, (pad_l, pad_r))[None, :]
    dsk = jnp.repeat(d_skip, SSM_HEAD_DIM)[None, :]
    gn = ssm_norm[None, :]
    e64, e128 = _expand_matrices()
    full = lambda shape: pl.BlockSpec(shape, lambda b, c: (0,) * len(shape))
    rows = lambda b, c: b * nch + c
    return pl.pallas_call(
        _ssd_kernel,
        grid=(batch, nch),
        in_specs=[
            pl.BlockSpec((n, SSM_INNER), lambda b, c: (rows(b, c), COL_Z // SSM_INNER)),
            pl.BlockSpec((n, SSM_INNER), lambda b, c: (rows(b, c), COL_XS // SSM_INNER)),
            pl.BlockSpec((n, SSM_BC), lambda b, c: (rows(b, c), COL_BM // SSM_BC)),
            pl.BlockSpec((n, SSM_BC), lambda b, c: (rows(b, c), COL_CM // SSM_BC)),
            pl.BlockSpec((n, V7X_LANES), lambda b, c: (rows(b, c), COL_MISC // V7X_LANES)),
            full((SSM_CONV, SSM_INNER)), full((SSM_CONV, SSM_BC)), full((SSM_CONV, SSM_BC)),
            full((1, SSM_INNER)), full((1, SSM_BC)), full((1, SSM_BC)),
            full((1, V7X_LANES)), full((1, V7X_LANES)),
            full((1, SSM_INNER)), full((1, SSM_INNER)),
            full((V7X_LANES, SSM_INNER)), full((V7X_LANES, SSM_HEADS * n)),
        ],
        out_specs=pl.BlockSpec((n, SSM_INNER), lambda b, c: (rows(b, c), 0)),
        out_shape=jax.ShapeDtypeStruct((batch * seq, SSM_INNER), BF16),
        scratch_shapes=[
            pltpu.VMEM((TAIL + n, SSM_INNER), F32),
            pltpu.VMEM((TAIL + n, SSM_BC), F32),
            pltpu.VMEM((TAIL + n, SSM_BC), F32),
            pltpu.VMEM((SSM_GROUPS, SSM_STATE, GROUP_LANES), F32),
        ],
        compiler_params=pltpu.CompilerParams(
            dimension_semantics=("arbitrary", "arbitrary"), vmem_limit_bytes=VMEM_LIMIT),
        name="ssd",
    )(proj, proj, proj, proj, proj, cwx, cwb, cwc, cbx, cbb, cbc, dtb, alog, dsk, gn, e64, e128)


MERGE_TM = 512


def _merge_kernel(att_ref, yn_ref, ga_ref, gs_ref, x_ref, wao_ref, wso_ref, wo_ref, gf_ref,
                  x1_ref, h2_ref):
    y_att = jnp.dot(att_ref[...], wao_ref[...], preferred_element_type=F32)
    y_ssm = jnp.dot(yn_ref[...], wso_ref[...], preferred_element_type=F32)
    merged = _sigmoid(ga_ref[...]) * y_att + _sigmoid(gs_ref[...]) * y_ssm
    x1 = x_ref[...] + jnp.dot(merged.astype(BF16), wo_ref[...], preferred_element_type=F32)
    x1_ref[...] = x1
    h2_ref[...] = _rms(x1, gf_ref[...]).astype(BF16)


def _merge(att, yn, proj, x2d, wao, wso, wo, norm_ffn):
    t = x2d.shape[0]
    tm = MERGE_TM
    full = lambda shape: pl.BlockSpec(shape, lambda i: (0,) * len(shape))
    return pl.pallas_call(
        _merge_kernel,
        grid=(t // tm,),
        in_specs=[
            pl.BlockSpec((tm, D_MODEL), lambda i: (i, 0)),
            pl.BlockSpec((tm, SSM_INNER), lambda i: (i, 0)),
            pl.BlockSpec((tm, D_MODEL), lambda i: (i, COL_GA // D_MODEL)),
            pl.BlockSpec((tm, D_MODEL), lambda i: (i, COL_GS // D_MODEL)),
            pl.BlockSpec((tm, D_MODEL), lambda i: (i, 0)),
            full((D_MODEL, D_MODEL)), full((SSM_INNER, D_MODEL)), full((D_MODEL, D_MODEL)),
            full((1, D_MODEL)),
        ],
        out_specs=[pl.BlockSpec((tm, D_MODEL), lambda i: (i, 0)),
                   pl.BlockSpec((tm, D_MODEL), lambda i: (i, 0))],
        out_shape=[jax.ShapeDtypeStruct((t, D_MODEL), F32),
                   jax.ShapeDtypeStruct((t, D_MODEL), BF16)],
        compiler_params=pltpu.CompilerParams(
            dimension_semantics=("arbitrary",), vmem_limit_bytes=VMEM_LIMIT),
        name="merge",
    )(att, yn, proj, proj, x2d, wao, wso, wo, norm_ffn)


FFN_TM = 512
FFN_HALO = 16
FFN_FC = V7X_MXU_WIDTH


def _ffn_kernel(h_ref, halo_ref, x1_ref, wup_ref, cw_ref, cb_ref, wdn_ref, gf_ref, o_ref,
                eg_s, ev_s, acc_s, *, tiles_per_seq):
    tm = FFN_TM
    keep = jnp.where(pl.program_id(0) % tiles_per_seq == 0, 0.0, 1.0)
    hcat = jnp.concatenate([halo_ref[...], h_ref[...]], axis=0)
    for c in range(FFN_DIM // FFN_FC):
        acts = []
        for ext_s, base in ((eg_s, 0), (ev_s, FFN_DIM)):
            cols = slice(base + c * FFN_FC, base + (c + 1) * FFN_FC)
            u = jnp.dot(hcat, wup_ref[:, cols], preferred_element_type=F32)
            ext_s[0:FFN_HALO, :] = u[0:FFN_HALO, :] * keep
            ext_s[FFN_HALO:FFN_HALO + tm, :] = u[FFN_HALO:, :]
            out = cb_ref[:, cols] + cw_ref[FFN_CONV - 1:FFN_CONV, cols] * u[FFN_HALO:, :]
            for k in range(1, FFN_CONV):
                out = out + (cw_ref[FFN_CONV - 1 - k:FFN_CONV - k, cols]
                             * ext_s[FFN_HALO - k:FFN_HALO - k + tm, :])
            acts.append(out)
        gate, val = acts
        act = (gate * _sigmoid(gate) * val).astype(BF16)
        contrib = jnp.dot(act, wdn_ref[c * FFN_FC:(c + 1) * FFN_FC, :], preferred_element_type=F32)
        if c == 0:
            acc_s[...] = contrib
        else:
            acc_s[...] += contrib
    o_ref[...] = _rms(x1_ref[...] + acc_s[...], gf_ref[...])


def _ffn(h2, x1, wup, conv_w, conv_b, wdn, norm_final, seq):
    t = h2.shape[0]
    tm = FFN_TM
    halo_blocks = tm // FFN_HALO
    single = pl.Buffered(1)
    full = lambda shape, **kw: pl.BlockSpec(shape, lambda i: (0,) * len(shape), **kw)
    return pl.pallas_call(
        functools.partial(_ffn_kernel, tiles_per_seq=seq // tm),
        grid=(t // tm,),
        in_specs=[
            pl.BlockSpec((tm, D_MODEL), lambda i: (i, 0)),
            pl.BlockSpec((FFN_HALO, D_MODEL), lambda i: (jnp.maximum(i * halo_blocks - 1, 0), 0)),
            pl.BlockSpec((tm, D_MODEL), lambda i: (i, 0)),
            full((D_MODEL, 2 * FFN_DIM), pipeline_mode=single),
            full((FFN_CONV, 2 * FFN_DIM)),
            full((1, 2 * FFN_DIM)),
            full((FFN_DIM, D_MODEL), pipeline_mode=single),
            full((1, D_MODEL)),
        ],
        out_specs=pl.BlockSpec((tm, D_MODEL), lambda i: (i, 0)),
        out_shape=jax.ShapeDtypeStruct((t, D_MODEL), F32),
        scratch_shapes=[
            pltpu.VMEM((FFN_HALO + tm, FFN_FC), F32),
            pltpu.VMEM((FFN_HALO + tm, FFN_FC), F32),
            pltpu.VMEM((tm, D_MODEL), F32),
        ],
        compiler_params=pltpu.CompilerParams(
            dimension_semantics=("arbitrary",), vmem_limit_bytes=VMEM_LIMIT),
        name="ffn",
    )(h2, h2, x1, wup, conv_w, conv_b, wdn, norm_final)


def kernel(x, rel_bias, norm_mix, w_in, kv_norm, w_uk, w_uv, conv_ssm_w, conv_ssm_b, dt_bias, a_log,
           d_skip, ssm_norm, w_att_out, w_ssm_out, w_out, norm_ffn, w_ffn_up, conv_ffn_w, conv_ffn_b,
           w_ffn_down, norm_final):
    batch, seq, _ = x.shape
    assert norm_mix.shape[0] == 1, "single layer"
    assert seq % KEY_CLASS == 0 and min(TOPK_MAX, seq // 4) == TOPK_MAX
    x2d = x.reshape(batch * seq, D_MODEL)

    tab = _bias_tiles(rel_bias)
    proj = _in_proj(x2d, norm_mix[0][None, :], _pack_w_in(w_in[0]))
    att = _dsa_attention(proj, jnp.swapaxes(w_uk[0], 1, 2).astype(BF16), w_uv[0].astype(BF16),
                         kv_norm[0][None, :], tab, batch, seq)
    yn = _ssd(proj, conv_ssm_w[0], conv_ssm_b[0], dt_bias[0], a_log[0], d_skip[0], ssm_norm[0],
              batch, seq)
    x1, h2 = _merge(att, yn, proj, x2d, w_att_out[0].astype(BF16), w_ssm_out[0].astype(BF16),
                    w_out[0].astype(BF16), norm_ffn[0][None, :])
    out = _ffn(h2, x1, w_ffn_up[0].astype(BF16), conv_ffn_w[0], conv_ffn_b[0][None, :],
               w_ffn_down[0].astype(BF16), norm_final[None, :], seq)
    return out.reshape(batch, seq, D_MODEL)
```

```python
import functools
import math

import numpy as np
import jax
import jax.numpy as jnp
from jax import lax
from jax.experimental import pallas as pl
from jax.experimental.pallas import tpu as pltpu

F32 = jnp.float32
BF16 = jnp.bfloat16
I32 = jnp.int32

D_MODEL = 1024
ATT_HEADS = 16
ATT_HEAD_DIM = 64
KV_LATENT = 128
IDX_HEADS = 8
IDX_DIM = 64
TOPK_MAX = 256
Q_BLOCK = 128
REL_BUCKETS = 32
REL_MAX_DIST = 128
SSM_INNER = 2 * D_MODEL
SSM_HEAD_DIM = 64
SSM_HEADS = SSM_INNER // SSM_HEAD_DIM
SSM_GROUPS = 4
SSM_HEADS_PER_GROUP = SSM_HEADS // SSM_GROUPS
SSM_STATE = 128
SSM_CONV = 4
SSM_CHUNK = 128
SSM_BC = SSM_GROUPS * SSM_STATE
FFN_DIM = 2816
FFN_CONV = 3
EPS = 1e-6

V7X_LANES = 128
V7X_MXU_WIDTH = 256
V7X_VMEM_BYTES = 64 * 1024 * 1024
VMEM_LIMIT = 56 * 1024 * 1024

INT_MIN = -(2 ** 31)
NEG_INF = float("-inf")
LOG2E = math.log2(math.e)

COL_Z = 0
COL_XS = COL_Z + SSM_INNER
COL_Q = COL_XS + SSM_INNER
COL_GA = COL_Q + D_MODEL
COL_GS = COL_GA + D_MODEL
COL_QI = COL_GS + D_MODEL
COL_BM = COL_QI + IDX_HEADS * IDX_DIM
COL_CM = COL_BM + SSM_BC
COL_CR = COL_CM + SSM_BC
COL_MISC = COL_CR + KV_LATENT
PROJ_COLS = COL_MISC + V7X_LANES
MISC_KIDX = 0
MISC_WIDX = IDX_DIM
MISC_DT = IDX_DIM + IDX_HEADS


def _pack_w_in(w):
    sizes = (ATT_HEADS * ATT_HEAD_DIM, KV_LATENT, IDX_HEADS * IDX_DIM, IDX_DIM, IDX_HEADS,
             SSM_INNER, SSM_INNER, SSM_BC, SSM_BC, SSM_HEADS, D_MODEL, D_MODEL)
    offs = np.cumsum((0,) + sizes)
    q, c_raw, q_idx, k_idx, w_idx, z, xs, bm, cm, dt, g_att, g_ssm = (
        w[:, offs[k]:offs[k + 1]] for k in range(len(sizes)))
    pad = jnp.zeros((w.shape[0], V7X_LANES - IDX_DIM - IDX_HEADS - SSM_HEADS), w.dtype)
    packed = jnp.concatenate([z, xs, q, g_att, g_ssm, q_idx, bm, cm, c_raw, k_idx, w_idx, dt, pad],
                             axis=1)
    assert packed.shape[1] == PROJ_COLS
    return packed.astype(BF16)


def _rms(x, gain):
    return x * lax.rsqrt(jnp.mean(x * x, axis=-1, keepdims=True) + EPS) * gain


def _sigmoid(x):
    return 1.0 / (1.0 + jnp.exp(-x))


def _bias_tiles_kernel(rb_ref, o_ref):
    h = pl.program_id(0)
    far = rb_ref[REL_BUCKETS - 1, h]
    key = lax.broadcasted_iota(I32, (Q_BLOCK, Q_BLOCK), 0)
    qry = lax.broadcasted_iota(I32, (Q_BLOCK, Q_BLOCK), 1)
    max_exact = REL_BUCKETS // 2
    for delta in range(2):
        n = jnp.maximum(delta * Q_BLOCK + qry - key, 0)
        nf = jnp.maximum(n, 1).astype(F32)
        large = max_exact + (jnp.log(nf / max_exact) / math.log(REL_MAX_DIST / max_exact)
                             * (REL_BUCKETS - max_exact)).astype(I32)
        large = jnp.minimum(large, REL_BUCKETS - 1)
        bucket = jnp.where(n < max_exact, n, large)
        val = jnp.zeros((Q_BLOCK, Q_BLOCK), F32)
        for b in range(REL_BUCKETS):
            val = jnp.where(bucket == b, rb_ref[b, h] - far, val)
        o_ref[delta] = val * LOG2E


def _bias_tiles(rel_bias):
    return pl.pallas_call(
        _bias_tiles_kernel,
        grid=(ATT_HEADS,),
        in_specs=[pl.BlockSpec(memory_space=pltpu.SMEM)],
        out_specs=pl.BlockSpec((2, Q_BLOCK, Q_BLOCK), lambda h: (0, 0, h)),
        out_shape=jax.ShapeDtypeStruct((2, Q_BLOCK, ATT_HEADS * Q_BLOCK), F32),
        name="bias_tiles",
    )(rel_bias)


INPROJ_TM = 1024
INPROJ_TN = 1792


def _inproj_kernel(x_ref, g_ref, w_ref, o_ref, h_s):
    @pl.when(pl.program_id(1) == 0)
    def _():
        h_s[...] = _rms(x_ref[...], g_ref[...]).astype(BF16)

    o_ref[...] = jnp.dot(h_s[...], w_ref[...], preferred_element_type=F32)


def _in_proj(x2d, gain, w_packed):
    t = x2d.shape[0]
    return pl.pallas_call(
        _inproj_kernel,
        grid=(t // INPROJ_TM, PROJ_COLS // INPROJ_TN),
        in_specs=[
            pl.BlockSpec((INPROJ_TM, D_MODEL), lambda i, j: (i, 0)),
            pl.BlockSpec((1, D_MODEL), lambda i, j: (0, 0)),
            pl.BlockSpec((D_MODEL, INPROJ_TN), lambda i, j: (0, j)),
        ],
        out_specs=pl.BlockSpec((INPROJ_TM, INPROJ_TN), lambda i, j: (i, j)),
        out_shape=jax.ShapeDtypeStruct((t, PROJ_COLS), F32),
        scratch_shapes=[pltpu.VMEM((INPROJ_TM, D_MODEL), BF16)],
        compiler_params=pltpu.CompilerParams(
            dimension_semantics=("arbitrary", "arbitrary"), vmem_limit_bytes=VMEM_LIMIT),
        name="in_proj",
    )(x2d, gain, w_packed)


TILES_PER_CLASS = 2
PAD_TILES = TILES_PER_CLASS - 1
PAD_ROWS = PAD_TILES * Q_BLOCK
ROW_CHUNK = 256
COUNT_ROWS = 64
PAIR_LANES = 2 * Q_BLOCK
assert PAIR_LANES == V7X_MXU_WIDTH and ROW_CHUNK == TILES_PER_CLASS * Q_BLOCK
ONES_ROWS = 16
PV_ROWS = KV_LATENT + ONES_ROWS
NEG_BIG = -1e30


def _attn_block(n_tiles, first_tile, i, qiT, wT, tab_ref, wuv_ref, cm_s, ckvT_s, kidx_s, qT_s, key_s,
                j_s, l0_s, l1_s, p0_s, p1_s, attT_s):
    nkp = n_tiles * Q_BLOCK
    n_sel = float(TOPK_MAX)
    start = pl.multiple_of(first_tile * Q_BLOCK, Q_BLOCK)
    key0 = (first_tile - PAD_TILES) * Q_BLOCK
    q_pos = i * Q_BLOCK + lax.broadcasted_iota(I32, (ROW_CHUNK, Q_BLOCK), 1)
    row_in_chunk = lax.broadcasted_iota(I32, (ROW_CHUNK, Q_BLOCK), 0)
    n_chunks = nkp // ROW_CHUNK

    for rc in range(n_chunks):
        rows = slice(rc * ROW_CHUNK, (rc + 1) * ROW_CHUNK)
        kc = kidx_s[pl.ds(pl.multiple_of(start + rc * ROW_CHUNK, Q_BLOCK), ROW_CHUNK), :]
        acc = jnp.zeros((ROW_CHUNK, Q_BLOCK), F32)
        for hp in range(IDX_HEADS // 2):
            rhs = jnp.concatenate(
                [qiT[(2 * hp) * IDX_DIM:(2 * hp + 1) * IDX_DIM, :],
                 qiT[(2 * hp + 1) * IDX_DIM:(2 * hp + 2) * IDX_DIM, :]], axis=1)
            lg = jnp.dot(kc, rhs, preferred_element_type=F32)
            w0 = wT[MISC_WIDX + 2 * hp:MISC_WIDX + 2 * hp + 1, :]
            w1 = wT[MISC_WIDX + 2 * hp + 1:MISC_WIDX + 2 * hp + 2, :]
            acc = acc + jnp.maximum(lg[:, :Q_BLOCK], 0.0) * w0 + jnp.maximum(lg[:, Q_BLOCK:], 0.0) * w1
        bits = pltpu.bitcast(acc, I32)
        bits = jnp.where(bits == INT_MIN, 0, bits)
        key = jnp.where(bits < 0, bits ^ 0x7FFFFFFF, bits)
        key_pos = key0 + rc * ROW_CHUNK + row_in_chunk
        if rc * ROW_CHUNK < PAD_ROWS:
            key = jnp.where(key_pos >= 0, key, INT_MIN)
        if rc == n_chunks - 1:
            key = jnp.where(key_pos <= q_pos, key, INT_MIN)
        key_s[rows, :] = key

    def count_ones(ones):
        part = jnp.sum(ones.reshape(nkp // COUNT_ROWS, COUNT_ROWS, Q_BLOCK), axis=0)
        return jnp.sum(part, axis=0, keepdims=True)

    def count(pred):
        return count_ones(jnp.where(pred, 1.0, 0.0))

    def value_step(bit, thr):
        cand = thr ^ jnp.left_shift(jnp.int32(1), 31 - bit)
        return jnp.where(count(key_s[0:nkp, :] >= cand) >= n_sel, cand, thr)

    thr = lax.fori_loop(0, 32, value_step, jnp.full((1, Q_BLOCK), INT_MIN, I32))
    keys = key_s[0:nkp, :]
    need = n_sel - count(keys > thr)
    split = count(keys == thr) > need

    j_s[...] = jnp.full((1, Q_BLOCK), 2 * nkp, I32)

    @pl.when(jnp.max(jnp.where(split, 1.0, 0.0)) > 0.5)
    def _():
        row = lax.broadcasted_iota(I32, (nkp, Q_BLOCK), 0)
        nbits = (2 * nkp - 1).bit_length()

        def index_step(bit, jmax):
            cand = jmax | jnp.left_shift(jnp.int32(1), nbits - 1 - bit)
            f = count_ones(jnp.where(key_s[0:nkp, :] == thr, jnp.where(row < cand, 1.0, 0.0), 0.0))
            return jnp.where(f <= need, cand, jmax)

        j_s[...] = lax.fori_loop(0, nbits, index_step, jnp.zeros((1, Q_BLOCK), I32))

    jmax = j_s[...]
    row_in_tile = lax.broadcasted_iota(I32, (Q_BLOCK, Q_BLOCK), 0)
    for t in range(n_tiles):
        k = key_s[t * Q_BLOCK:(t + 1) * Q_BLOCK, :]
        row = t * Q_BLOCK + row_in_tile
        add = jnp.where(k > thr, 0.0,
                        jnp.where(k == thr, jnp.where(row < jmax, 0.0, NEG_BIG), NEG_BIG))
        add = jnp.where(k == INT_MIN, NEG_BIG, add)
        cm_s[pl.ds(pl.multiple_of(start + t * Q_BLOCK, Q_BLOCK), Q_BLOCK),
             KV_LATENT:2 * KV_LATENT] = add.astype(BF16)

    def logits_stage(hp, lbuf):
        off = pl.multiple_of(hp * PAIR_LANES, PAIR_LANES)
        rhs = qT_s[:, pl.ds(off, PAIR_LANES)]
        m8 = jnp.full((8, PAIR_LANES), NEG_INF, F32)
        for rc in range(n_chunks):
            lc = jnp.dot(cm_s[pl.ds(pl.multiple_of(start + rc * ROW_CHUNK, Q_BLOCK), ROW_CHUNK), :],
                         rhs, preferred_element_type=F32)
            for tt in range(ROW_CHUNK // Q_BLOCK):
                t = rc * (ROW_CHUNK // Q_BLOCK) + tt
                lt = lc[tt * Q_BLOCK:(tt + 1) * Q_BLOCK, :]
                if t >= n_tiles - 2:
                    lt = lt + tab_ref[n_tiles - 1 - t, :, pl.ds(off, PAIR_LANES)]
                lbuf[t * Q_BLOCK:(t + 1) * Q_BLOCK, :] = lt
                m8 = jnp.maximum(m8, jnp.max(lt.reshape(Q_BLOCK // 8, 8, PAIR_LANES), axis=0))
        return jnp.max(m8, axis=0, keepdims=True)

    def value_stage(hp, lbuf, pbuf, m):
        for t in range(n_tiles):
            rows = slice(t * Q_BLOCK, (t + 1) * Q_BLOCK)
            pbuf[rows, :] = jnp.exp2(lbuf[rows, :] - m).astype(BF16)
        ot = jnp.dot(ckvT_s[:, pl.ds(start, nkp)], pbuf[0:nkp, :], preferred_element_type=F32)
        ot = ot[0:KV_LATENT, :] / ot[KV_LATENT:KV_LATENT + 1, :]
        for j in range(2):
            oh = ot[:, j * Q_BLOCK:(j + 1) * Q_BLOCK].astype(BF16)
            a = jnp.dot(wuv_ref[2 * hp + j], oh, preferred_element_type=F32)
            r0 = pl.multiple_of((2 * hp + j) * ATT_HEAD_DIM, ATT_HEAD_DIM)
            attT_s[pl.ds(r0, ATT_HEAD_DIM), :] = a

    n_pairs = ATT_HEADS // 2

    def two_pairs(jj, m_even):
        hp = 2 * jj
        m_odd = logits_stage(hp + 1, l1_s)
        value_stage(hp, l0_s, p0_s, m_even)
        m_next = logits_stage(hp + 2, l0_s)
        value_stage(hp + 1, l1_s, p1_s, m_odd)
        return m_next

    m_even = lax.fori_loop(0, n_pairs // 2 - 1, two_pairs, logits_stage(0, l0_s))
    m_odd = logits_stage(n_pairs - 1, l1_s)
    value_stage(n_pairs - 2, l0_s, p0_s, m_even)
    value_stage(n_pairs - 1, l1_s, p1_s, m_odd)


def _attn_kernel(q_ref, qi_ref, mq_ref, cr_ref, mk_ref, wukT_ref, wuv_ref, kvn_ref, tab_ref, o_ref,
                 cm_s, ckvT_s, kidx_s, qT_s, key_s, j_s, l0_s, l1_s, p0_s, p1_s, attT_s):
    i = pl.program_id(1)
    seq = cr_ref.shape[0]

    @pl.when(i == 0)
    def _():
        c = _rms(cr_ref[...], kvn_ref[...])
        cm_s[0:PAD_ROWS, :] = jnp.zeros((PAD_ROWS, 2 * KV_LATENT), BF16)
        cm_s[PAD_ROWS:, 0:KV_LATENT] = c.astype(BF16)
        ckvT_s[0:KV_LATENT, 0:PAD_ROWS] = jnp.zeros((KV_LATENT, PAD_ROWS), BF16)
        ckvT_s[0:KV_LATENT, PAD_ROWS:] = c.T.astype(BF16)
        ckvT_s[KV_LATENT:, :] = jnp.ones((ONES_ROWS, seq + PAD_ROWS), BF16)
        kidx_s[0:PAD_ROWS, :] = jnp.zeros((PAD_ROWS, IDX_DIM), BF16)
        kidx_s[PAD_ROWS:, :] = mk_ref[:, MISC_KIDX:MISC_KIDX + IDX_DIM].astype(BF16)
        eye = jnp.where(lax.broadcasted_iota(I32, (Q_BLOCK, Q_BLOCK), 0)
                        == lax.broadcasted_iota(I32, (Q_BLOCK, Q_BLOCK), 1), 1.0, 0.0).astype(BF16)
        for h in range(ATT_HEADS):
            qT_s[KV_LATENT:, h * Q_BLOCK:(h + 1) * Q_BLOCK] = eye

    scale = ATT_HEAD_DIM ** -0.5 * LOG2E
    qb = q_ref[...].astype(BF16)
    for h in range(ATT_HEADS):
        qh = qb[:, h * ATT_HEAD_DIM:(h + 1) * ATT_HEAD_DIM]
        qt = lax.dot_general(wukT_ref[h], qh, (((1,), (1,)), ((), ())), preferred_element_type=F32)
        qT_s[0:KV_LATENT, h * Q_BLOCK:(h + 1) * Q_BLOCK] = (qt * scale).astype(BF16)

    qiT = qi_ref[...].T.astype(BF16)
    wT = mq_ref[...].T

    cls = lax.shift_right_logical(i, int(math.log2(TILES_PER_CLASS)))
    for k in range(seq // Q_BLOCK // TILES_PER_CLASS):
        @pl.when(cls == k)
        def _(k=k):
            _attn_block(TILES_PER_CLASS * (k + 1), i - TILES_PER_CLASS * k, i, qiT, wT, tab_ref,
                        wuv_ref, cm_s, ckvT_s, kidx_s, qT_s, key_s, j_s, l0_s, l1_s, p0_s, p1_s, attT_s)

    o_ref[...] = attT_s[...].T.astype(BF16)


def _dsa_attention(proj, wukT, wuv, kv_norm, tab, batch, seq):
    nqb = seq // Q_BLOCK
    att_dim = ATT_HEADS * ATT_HEAD_DIM
    return pl.pallas_call(
        _attn_kernel,
        grid=(batch, nqb),
        in_specs=[
            pl.BlockSpec((Q_BLOCK, att_dim), lambda b, i: (b * nqb + i, COL_Q // att_dim)),
            pl.BlockSpec((Q_BLOCK, IDX_HEADS * IDX_DIM),
                         lambda b, i: (b * nqb + i, COL_QI // (IDX_HEADS * IDX_DIM))),
            pl.BlockSpec((Q_BLOCK, V7X_LANES), lambda b, i: (b * nqb + i, COL_MISC // V7X_LANES)),
            pl.BlockSpec((seq, KV_LATENT), lambda b, i: (b, COL_CR // KV_LATENT)),
            pl.BlockSpec((seq, V7X_LANES), lambda b, i: (b, COL_MISC // V7X_LANES)),
            pl.BlockSpec((ATT_HEADS, KV_LATENT, ATT_HEAD_DIM), lambda b, i: (0, 0, 0)),
            pl.BlockSpec((ATT_HEADS, ATT_HEAD_DIM, KV_LATENT), lambda b, i: (0, 0, 0)),
            pl.BlockSpec((1, KV_LATENT), lambda b, i: (0, 0)),
            pl.BlockSpec((2, Q_BLOCK, ATT_HEADS * Q_BLOCK), lambda b, i: (0, 0, 0)),
        ],
        out_specs=pl.BlockSpec((Q_BLOCK, att_dim), lambda b, i: (b * nqb + i, 0)),
        out_shape=jax.ShapeDtypeStruct((batch * seq, att_dim), BF16),
        scratch_shapes=[
            pltpu.VMEM((seq + PAD_ROWS, 2 * KV_LATENT), BF16),
            pltpu.VMEM((PV_ROWS, seq + PAD_ROWS), BF16),
            pltpu.VMEM((seq + PAD_ROWS, IDX_DIM), BF16),
            pltpu.VMEM((2 * KV_LATENT, ATT_HEADS * Q_BLOCK), BF16),
            pltpu.VMEM((seq, Q_BLOCK), I32),
            pltpu.VMEM((1, Q_BLOCK), I32),
            pltpu.VMEM((seq, PAIR_LANES), F32),
            pltpu.VMEM((seq, PAIR_LANES), F32),
            pltpu.VMEM((seq, PAIR_LANES), BF16),
            pltpu.VMEM((seq, PAIR_LANES), BF16),
            pltpu.VMEM((att_dim, Q_BLOCK), F32),
        ],
        compiler_params=pltpu.CompilerParams(
            dimension_semantics=("arbitrary", "arbitrary"), vmem_limit_bytes=VMEM_LIMIT),
        name="dsa_attn",
    )(proj, proj, proj, proj, proj, wukT, wuv, kv_norm, tab)


TAIL = 8
GROUP_LANES = SSM_HEADS_PER_GROUP * SSM_HEAD_DIM


def _expand_matrices():
    e64 = np.zeros((V7X_LANES, SSM_INNER), np.float32)
    e128 = np.zeros((V7X_LANES, SSM_HEADS * SSM_CHUNK), np.float32)
    for h in range(SSM_HEADS):
        for piece in range(3):
            lane = (MISC_DT + h + piece * SSM_HEADS) % V7X_LANES
            e64[lane, h * SSM_HEAD_DIM:(h + 1) * SSM_HEAD_DIM] = 1.0
            e128[lane, h * SSM_CHUNK:(h + 1) * SSM_CHUNK] = 1.0
    return jnp.asarray(e64, BF16), jnp.asarray(e128, BF16)


def _split3(v):
    hi = v.astype(BF16).astype(F32)
    r1 = v - hi
    mid = r1.astype(BF16).astype(F32)
    lo = r1 - mid
    packed = hi + pltpu.roll(mid, SSM_HEADS, axis=1) + pltpu.roll(lo, 2 * SSM_HEADS, axis=1)
    return packed.astype(BF16)


def _conv_silu(u_ref, ext_s, w_ref, b_ref):
    n = SSM_CHUNK
    u = u_ref[...]
    ext_s[TAIL:TAIL + n, :] = u
    out = b_ref[...] + w_ref[SSM_CONV - 1:SSM_CONV, :] * u
    for k in range(1, SSM_CONV):
        out = out + w_ref[SSM_CONV - 1 - k:SSM_CONV - k, :] * ext_s[TAIL - k:TAIL - k + n, :]
    ext_s[0:TAIL, :] = u[n - TAIL:n, :]
    return out * _sigmoid(out)


def _ssd_kernel(z_ref, xs_ref, bm_ref, cm_ref, misc_ref, cwx_ref, cwb_ref, cwc_ref, cbx_ref, cbb_ref,
                cbc_ref, dtb_ref, alog_ref, dsk_ref, gn_ref, e64_ref, e128_ref, o_ref,
                xext_s, bext_s, cext_s, state_s):
    n = SSM_CHUNK

    @pl.when(pl.program_id(1) == 0)
    def _():
        state_s[...] = jnp.zeros_like(state_s)
        xext_s[0:TAIL, :] = jnp.zeros((TAIL, SSM_INNER), F32)
        bext_s[0:TAIL, :] = jnp.zeros((TAIL, SSM_BC), F32)
        cext_s[0:TAIL, :] = jnp.zeros((TAIL, SSM_BC), F32)

    xs = _conv_silu(xs_ref, xext_s, cwx_ref, cbx_ref)
    bm = _conv_silu(bm_ref, bext_s, cwb_ref, cbb_ref)
    cm = _conv_silu(cm_ref, cext_s, cwc_ref, cbc_ref)

    lane = lax.broadcasted_iota(I32, (n, V7X_LANES), 1)
    row = lax.broadcasted_iota(I32, (n, V7X_LANES), 0)
    on_dt = (lane >= MISC_DT) & (lane < MISC_DT + SSM_HEADS)

    v = misc_ref[...] + dtb_ref[...]
    dt = jnp.maximum(v, 0.0) + jnp.log1p(jnp.exp(-jnp.abs(v)))
    dt = jnp.where(on_dt, dt, 0.0)
    a_neg = -jnp.exp(alog_ref[...])
    acum = dt * a_neg
    k = 1
    while k < n:
        acum = acum + jnp.where(row >= k, pltpu.roll(acum, k, axis=0), 0.0)
        k *= 2

    e64 = e64_ref[...]
    dt_e = jnp.dot(_split3(dt), e64, preferred_element_type=F32)
    acum_p = _split3(acum)
    acum_e = jnp.dot(acum_p, e64, preferred_element_type=F32)
    acum_cb = jnp.dot(acum_p, e128_ref[...], preferred_element_type=F32)
    acum_t = acum.T

    last = acum_e[n - 1:n, :]
    xdt = xs * dt_e
    xds_b = (xdt * jnp.exp(last - acum_e)).astype(BF16)
    chunk_decay = jnp.exp(last)
    ea_e = jnp.exp(acum_e)

    tri = row >= lane
    first_head = lane < SSM_HEAD_DIM
    y_groups = []
    for g in range(SSM_GROUPS):
        gl = slice(g * GROUP_LANES, (g + 1) * GROUP_LANES)
        bmg = bm[:, g * SSM_STATE:(g + 1) * SSM_STATE]
        cmg_b = cm[:, g * SSM_STATE:(g + 1) * SSM_STATE].astype(BF16)
        cb = lax.dot_general(cmg_b, bmg.astype(BF16), (((1,), (1,)), ((), ())),
                             preferred_element_type=F32)
        y_pairs = []
        for pr in range(SSM_HEADS_PER_GROUP // 2):
            h0 = g * SSM_HEADS_PER_GROUP + 2 * pr
            mats = []
            for h in (h0, h0 + 1):
                seg = acum_cb[:, h * n:(h + 1) * n] - acum_t[MISC_DT + h:MISC_DT + h + 1, :]
                mats.append((cb * jnp.exp(jnp.where(tri, seg, NEG_INF))).astype(BF16))
            xp = xdt[:, h0 * SSM_HEAD_DIM:(h0 + 2) * SSM_HEAD_DIM]
            rhs = jnp.concatenate([jnp.where(first_head, xp, 0.0), jnp.where(first_head, 0.0, xp)],
                                  axis=0).astype(BF16)
            y_pairs.append(jnp.dot(jnp.concatenate(mats, axis=1), rhs, preferred_element_type=F32))
        prev = state_s[g]
        y_off = jnp.dot(cmg_b, prev.astype(BF16), preferred_element_type=F32) * ea_e[:, gl]
        state_s[g] = prev * chunk_decay[:, gl] + jnp.dot(bmg.T.astype(BF16), xds_b[:, gl],
                                                         preferred_element_type=F32)
        y_groups.append(jnp.concatenate(y_pairs, axis=1) + y_off)

    z = z_ref[...]
    gz = z * _sigmoid(z)
    outs = []
    for g in range(SSM_GROUPS):
        gl = slice(g * GROUP_LANES, (g + 1) * GROUP_LANES)
        u = (y_groups[g] + dsk_ref[:, gl] * xs[:, gl]) * gz[:, gl]
        outs.append(u * lax.rsqrt(jnp.mean(u * u, axis=-1, keepdims=True) + EPS) * gn_ref[:, gl])
    o_ref[...] = jnp.concatenate(outs, axis=1).astype(BF16)


def _ssd(proj, conv_w, conv_b, dt_bias, a_log, d_skip, ssm_norm, batch, seq):
    nch = seq // SSM_CHUNK
    n = SSM_CHUNK
    cwx, cwb, cwc = (conv_w[:, :SSM_INNER], conv_w[:, SSM_INNER:SSM_INNER + SSM_BC],
                     conv_w[:, SSM_INNER + SSM_BC:])
    cbx, cbb, cbc = (conv_b[None, :SSM_INNER], conv_b[None, SSM_INNER:SSM_INNER + SSM_BC],
                     conv_b[None, SSM_INNER + SSM_BC:])
    pad_l, pad_r = MISC_DT, V7X_LANES - MISC_DT - SSM_HEADS
    dtb = jnp.pad(dt_bias, (pad_l, pad_r))[None, :]
    alog = jnp.pad(a_log, (pad_l, pad_r))[None, :]
    dsk = jnp.repeat(d_skip, SSM_HEAD_DIM)[None, :]
    gn = ssm_norm[None, :]
    e64, e128 = _expand_matrices()
    full = lambda shape: pl.BlockSpec(shape, lambda b, c: (0,) * len(shape))
    rows = lambda b, c: b * nch + c
    return pl.pallas_call(
        _ssd_kernel,
        grid=(batch, nch),
        in_specs=[
            pl.BlockSpec((n, SSM_INNER), lambda b, c: (rows(b, c), COL_Z // SSM_INNER)),
            pl.BlockSpec((n, SSM_INNER), lambda b, c: (rows(b, c), COL_XS // SSM_INNER)),
            pl.BlockSpec((n, SSM_BC), lambda b, c: (rows(b, c), COL_BM // SSM_BC)),
            pl.BlockSpec((n, SSM_BC), lambda b, c: (rows(b, c), COL_CM // SSM_BC)),
            pl.BlockSpec((n, V7X_LANES), lambda b, c: (rows(b, c), COL_MISC // V7X_LANES)),
            full((SSM_CONV, SSM_INNER)), full((SSM_CONV, SSM_BC)), full((SSM_CONV, SSM_BC)),
            full((1, SSM_INNER)), full((1, SSM_BC)), full((1, SSM_BC)),
            full((1, V7X_LANES)), full((1, V7X_LANES)),
            full((1, SSM_INNER)), full((1, SSM_INNER)),
            full((V7X_LANES, SSM_INNER)), full((V7X_LANES, SSM_HEADS * n)),
        ],
        out_specs=pl.BlockSpec((n, SSM_INNER), lambda b, c: (rows(b, c), 0)),
        out_shape=jax.ShapeDtypeStruct((batch * seq, SSM_INNER), BF16),
        scratch_shapes=[
            pltpu.VMEM((TAIL + n, SSM_INNER), F32),
            pltpu.VMEM((TAIL + n, SSM_BC), F32),
            pltpu.VMEM((TAIL + n, SSM_BC), F32),
            pltpu.VMEM((SSM_GROUPS, SSM_STATE, GROUP_LANES), F32),
        ],
        compiler_params=pltpu.CompilerParams(
            dimension_semantics=("arbitrary", "arbitrary"), vmem_limit_bytes=VMEM_LIMIT),
        name="ssd",
    )(proj, proj, proj, proj, proj, cwx, cwb, cwc, cbx, cbb, cbc, dtb, alog, dsk, gn, e64, e128)


MERGE_TM = 512


def _merge_kernel(att_ref, yn_ref, ga_ref, gs_ref, x_ref, wao_ref, wso_ref, wo_ref, gf_ref,
                  x1_ref, h2_ref):
    y_att = jnp.dot(att_ref[...], wao_ref[...], preferred_element_type=F32)
    y_ssm = jnp.dot(yn_ref[...], wso_ref[...], preferred_element_type=F32)
    merged = _sigmoid(ga_ref[...]) * y_att + _sigmoid(gs_ref[...]) * y_ssm
    x1 = x_ref[...] + jnp.dot(merged.astype(BF16), wo_ref[...], preferred_element_type=F32)
    x1_ref[...] = x1
    h2_ref[...] = _rms(x1, gf_ref[...]).astype(BF16)


def _merge(att, yn, proj, x2d, wao, wso, wo, norm_ffn):
    t = x2d.shape[0]
    tm = MERGE_TM
    full = lambda shape: pl.BlockSpec(shape, lambda i: (0,) * len(shape))
    return pl.pallas_call(
        _merge_kernel,
        grid=(t // tm,),
        in_specs=[
            pl.BlockSpec((tm, D_MODEL), lambda i: (i, 0)),
            pl.BlockSpec((tm, SSM_INNER), lambda i: (i, 0)),
            pl.BlockSpec((tm, D_MODEL), lambda i: (i, COL_GA // D_MODEL)),
            pl.BlockSpec((tm, D_MODEL), lambda i: (i, COL_GS // D_MODEL)),
            pl.BlockSpec((tm, D_MODEL), lambda i: (i, 0)),
            full((D_MODEL, D_MODEL)), full((SSM_INNER, D_MODEL)), full((D_MODEL, D_MODEL)),
            full((1, D_MODEL)),
        ],
        out_specs=[pl.BlockSpec((tm, D_MODEL), lambda i: (i, 0)),
                   pl.BlockSpec((tm, D_MODEL), lambda i: (i, 0))],
        out_shape=[jax.ShapeDtypeStruct((t, D_MODEL), F32),
                   jax.ShapeDtypeStruct((t, D_MODEL), BF16)],
        compiler_params=pltpu.CompilerParams(
            dimension_semantics=("arbitrary",), vmem_limit_bytes=VMEM_LIMIT),
        name="merge",
    )(att, yn, proj, proj, x2d, wao, wso, wo, norm_ffn)


FFN_TM = 512
FFN_HALO = 16
FFN_FC = V7X_MXU_WIDTH


def _ffn_kernel(h_ref, halo_ref, x1_ref, wup_ref, cw_ref, cb_ref, wdn_ref, gf_ref, o_ref,
                eg_s, ev_s, acc_s, *, tiles_per_seq):
    tm = FFN_TM
    keep = jnp.where(pl.program_id(0) % tiles_per_seq == 0, 0.0, 1.0)
    hcat = jnp.concatenate([halo_ref[...], h_ref[...]], axis=0)
    for c in range(FFN_DIM // FFN_FC):
        acts = []
        for ext_s, base in ((eg_s, 0), (ev_s, FFN_DIM)):
            cols = slice(base + c * FFN_FC, base + (c + 1) * FFN_FC)
            u = jnp.dot(hcat, wup_ref[:, cols], preferred_element_type=F32)
            ext_s[0:FFN_HALO, :] = u[0:FFN_HALO, :] * keep
            ext_s[FFN_HALO:FFN_HALO + tm, :] = u[FFN_HALO:, :]
            out = cb_ref[:, cols] + cw_ref[FFN_CONV - 1:FFN_CONV, cols] * u[FFN_HALO:, :]
            for k in range(1, FFN_CONV):
                out = out + (cw_ref[FFN_CONV - 1 - k:FFN_CONV - k, cols]
                             * ext_s[FFN_HALO - k:FFN_HALO - k + tm, :])
            acts.append(out)
        gate, val = acts
        act = (gate * _sigmoid(gate) * val).astype(BF16)
        contrib = jnp.dot(act, wdn_ref[c * FFN_FC:(c + 1) * FFN_FC, :], preferred_element_type=F32)
        if c == 0:
            acc_s[...] = contrib
        else:
            acc_s[...] += contrib
    o_ref[...] = _rms(x1_ref[...] + acc_s[...], gf_ref[...])


def _ffn(h2, x1, wup, conv_w, conv_b, wdn, norm_final, seq):
    t = h2.shape[0]
    tm = FFN_TM
    halo_blocks = tm // FFN_HALO
    single = pl.Buffered(1)
    full = lambda shape, **kw: pl.BlockSpec(shape, lambda i: (0,) * len(shape), **kw)
    return pl.pallas_call(
        functools.partial(_ffn_kernel, tiles_per_seq=seq // tm),
        grid=(t // tm,),
        in_specs=[
            pl.BlockSpec((tm, D_MODEL), lambda i: (i, 0)),
            pl.BlockSpec((FFN_HALO, D_MODEL), lambda i: (jnp.maximum(i * halo_blocks - 1, 0), 0)),
            pl.BlockSpec((tm, D_MODEL), lambda i: (i, 0)),
            full((D_MODEL, 2 * FFN_DIM), pipeline_mode=single),
            full((FFN_CONV, 2 * FFN_DIM)),
            full((1, 2 * FFN_DIM)),
            full((FFN_DIM, D_MODEL), pipeline_mode=single),
            full((1, D_MODEL)),
        ],
        out_specs=pl.BlockSpec((tm, D_MODEL), lambda i: (i, 0)),
        out_shape=jax.ShapeDtypeStruct((t, D_MODEL), F32),
        scratch_shapes=[
            pltpu.VMEM((FFN_HALO + tm, FFN_FC), F32),
            pltpu.VMEM((FFN_HALO + tm, FFN_FC), F32),
            pltpu.VMEM((tm, D_MODEL), F32),
        ],
        compiler_params=pltpu.CompilerParams(
            dimension_semantics=("arbitrary",), vmem_limit_bytes=VMEM_LIMIT),
        name="ffn",
    )(h2, h2, x1, wup, conv_w, conv_b, wdn, norm_final)


def kernel(x, rel_bias, norm_mix, w_in, kv_norm, w_uk, w_uv, conv_ssm_w, conv_ssm_b, dt_bias, a_log,
           d_skip, ssm_norm, w_att_out, w_ssm_out, w_out, norm_ffn, w_ffn_up, conv_ffn_w, conv_ffn_b,
           w_ffn_down, norm_final):
    batch, seq, _ = x.shape
    assert norm_mix.shape[0] == 1, "single layer"
    assert seq % (TILES_PER_CLASS * Q_BLOCK) == 0 and min(TOPK_MAX, seq // 4) == TOPK_MAX
    x2d = x.reshape(batch * seq, D_MODEL)

    tab = _bias_tiles(rel_bias)
    proj = _in_proj(x2d, norm_mix[0][None, :], _pack_w_in(w_in[0]))
    att = _dsa_attention(proj, jnp.swapaxes(w_uk[0], 1, 2).astype(BF16), w_uv[0].astype(BF16),
                         kv_norm[0][None, :], tab, batch, seq)
    yn = _ssd(proj, conv_ssm_w[0], conv_ssm_b[0], dt_bias[0], a_log[0], d_skip[0], ssm_norm[0],
              batch, seq)
    x1, h2 = _merge(att, yn, proj, x2d, w_att_out[0].astype(BF16), w_ssm_out[0].astype(BF16),
                    w_out[0].astype(BF16), norm_ffn[0][None, :])
    out = _ffn(h2, x1, w_ffn_up[0].astype(BF16), conv_ffn_w[0], conv_ffn_b[0][None, :],
               w_ffn_down[0].astype(BF16), norm_final[None, :], seq)
    return out.reshape(batch, seq, D_MODEL)
```

```python
import functools
import math

import numpy as np
import jax
import jax.numpy as jnp
from jax import lax
from jax.experimental import pallas as pl
from jax.experimental.pallas import tpu as pltpu

F32 = jnp.float32
BF16 = jnp.bfloat16
I32 = jnp.int32

D_MODEL = 1024
ATT_HEADS = 16
ATT_HEAD_DIM = 64
KV_LATENT = 128
IDX_HEADS = 8
IDX_DIM = 64
TOPK_MAX = 256
Q_BLOCK = 128
REL_BUCKETS = 32
REL_MAX_DIST = 128
SSM_INNER = 2 * D_MODEL
SSM_HEAD_DIM = 64
SSM_HEADS = SSM_INNER // SSM_HEAD_DIM
SSM_GROUPS = 4
SSM_HEADS_PER_GROUP = SSM_HEADS // SSM_GROUPS
SSM_STATE = 128
SSM_CONV = 4
SSM_CHUNK = 128
SSM_BC = SSM_GROUPS * SSM_STATE
FFN_DIM = 2816
FFN_CONV = 3
EPS = 1e-6

V7X_LANES = 128
V7X_MXU_WIDTH = 256
V7X_VMEM_BYTES = 64 * 1024 * 1024
VMEM_LIMIT = 56 * 1024 * 1024

INT_MIN = -(2 ** 31)
NEG_INF = float("-inf")
LOG2E = math.log2(math.e)

COL_Z = 0
COL_XS = COL_Z + SSM_INNER
COL_Q = COL_XS + SSM_INNER
COL_GA = COL_Q + D_MODEL
COL_GS = COL_GA + D_MODEL
COL_QI = COL_GS + D_MODEL
COL_BM = COL_QI + IDX_HEADS * IDX_DIM
COL_CM = COL_BM + SSM_BC
COL_CR = COL_CM + SSM_BC
COL_MISC = COL_CR + KV_LATENT
PROJ_COLS = COL_MISC + V7X_LANES
MISC_KIDX = 0
MISC_WIDX = IDX_DIM
MISC_DT = IDX_DIM + IDX_HEADS


def _pack_moves():
    sizes = (ATT_HEADS * ATT_HEAD_DIM, KV_LATENT, IDX_HEADS * IDX_DIM, IDX_DIM, IDX_HEADS,
             SSM_INNER, SSM_INNER, SSM_BC, SSM_BC, SSM_HEADS, D_MODEL, D_MODEL)
    src = np.cumsum((0,) + sizes)
    dst = (COL_Q, COL_CR, COL_QI, COL_MISC + MISC_KIDX, COL_MISC + MISC_WIDX, COL_Z, COL_XS, COL_BM,
           COL_CM, COL_MISC + MISC_DT, COL_GA, COL_GS)
    return [(int(s), int(d), int(n)) for s, d, n in zip(src[:-1], dst, sizes)]


PACK_ROWS = 128


def _pack_kernel(w_ref, o_ref):
    misc = []
    for s, d, n in _pack_moves():
        if d >= COL_MISC:
            assert d == COL_MISC + sum(p.shape[1] for p in misc)
            misc.append(w_ref[:, s:s + n])
        else:
            o_ref[:, d:d + n] = w_ref[:, s:s + n].astype(BF16)
    used = sum(p.shape[1] for p in misc)
    misc.append(jnp.zeros((PACK_ROWS, V7X_LANES - used), F32))
    o_ref[:, COL_MISC:PROJ_COLS] = jnp.concatenate(misc, axis=1).astype(BF16)


def _pack_w_in(w):
    k, cols = w.shape
    return pl.pallas_call(
        _pack_kernel,
        grid=(k // PACK_ROWS,),
        in_specs=[pl.BlockSpec((PACK_ROWS, cols), lambda i: (i, 0))],
        out_specs=pl.BlockSpec((PACK_ROWS, PROJ_COLS), lambda i: (i, 0)),
        out_shape=jax.ShapeDtypeStruct((k, PROJ_COLS), BF16),
        name="pack_w_in",
    )(w)


def _rms(x, gain):
    return x * lax.rsqrt(jnp.mean(x * x, axis=-1, keepdims=True) + EPS) * gain


def _sigmoid(x):
    return 1.0 / (1.0 + jnp.exp(-x))


def _bias_tiles_kernel(rb_ref, o_ref):
    h = pl.program_id(0)
    far = rb_ref[REL_BUCKETS - 1, h]
    key = lax.broadcasted_iota(I32, (Q_BLOCK, Q_BLOCK), 0)
    qry = lax.broadcasted_iota(I32, (Q_BLOCK, Q_BLOCK), 1)
    max_exact = REL_BUCKETS // 2
    for delta in range(2):
        n = jnp.maximum(delta * Q_BLOCK + qry - key, 0)
        nf = jnp.maximum(n, 1).astype(F32)
        large = max_exact + (jnp.log(nf / max_exact) / math.log(REL_MAX_DIST / max_exact)
                             * (REL_BUCKETS - max_exact)).astype(I32)
        large = jnp.minimum(large, REL_BUCKETS - 1)
        bucket = jnp.where(n < max_exact, n, large)
        val = jnp.zeros((Q_BLOCK, Q_BLOCK), F32)
        for b in range(REL_BUCKETS):
            val = jnp.where(bucket == b, rb_ref[b, h] - far, val)
        o_ref[delta] = val * LOG2E


def _bias_tiles(rel_bias):
    return pl.pallas_call(
        _bias_tiles_kernel,
        grid=(ATT_HEADS,),
        in_specs=[pl.BlockSpec(memory_space=pltpu.SMEM)],
        out_specs=pl.BlockSpec((2, Q_BLOCK, Q_BLOCK), lambda h: (0, 0, h)),
        out_shape=jax.ShapeDtypeStruct((2, Q_BLOCK, ATT_HEADS * Q_BLOCK), F32),
        name="bias_tiles",
    )(rel_bias)


INPROJ_TM = 256


def _inproj_kernel(x_ref, g_ref, w_ref, o_ref):
    h = _rms(x_ref[...], g_ref[...]).astype(BF16)
    o_ref[...] = jnp.dot(h, w_ref[...], preferred_element_type=F32)


def _in_proj(x2d, gain, w_packed):
    t = x2d.shape[0]
    return pl.pallas_call(
        _inproj_kernel,
        grid=(t // INPROJ_TM,),
        in_specs=[
            pl.BlockSpec((INPROJ_TM, D_MODEL), lambda i: (i, 0)),
            pl.BlockSpec((1, D_MODEL), lambda i: (0, 0)),
            pl.BlockSpec((D_MODEL, PROJ_COLS), lambda i: (0, 0), pipeline_mode=pl.Buffered(1)),
        ],
        out_specs=pl.BlockSpec((INPROJ_TM, PROJ_COLS), lambda i: (i, 0)),
        out_shape=jax.ShapeDtypeStruct((t, PROJ_COLS), F32),
        compiler_params=pltpu.CompilerParams(
            dimension_semantics=("arbitrary",), vmem_limit_bytes=VMEM_LIMIT),
        name="in_proj",
    )(x2d, gain, w_packed)


TILES_PER_CLASS = 2
PAD_TILES = TILES_PER_CLASS - 1
PAD_ROWS = PAD_TILES * Q_BLOCK
ROW_CHUNK = 256
COUNT_ROWS = 64
PAIR_LANES = 2 * Q_BLOCK
assert PAIR_LANES == V7X_MXU_WIDTH and ROW_CHUNK == TILES_PER_CLASS * Q_BLOCK
ONES_ROWS = 16
PV_ROWS = KV_LATENT + ONES_ROWS
NEG_BIG = -1e30


def _attn_block(n_tiles, first_tile, i, qiT, wT, tab_ref, wuv_ref, cm_s, ckvT_s, kidx_s, qT_s, key_s,
                j_s, l0_s, l1_s, p0_s, p1_s, attT_s):
    nkp = n_tiles * Q_BLOCK
    n_sel = float(TOPK_MAX)
    start = pl.multiple_of(first_tile * Q_BLOCK, Q_BLOCK)
    key0 = (first_tile - PAD_TILES) * Q_BLOCK
    q_pos = i * Q_BLOCK + lax.broadcasted_iota(I32, (ROW_CHUNK, Q_BLOCK), 1)
    row_in_chunk = lax.broadcasted_iota(I32, (ROW_CHUNK, Q_BLOCK), 0)
    n_chunks = nkp // ROW_CHUNK

    for rc in range(n_chunks):
        rows = slice(rc * ROW_CHUNK, (rc + 1) * ROW_CHUNK)
        kc = kidx_s[pl.ds(pl.multiple_of(start + rc * ROW_CHUNK, Q_BLOCK), ROW_CHUNK), :]
        acc = jnp.zeros((ROW_CHUNK, Q_BLOCK), F32)
        for hp in range(IDX_HEADS // 2):
            rhs = jnp.concatenate(
                [qiT[(2 * hp) * IDX_DIM:(2 * hp + 1) * IDX_DIM, :],
                 qiT[(2 * hp + 1) * IDX_DIM:(2 * hp + 2) * IDX_DIM, :]], axis=1)
            lg = jnp.dot(kc, rhs, preferred_element_type=F32)
            w0 = wT[MISC_WIDX + 2 * hp:MISC_WIDX + 2 * hp + 1, :]
            w1 = wT[MISC_WIDX + 2 * hp + 1:MISC_WIDX + 2 * hp + 2, :]
            acc = acc + jnp.maximum(lg[:, :Q_BLOCK], 0.0) * w0 + jnp.maximum(lg[:, Q_BLOCK:], 0.0) * w1
        bits = pltpu.bitcast(acc, I32)
        bits = jnp.where(bits == INT_MIN, 0, bits)
        key = jnp.where(bits < 0, bits ^ 0x7FFFFFFF, bits)
        key_pos = key0 + rc * ROW_CHUNK + row_in_chunk
        if rc * ROW_CHUNK < PAD_ROWS:
            key = jnp.where(key_pos >= 0, key, INT_MIN)
        if rc == n_chunks - 1:
            key = jnp.where(key_pos <= q_pos, key, INT_MIN)
        key_s[rows, :] = key

    def count_ones(ones):
        part = jnp.sum(ones.reshape(nkp // COUNT_ROWS, COUNT_ROWS, Q_BLOCK), axis=0)
        return jnp.sum(part, axis=0, keepdims=True)

    def count(pred):
        return count_ones(jnp.where(pred, 1.0, 0.0))

    def value_step(bit, thr):
        cand = thr ^ jnp.left_shift(jnp.int32(1), 31 - bit)
        return jnp.where(count(key_s[0:nkp, :] >= cand) >= n_sel, cand, thr)

    thr = lax.fori_loop(0, 32, value_step, jnp.full((1, Q_BLOCK), INT_MIN, I32))
    keys = key_s[0:nkp, :]
    need = n_sel - count(keys > thr)
    split = count(keys == thr) > need

    j_s[...] = jnp.full((1, Q_BLOCK), 2 * nkp, I32)

    @pl.when(jnp.max(jnp.where(split, 1.0, 0.0)) > 0.5)
    def _():
        row = lax.broadcasted_iota(I32, (nkp, Q_BLOCK), 0)
        nbits = (2 * nkp - 1).bit_length()

        def index_step(bit, jmax):
            cand = jmax | jnp.left_shift(jnp.int32(1), nbits - 1 - bit)
            f = count_ones(jnp.where(key_s[0:nkp, :] == thr, jnp.where(row < cand, 1.0, 0.0), 0.0))
            return jnp.where(f <= need, cand, jmax)

        j_s[...] = lax.fori_loop(0, nbits, index_step, jnp.zeros((1, Q_BLOCK), I32))

    jmax = j_s[...]
    row_in_tile = lax.broadcasted_iota(I32, (Q_BLOCK, Q_BLOCK), 0)
    for t in range(n_tiles):
        k = key_s[t * Q_BLOCK:(t + 1) * Q_BLOCK, :]
        row = t * Q_BLOCK + row_in_tile
        add = jnp.where(k > thr, 0.0,
                        jnp.where(k == thr, jnp.where(row < jmax, 0.0, NEG_BIG), NEG_BIG))
        add = jnp.where(k == INT_MIN, NEG_BIG, add)
        cm_s[pl.ds(pl.multiple_of(start + t * Q_BLOCK, Q_BLOCK), Q_BLOCK),
             KV_LATENT:2 * KV_LATENT] = add.astype(BF16)

    def logits_stage(hp, lbuf):
        off = pl.multiple_of(hp * PAIR_LANES, PAIR_LANES)
        rhs = qT_s[:, pl.ds(off, PAIR_LANES)]
        m8 = jnp.full((8, PAIR_LANES), NEG_INF, F32)
        for rc in range(n_chunks):
            lc = jnp.dot(cm_s[pl.ds(pl.multiple_of(start + rc * ROW_CHUNK, Q_BLOCK), ROW_CHUNK), :],
                         rhs, preferred_element_type=F32)
            for tt in range(ROW_CHUNK // Q_BLOCK):
                t = rc * (ROW_CHUNK // Q_BLOCK) + tt
                lt = lc[tt * Q_BLOCK:(tt + 1) * Q_BLOCK, :]
                if t >= n_tiles - 2:
                    lt = lt + tab_ref[n_tiles - 1 - t, :, pl.ds(off, PAIR_LANES)]
                lbuf[t * Q_BLOCK:(t + 1) * Q_BLOCK, :] = lt
                m8 = jnp.maximum(m8, jnp.max(lt.reshape(Q_BLOCK // 8, 8, PAIR_LANES), axis=0))
        return jnp.max(m8, axis=0, keepdims=True)

    def value_stage(hp, lbuf, pbuf, m):
        for t in range(n_tiles):
            rows = slice(t * Q_BLOCK, (t + 1) * Q_BLOCK)
            pbuf[rows, :] = jnp.exp2(lbuf[rows, :] - m).astype(BF16)
        ot = jnp.dot(ckvT_s[:, pl.ds(start, nkp)], pbuf[0:nkp, :], preferred_element_type=F32)
        ot = ot[0:KV_LATENT, :] / ot[KV_LATENT:KV_LATENT + 1, :]
        for j in range(2):
            oh = ot[:, j * Q_BLOCK:(j + 1) * Q_BLOCK].astype(BF16)
            a = jnp.dot(wuv_ref[2 * hp + j], oh, preferred_element_type=F32)
            r0 = pl.multiple_of((2 * hp + j) * ATT_HEAD_DIM, ATT_HEAD_DIM)
            attT_s[pl.ds(r0, ATT_HEAD_DIM), :] = a

    n_pairs = ATT_HEADS // 2

    def two_pairs(jj, m_even):
        hp = 2 * jj
        m_odd = logits_stage(hp + 1, l1_s)
        value_stage(hp, l0_s, p0_s, m_even)
        m_next = logits_stage(hp + 2, l0_s)
        value_stage(hp + 1, l1_s, p1_s, m_odd)
        return m_next

    m_even = lax.fori_loop(0, n_pairs // 2 - 1, two_pairs, logits_stage(0, l0_s))
    m_odd = logits_stage(n_pairs - 1, l1_s)
    value_stage(n_pairs - 2, l0_s, p0_s, m_even)
    value_stage(n_pairs - 1, l1_s, p1_s, m_odd)


def _attn_kernel(q_ref, qi_ref, mq_ref, cr_ref, mk_ref, wukT_ref, wuv_ref, kvn_ref, tab_ref, o_ref,
                 cm_s, ckvT_s, kidx_s, qT_s, key_s, j_s, l0_s, l1_s, p0_s, p1_s, attT_s):
    i = pl.program_id(1)
    seq = cr_ref.shape[0]

    @pl.when(i == 0)
    def _():
        c = _rms(cr_ref[...], kvn_ref[...])
        cm_s[0:PAD_ROWS, :] = jnp.zeros((PAD_ROWS, 2 * KV_LATENT), BF16)
        cm_s[PAD_ROWS:, 0:KV_LATENT] = c.astype(BF16)
        ckvT_s[0:KV_LATENT, 0:PAD_ROWS] = jnp.zeros((KV_LATENT, PAD_ROWS), BF16)
        ckvT_s[0:KV_LATENT, PAD_ROWS:] = c.T.astype(BF16)
        ckvT_s[KV_LATENT:, :] = jnp.ones((ONES_ROWS, seq + PAD_ROWS), BF16)
        kidx_s[0:PAD_ROWS, :] = jnp.zeros((PAD_ROWS, IDX_DIM), BF16)
        kidx_s[PAD_ROWS:, :] = mk_ref[:, MISC_KIDX:MISC_KIDX + IDX_DIM].astype(BF16)
        eye = jnp.where(lax.broadcasted_iota(I32, (Q_BLOCK, Q_BLOCK), 0)
                        == lax.broadcasted_iota(I32, (Q_BLOCK, Q_BLOCK), 1), 1.0, 0.0).astype(BF16)
        for h in range(ATT_HEADS):
            qT_s[KV_LATENT:, h * Q_BLOCK:(h + 1) * Q_BLOCK] = eye

    scale = ATT_HEAD_DIM ** -0.5 * LOG2E
    qb = q_ref[...].astype(BF16)
    for h in range(ATT_HEADS):
        qh = qb[:, h * ATT_HEAD_DIM:(h + 1) * ATT_HEAD_DIM]
        qt = lax.dot_general(wukT_ref[h], qh, (((1,), (1,)), ((), ())), preferred_element_type=F32)
        qT_s[0:KV_LATENT, h * Q_BLOCK:(h + 1) * Q_BLOCK] = (qt * scale).astype(BF16)

    qiT = qi_ref[...].T.astype(BF16)
    wT = mq_ref[...].T

    cls = lax.shift_right_logical(i, int(math.log2(TILES_PER_CLASS)))
    for k in range(seq // Q_BLOCK // TILES_PER_CLASS):
        @pl.when(cls == k)
        def _(k=k):
            _attn_block(TILES_PER_CLASS * (k + 1), i - TILES_PER_CLASS * k, i, qiT, wT, tab_ref,
                        wuv_ref, cm_s, ckvT_s, kidx_s, qT_s, key_s, j_s, l0_s, l1_s, p0_s, p1_s, attT_s)

    o_ref[...] = attT_s[...].T.astype(BF16)


def _dsa_attention(proj, wukT, wuv, kv_norm, tab, batch, seq):
    nqb = seq // Q_BLOCK
    att_dim = ATT_HEADS * ATT_HEAD_DIM
    return pl.pallas_call(
        _attn_kernel,
        grid=(batch, nqb),
        in_specs=[
            pl.BlockSpec((Q_BLOCK, att_dim), lambda b, i: (b * nqb + i, COL_Q // att_dim)),
            pl.BlockSpec((Q_BLOCK, IDX_HEADS * IDX_DIM),
                         lambda b, i: (b * nqb + i, COL_QI // (IDX_HEADS * IDX_DIM))),
            pl.BlockSpec((Q_BLOCK, V7X_LANES), lambda b, i: (b * nqb + i, COL_MISC // V7X_LANES)),
            pl.BlockSpec((seq, KV_LATENT), lambda b, i: (b, COL_CR // KV_LATENT)),
            pl.BlockSpec((seq, V7X_LANES), lambda b, i: (b, COL_MISC // V7X_LANES)),
            pl.BlockSpec((ATT_HEADS, KV_LATENT, ATT_HEAD_DIM), lambda b, i: (0, 0, 0)),
            pl.BlockSpec((ATT_HEADS, ATT_HEAD_DIM, KV_LATENT), lambda b, i: (0, 0, 0)),
            pl.BlockSpec((1, KV_LATENT), lambda b, i: (0, 0)),
            pl.BlockSpec((2, Q_BLOCK, ATT_HEADS * Q_BLOCK), lambda b, i: (0, 0, 0)),
        ],
        out_specs=pl.BlockSpec((Q_BLOCK, att_dim), lambda b, i: (b * nqb + i, 0)),
        out_shape=jax.ShapeDtypeStruct((batch * seq, att_dim), BF16),
        scratch_shapes=[
            pltpu.VMEM((seq + PAD_ROWS, 2 * KV_LATENT), BF16),
            pltpu.VMEM((PV_ROWS, seq + PAD_ROWS), BF16),
            pltpu.VMEM((seq + PAD_ROWS, IDX_DIM), BF16),
            pltpu.VMEM((2 * KV_LATENT, ATT_HEADS * Q_BLOCK), BF16),
            pltpu.VMEM((seq, Q_BLOCK), I32),
            pltpu.VMEM((1, Q_BLOCK), I32),
            pltpu.VMEM((seq, PAIR_LANES), F32),
            pltpu.VMEM((seq, PAIR_LANES), F32),
            pltpu.VMEM((seq, PAIR_LANES), BF16),
            pltpu.VMEM((seq, PAIR_LANES), BF16),
            pltpu.VMEM((att_dim, Q_BLOCK), F32),
        ],
        compiler_params=pltpu.CompilerParams(
            dimension_semantics=("arbitrary", "arbitrary"), vmem_limit_bytes=VMEM_LIMIT),
        name="dsa_attn",
    )(proj, proj, proj, proj, proj, wukT, wuv, kv_norm, tab)


TAIL = 8
GROUP_LANES = SSM_HEADS_PER_GROUP * SSM_HEAD_DIM


def _expand_matrices():
    e64 = np.zeros((V7X_LANES, SSM_INNER), np.float32)
    e128 = np.zeros((V7X_LANES, SSM_HEADS * SSM_CHUNK), np.float32)
    for h in range(SSM_HEADS):
        for piece in range(3):
            lane = (MISC_DT + h + piece * SSM_HEADS) % V7X_LANES
            e64[lane, h * SSM_HEAD_DIM:(h + 1) * SSM_HEAD_DIM] = 1.0
            e128[lane, h * SSM_CHUNK:(h + 1) * SSM_CHUNK] = 1.0
    return jnp.asarray(e64, BF16), jnp.asarray(e128, BF16)


def _split3(v):
    hi = v.astype(BF16).astype(F32)
    r1 = v - hi
    mid = r1.astype(BF16).astype(F32)
    lo = r1 - mid
    packed = hi + pltpu.roll(mid, SSM_HEADS, axis=1) + pltpu.roll(lo, 2 * SSM_HEADS, axis=1)
    return packed.astype(BF16)


def _conv_silu(u_ref, tail_s, w_ref, b_ref):
    n = SSM_CHUNK
    u = u_ref[...]
    ext = jnp.concatenate([tail_s[...], u], axis=0)
    out = b_ref[...] + w_ref[SSM_CONV - 1:SSM_CONV, :] * u
    for k in range(1, SSM_CONV):
        out = out + w_ref[SSM_CONV - 1 - k:SSM_CONV - k, :] * pltpu.roll(ext, k, axis=0)[TAIL:, :]
    tail_s[...] = u[n - TAIL:n, :]
    return out * _sigmoid(out)


def _ssd_kernel(z_ref, xs_ref, bm_ref, cm_ref, misc_ref, cwx_ref, cwb_ref, cwc_ref, cbx_ref, cbb_ref,
                cbc_ref, dtb_ref, alog_ref, dsk_ref, gn_ref, e64_ref, e128_ref, o_ref,
                xtail_s, btail_s, ctail_s, state_s):
    n = SSM_CHUNK

    @pl.when(pl.program_id(1) == 0)
    def _():
        state_s[...] = jnp.zeros_like(state_s)
        xtail_s[...] = jnp.zeros_like(xtail_s)
        btail_s[...] = jnp.zeros_like(btail_s)
        ctail_s[...] = jnp.zeros_like(ctail_s)

    xs = _conv_silu(xs_ref, xtail_s, cwx_ref, cbx_ref)
    bm = _conv_silu(bm_ref, btail_s, cwb_ref, cbb_ref)
    cm = _conv_silu(cm_ref, ctail_s, cwc_ref, cbc_ref)

    lane = lax.broadcasted_iota(I32, (n, V7X_LANES), 1)
    row = lax.broadcasted_iota(I32, (n, V7X_LANES), 0)
    on_dt = (lane >= MISC_DT) & (lane < MISC_DT + SSM_HEADS)

    v = misc_ref[...] + dtb_ref[...]
    dt = jnp.maximum(v, 0.0) + jnp.log1p(jnp.exp(-jnp.abs(v)))
    dt = jnp.where(on_dt, dt, 0.0)
    a_neg = -jnp.exp(alog_ref[...])
    acum = dt * a_neg
    k = 1
    while k < n:
        acum = acum + jnp.where(row >= k, pltpu.roll(acum, k, axis=0), 0.0)
        k *= 2

    e64 = e64_ref[...]
    dt_e = jnp.dot(_split3(dt), e64, preferred_element_type=F32)
    acum_p = _split3(acum)
    acum_e = jnp.dot(acum_p, e64, preferred_element_type=F32)
    acum_cb = jnp.dot(acum_p, e128_ref[...], preferred_element_type=F32)
    acum_t = acum.T

    last = acum_e[n - 1:n, :]
    xdt = xs * dt_e
    xds_b = (xdt * jnp.exp(last - acum_e)).astype(BF16)
    chunk_decay = jnp.exp(last)
    ea_e = jnp.exp(acum_e)

    tri = row >= lane
    first_head = lane < SSM_HEAD_DIM
    y_groups = []
    for g in range(SSM_GROUPS):
        gl = slice(g * GROUP_LANES, (g + 1) * GROUP_LANES)
        bmg = bm[:, g * SSM_STATE:(g + 1) * SSM_STATE]
        cmg_b = cm[:, g * SSM_STATE:(g + 1) * SSM_STATE].astype(BF16)
        cb = lax.dot_general(cmg_b, bmg.astype(BF16), (((1,), (1,)), ((), ())),
                             preferred_element_type=F32)
        y_pairs = []
        for pr in range(SSM_HEADS_PER_GROUP // 2):
            h0 = g * SSM_HEADS_PER_GROUP + 2 * pr
            mats = []
            for h in (h0, h0 + 1):
                seg = acum_cb[:, h * n:(h + 1) * n] - acum_t[MISC_DT + h:MISC_DT + h + 1, :]
                mats.append((cb * jnp.exp(jnp.where(tri, seg, NEG_INF))).astype(BF16))
            xp = xdt[:, h0 * SSM_HEAD_DIM:(h0 + 2) * SSM_HEAD_DIM]
            rhs = jnp.concatenate([jnp.where(first_head, xp, 0.0), jnp.where(first_head, 0.0, xp)],
                                  axis=0).astype(BF16)
            y_pairs.append(jnp.dot(jnp.concatenate(mats, axis=1), rhs, preferred_element_type=F32))
        prev = state_s[g]
        y_off = jnp.dot(cmg_b, prev.astype(BF16), preferred_element_type=F32) * ea_e[:, gl]
        state_s[g] = prev * chunk_decay[:, gl] + jnp.dot(bmg.T.astype(BF16), xds_b[:, gl],
                                                         preferred_element_type=F32)
        y_groups.append(jnp.concatenate(y_pairs, axis=1) + y_off)

    z = z_ref[...]
    gz = z * _sigmoid(z)
    outs = []
    for g in range(SSM_GROUPS):
        gl = slice(g * GROUP_LANES, (g + 1) * GROUP_LANES)
        u = (y_groups[g] + dsk_ref[:, gl] * xs[:, gl]) * gz[:, gl]
        outs.append(u * lax.rsqrt(jnp.mean(u * u, axis=-1, keepdims=True) + EPS) * gn_ref[:, gl])
    o_ref[...] = jnp.concatenate(outs, axis=1).astype(BF16)


def _ssd(proj, conv_w, conv_b, dt_bias, a_log, d_skip, ssm_norm, batch, seq):
    nch = seq // SSM_CHUNK
    n = SSM_CHUNK
    cwx, cwb, cwc = (conv_w[:, :SSM_INNER], conv_w[:, SSM_INNER:SSM_INNER + SSM_BC],
                     conv_w[:, SSM_INNER + SSM_BC:])
    cbx, cbb, cbc = (conv_b[None, :SSM_INNER], conv_b[None, SSM_INNER:SSM_INNER + SSM_BC],
                     conv_b[None, SSM_INNER + SSM_BC:])
    pad_l, pad_r = MISC_DT, V7X_LANES - MISC_DT - SSM_HEADS
    dtb = jnp.pad(dt_bias, (pad_l, pad_r))[None, :]
    alog = jnp.pad(a_log, (pad_l, pad_r))[None, :]
    dsk = jnp.repeat(d_skip, SSM_HEAD_DIM)[None, :]
    gn = ssm_norm[None, :]
    e64, e128 = _expand_matrices()
    full = lambda shape: pl.BlockSpec(shape, lambda b, c: (0,) * len(shape))
    rows = lambda b, c: b * nch + c
    return pl.pallas_call(
        _ssd_kernel,
        grid=(batch, nch),
        in_specs=[
            pl.BlockSpec((n, SSM_INNER), lambda b, c: (rows(b, c), COL_Z // SSM_INNER)),
            pl.BlockSpec((n, SSM_INNER), lambda b, c: (rows(b, c), COL_XS // SSM_INNER)),
            pl.BlockSpec((n, SSM_BC), lambda b, c: (rows(b, c), COL_BM // SSM_BC)),
            pl.BlockSpec((n, SSM_BC), lambda b, c: (rows(b, c), COL_CM // SSM_BC)),
            pl.BlockSpec((n, V7X_LANES), lambda b, c: (rows(b, c), COL_MISC // V7X_LANES)),
            full((SSM_CONV, SSM_INNER)), full((SSM_CONV, SSM_BC)), full((SSM_CONV, SSM_BC)),
            full((1, SSM_INNER)), full((1, SSM_BC)), full((1, SSM_BC)),
            full((1, V7X_LANES)), full((1, V7X_LANES)),
            full((1, SSM_INNER)), full((1, SSM_INNER)),
            full((V7X_LANES, SSM_INNER)), full((V7X_LANES, SSM_HEADS * n)),
        ],
        out_specs=pl.BlockSpec((n, SSM_INNER), lambda b, c: (rows(b, c), 0)),
        out_shape=jax.ShapeDtypeStruct((batch * seq, SSM_INNER), BF16),
        scratch_shapes=[
            pltpu.VMEM((TAIL, SSM_INNER), F32),
            pltpu.VMEM((TAIL, SSM_BC), F32),
            pltpu.VMEM((TAIL, SSM_BC), F32),
            pltpu.VMEM((SSM_GROUPS, SSM_STATE, GROUP_LANES), F32),
        ],
        compiler_params=pltpu.CompilerParams(
            dimension_semantics=("arbitrary", "arbitrary"), vmem_limit_bytes=VMEM_LIMIT),
        name="ssd",
    )(proj, proj, proj, proj, proj, cwx, cwb, cwc, cbx, cbb, cbc, dtb, alog, dsk, gn, e64, e128)


MERGE_TM = 512


def _merge_kernel(att_ref, yn_ref, ga_ref, gs_ref, x_ref, wao_ref, wso_ref, wo_ref, gf_ref,
                  x1_ref, h2_ref):
    y_att = jnp.dot(att_ref[...], wao_ref[...], preferred_element_type=F32)
    y_ssm = jnp.dot(yn_ref[...], wso_ref[...], preferred_element_type=F32)
    merged = _sigmoid(ga_ref[...]) * y_att + _sigmoid(gs_ref[...]) * y_ssm
    x1 = x_ref[...] + jnp.dot(merged.astype(BF16), wo_ref[...], preferred_element_type=F32)
    x1_ref[...] = x1
    h2_ref[...] = _rms(x1, gf_ref[...]).astype(BF16)


def _merge(att, yn, proj, x2d, wao, wso, wo, norm_ffn):
    t = x2d.shape[0]
    tm = MERGE_TM
    full = lambda shape: pl.BlockSpec(shape, lambda i: (0,) * len(shape))
    return pl.pallas_call(
        _merge_kernel,
        grid=(t // tm,),
        in_specs=[
            pl.BlockSpec((tm, D_MODEL), lambda i: (i, 0)),
            pl.BlockSpec((tm, SSM_INNER), lambda i: (i, 0)),
            pl.BlockSpec((tm, D_MODEL), lambda i: (i, COL_GA // D_MODEL)),
            pl.BlockSpec((tm, D_MODEL), lambda i: (i, COL_GS // D_MODEL)),
            pl.BlockSpec((tm, D_MODEL), lambda i: (i, 0)),
            full((D_MODEL, D_MODEL)), full((SSM_INNER, D_MODEL)), full((D_MODEL, D_MODEL)),
            full((1, D_MODEL)),
        ],
        out_specs=[pl.BlockSpec((tm, D_MODEL), lambda i: (i, 0)),
                   pl.BlockSpec((tm, D_MODEL), lambda i: (i, 0))],
        out_shape=[jax.ShapeDtypeStruct((t, D_MODEL), F32),
                   jax.ShapeDtypeStruct((t, D_MODEL), BF16)],
        compiler_params=pltpu.CompilerParams(
            dimension_semantics=("arbitrary",), vmem_limit_bytes=VMEM_LIMIT),
        name="merge",
    )(att, yn, proj, proj, x2d, wao, wso, wo, norm_ffn)


FFN_TM = 512
FFN_HALO = 16
FFN_FC = V7X_MXU_WIDTH


def _ffn_kernel(h_ref, halo_ref, x1_ref, wup_ref, cw_ref, cb_ref, wdn_ref, gf_ref, o_ref,
                acc_s, *, tiles_per_seq):
    tm = FFN_TM
    keep = jnp.where(pl.program_id(0) % tiles_per_seq == 0, 0.0, 1.0)
    hcat = jnp.concatenate([halo_ref[...], h_ref[...]], axis=0)
    n_chunks = FFN_DIM // FFN_FC

    def up(c, base):
        return jnp.dot(hcat, wup_ref[:, base + c * FFN_FC:base + (c + 1) * FFN_FC],
                       preferred_element_type=F32)

    def conv(u, c, base):
        cols = slice(base + c * FFN_FC, base + (c + 1) * FFN_FC)
        u = jnp.concatenate([u[0:FFN_HALO, :] * keep, u[FFN_HALO:, :]], axis=0)
        out = cb_ref[:, cols] + cw_ref[FFN_CONV - 1:FFN_CONV, cols] * u[FFN_HALO:, :]
        for k in range(1, FFN_CONV):
            out = out + (cw_ref[FFN_CONV - 1 - k:FFN_CONV - k, cols]
                         * pltpu.roll(u, k, axis=0)[FFN_HALO:, :])
        return out

    ug, uv = up(0, 0), up(0, FFN_DIM)
    for c in range(n_chunks):
        last = c + 1 == n_chunks
        ug_next = None if last else up(c + 1, 0)
        gate = conv(ug, c, 0)
        uv_next = None if last else up(c + 1, FFN_DIM)
        val = conv(uv, c, FFN_DIM)
        ug, uv = ug_next, uv_next
        act = (gate * _sigmoid(gate) * val).astype(BF16)
        contrib = jnp.dot(act, wdn_ref[c * FFN_FC:(c + 1) * FFN_FC, :], preferred_element_type=F32)
        if c == 0:
            acc_s[...] = contrib
        else:
            acc_s[...] += contrib
    o_ref[...] = _rms(x1_ref[...] + acc_s[...], gf_ref[...])


def _ffn(h2, x1, wup, conv_w, conv_b, wdn, norm_final, seq):
    t = h2.shape[0]
    tm = FFN_TM
    halo_blocks = tm // FFN_HALO
    single = pl.Buffered(1)
    full = lambda shape, **kw: pl.BlockSpec(shape, lambda i: (0,) * len(shape), **kw)
    return pl.pallas_call(
        functools.partial(_ffn_kernel, tiles_per_seq=seq // tm),
        grid=(t // tm,),
        in_specs=[
            pl.BlockSpec((tm, D_MODEL), lambda i: (i, 0)),
            pl.BlockSpec((FFN_HALO, D_MODEL), lambda i: (jnp.maximum(i * halo_blocks - 1, 0), 0)),
            pl.BlockSpec((tm, D_MODEL), lambda i: (i, 0)),
            full((D_MODEL, 2 * FFN_DIM), pipeline_mode=single),
            full((FFN_CONV, 2 * FFN_DIM)),
            full((1, 2 * FFN_DIM)),
            full((FFN_DIM, D_MODEL), pipeline_mode=single),
            full((1, D_MODEL)),
        ],
        out_specs=pl.BlockSpec((tm, D_MODEL), lambda i: (i, 0)),
        out_shape=jax.ShapeDtypeStruct((t, D_MODEL), F32),
        scratch_shapes=[pltpu.VMEM((tm, D_MODEL), F32)],
        compiler_params=pltpu.CompilerParams(
            dimension_semantics=("arbitrary",), vmem_limit_bytes=VMEM_LIMIT),
        name="ffn",
    )(h2, h2, x1, wup, conv_w, conv_b, wdn, norm_final)


def kernel(x, rel_bias, norm_mix, w_in, kv_norm, w_uk, w_uv, conv_ssm_w, conv_ssm_b, dt_bias, a_log,
           d_skip, ssm_norm, w_att_out, w_ssm_out, w_out, norm_ffn, w_ffn_up, conv_ffn_w, conv_ffn_b,
           w_ffn_down, norm_final):
    batch, seq, _ = x.shape
    assert norm_mix.shape[0] == 1, "single layer"
    assert seq % (TILES_PER_CLASS * Q_BLOCK) == 0 and min(TOPK_MAX, seq // 4) == TOPK_MAX
    x2d = x.reshape(batch * seq, D_MODEL)

    tab = _bias_tiles(rel_bias)
    proj = _in_proj(x2d, norm_mix[0][None, :], _pack_w_in(w_in[0]))
    att = _dsa_attention(proj, jnp.swapaxes(w_uk[0], 1, 2).astype(BF16), w_uv[0].astype(BF16),
                         kv_norm[0][None, :], tab, batch, seq)
    yn = _ssd(proj, conv_ssm_w[0], conv_ssm_b[0], dt_bias[0], a_log[0], d_skip[0], ssm_norm[0],
              batch, seq)
    x1, h2 = _merge(att, yn, proj, x2d, w_att_out[0].astype(BF16), w_ssm_out[0].astype(BF16),
                    w_out[0].astype(BF16), norm_ffn[0][None, :])
    out = _ffn(h2, x1, w_ffn_up[0].astype(BF16), conv_ffn_w[0], conv_ffn_b[0][None, :],
               w_ffn_down[0].astype(BF16), norm_final[None, :], seq)
    return out.reshape(batch, seq, D_MODEL)
```

```python
import functools
import math

import numpy as np
import jax
import jax.numpy as jnp
from jax import lax
from jax.experimental import pallas as pl
from jax.experimental.pallas import tpu as pltpu

F32 = jnp.float32
BF16 = jnp.bfloat16
I32 = jnp.int32

D_MODEL = 1024
ATT_HEADS = 16
ATT_HEAD_DIM = 64
KV_LATENT = 128
IDX_HEADS = 8
IDX_DIM = 64
TOPK_MAX = 256
Q_BLOCK = 128
REL_BUCKETS = 32
REL_MAX_DIST = 128
SSM_INNER = 2 * D_MODEL
SSM_HEAD_DIM = 64
SSM_HEADS = SSM_INNER // SSM_HEAD_DIM
SSM_GROUPS = 4
SSM_HEADS_PER_GROUP = SSM_HEADS // SSM_GROUPS
SSM_STATE = 128
SSM_CONV = 4
SSM_CHUNK = 128
SSM_BC = SSM_GROUPS * SSM_STATE
FFN_DIM = 2816
FFN_CONV = 3
EPS = 1e-6

V7X_LANES = 128
V7X_MXU_WIDTH = 256
V7X_VMEM_BYTES = 64 * 1024 * 1024
VMEM_LIMIT = 56 * 1024 * 1024

INT_MIN = -(2 ** 31)
NEG_INF = float("-inf")
LOG2E = math.log2(math.e)

COL_Z = 0
COL_XS = COL_Z + SSM_INNER
COL_Q = COL_XS + SSM_INNER
COL_GA = COL_Q + D_MODEL
COL_GS = COL_GA + D_MODEL
COL_QI = COL_GS + D_MODEL
COL_BM = COL_QI + IDX_HEADS * IDX_DIM
COL_CM = COL_BM + SSM_BC
COL_CR = COL_CM + SSM_BC
COL_MISC = COL_CR + KV_LATENT
PROJ_COLS = COL_MISC + V7X_LANES
MISC_KIDX = 0
MISC_WIDX = IDX_DIM
MISC_DT = IDX_DIM + IDX_HEADS


def _pack_moves():
    sizes = (ATT_HEADS * ATT_HEAD_DIM, KV_LATENT, IDX_HEADS * IDX_DIM, IDX_DIM, IDX_HEADS,
             SSM_INNER, SSM_INNER, SSM_BC, SSM_BC, SSM_HEADS, D_MODEL, D_MODEL)
    src = np.cumsum((0,) + sizes)
    dst = (COL_Q, COL_CR, COL_QI, COL_MISC + MISC_KIDX, COL_MISC + MISC_WIDX, COL_Z, COL_XS, COL_BM,
           COL_CM, COL_MISC + MISC_DT, COL_GA, COL_GS)
    return [(int(s), int(d), int(n)) for s, d, n in zip(src[:-1], dst, sizes)]


PACK_ROWS = 128


def _pack_kernel(w_ref, o_ref):
    misc = []
    for s, d, n in _pack_moves():
        if d >= COL_MISC:
            assert d == COL_MISC + sum(p.shape[1] for p in misc)
            misc.append(w_ref[:, s:s + n])
        else:
            o_ref[:, d:d + n] = w_ref[:, s:s + n].astype(BF16)
    used = sum(p.shape[1] for p in misc)
    misc.append(jnp.zeros((PACK_ROWS, V7X_LANES - used), F32))
    o_ref[:, COL_MISC:PROJ_COLS] = jnp.concatenate(misc, axis=1).astype(BF16)


def _pack_w_in(w):
    k, cols = w.shape
    return pl.pallas_call(
        _pack_kernel,
        grid=(k // PACK_ROWS,),
        in_specs=[pl.BlockSpec((PACK_ROWS, cols), lambda i: (i, 0))],
        out_specs=pl.BlockSpec((PACK_ROWS, PROJ_COLS), lambda i: (i, 0)),
        out_shape=jax.ShapeDtypeStruct((k, PROJ_COLS), BF16),
        name="pack_w_in",
    )(w)


def _rms(x, gain):
    return x * lax.rsqrt(jnp.mean(x * x, axis=-1, keepdims=True) + EPS) * gain


def _sigmoid(x):
    return 1.0 / (1.0 + jnp.exp(-x))


def _bias_tiles_kernel(rb_ref, o_ref):
    h = pl.program_id(0)
    far = rb_ref[REL_BUCKETS - 1, h]
    key = lax.broadcasted_iota(I32, (Q_BLOCK, Q_BLOCK), 0)
    qry = lax.broadcasted_iota(I32, (Q_BLOCK, Q_BLOCK), 1)
    max_exact = REL_BUCKETS // 2
    for delta in range(2):
        n = jnp.maximum(delta * Q_BLOCK + qry - key, 0)
        nf = jnp.maximum(n, 1).astype(F32)
        large = max_exact + (jnp.log(nf / max_exact) / math.log(REL_MAX_DIST / max_exact)
                             * (REL_BUCKETS - max_exact)).astype(I32)
        large = jnp.minimum(large, REL_BUCKETS - 1)
        bucket = jnp.where(n < max_exact, n, large)
        val = jnp.zeros((Q_BLOCK, Q_BLOCK), F32)
        for b in range(REL_BUCKETS):
            val = jnp.where(bucket == b, rb_ref[b, h] - far, val)
        o_ref[delta] = val * LOG2E


def _bias_tiles(rel_bias):
    return pl.pallas_call(
        _bias_tiles_kernel,
        grid=(ATT_HEADS,),
        in_specs=[pl.BlockSpec(memory_space=pltpu.SMEM)],
        out_specs=pl.BlockSpec((2, Q_BLOCK, Q_BLOCK), lambda h: (0, 0, h)),
        out_shape=jax.ShapeDtypeStruct((2, Q_BLOCK, ATT_HEADS * Q_BLOCK), F32),
        name="bias_tiles",
    )(rel_bias)


INPROJ_TM = 256


def _inproj_kernel(x_ref, g_ref, w_ref, o_ref):
    h = _rms(x_ref[...], g_ref[...]).astype(BF16)
    o_ref[...] = jnp.dot(h, w_ref[...], preferred_element_type=F32)


def _in_proj(x2d, gain, w_packed):
    t = x2d.shape[0]
    return pl.pallas_call(
        _inproj_kernel,
        grid=(t // INPROJ_TM,),
        in_specs=[
            pl.BlockSpec((INPROJ_TM, D_MODEL), lambda i: (i, 0)),
            pl.BlockSpec((1, D_MODEL), lambda i: (0, 0)),
            pl.BlockSpec((D_MODEL, PROJ_COLS), lambda i: (0, 0), pipeline_mode=pl.Buffered(1)),
        ],
        out_specs=pl.BlockSpec((INPROJ_TM, PROJ_COLS), lambda i: (i, 0)),
        out_shape=jax.ShapeDtypeStruct((t, PROJ_COLS), F32),
        compiler_params=pltpu.CompilerParams(
            dimension_semantics=("arbitrary",), vmem_limit_bytes=VMEM_LIMIT),
        name="in_proj",
    )(x2d, gain, w_packed)


TILES_PER_CLASS = 2
PAD_TILES = TILES_PER_CLASS - 1
PAD_ROWS = PAD_TILES * Q_BLOCK
ROW_CHUNK = 256
COUNT_ROWS = 64
PAIR_LANES = 2 * Q_BLOCK
assert PAIR_LANES == V7X_MXU_WIDTH and ROW_CHUNK == TILES_PER_CLASS * Q_BLOCK
ONES_ROWS = 16
PV_ROWS = KV_LATENT + ONES_ROWS
NEG_BIG = -1e30


def _attn_block(n_tiles, first_tile, i, qiT, wT, tab_ref, wuv_ref, cm_s, ckvT_s, kidx_s, qT_s, key_s,
                hi_s, lo_s, j_s, l0_s, l1_s, p0_s, p1_s, ot_s, attT_s):
    nkp = n_tiles * Q_BLOCK
    n_sel = float(TOPK_MAX)
    start = pl.multiple_of(first_tile * Q_BLOCK, Q_BLOCK)
    key0 = (first_tile - PAD_TILES) * Q_BLOCK
    q_pos = i * Q_BLOCK + lax.broadcasted_iota(I32, (ROW_CHUNK, Q_BLOCK), 1)
    row_in_chunk = lax.broadcasted_iota(I32, (ROW_CHUNK, Q_BLOCK), 0)
    n_chunks = nkp // ROW_CHUNK

    for rc in range(n_chunks):
        rows = slice(rc * ROW_CHUNK, (rc + 1) * ROW_CHUNK)
        kc = kidx_s[pl.ds(pl.multiple_of(start + rc * ROW_CHUNK, Q_BLOCK), ROW_CHUNK), :]
        acc = jnp.zeros((ROW_CHUNK, Q_BLOCK), F32)
        for hp in range(IDX_HEADS // 2):
            rhs = jnp.concatenate(
                [qiT[(2 * hp) * IDX_DIM:(2 * hp + 1) * IDX_DIM, :],
                 qiT[(2 * hp + 1) * IDX_DIM:(2 * hp + 2) * IDX_DIM, :]], axis=1)
            lg = jnp.dot(kc, rhs, preferred_element_type=F32)
            w0 = wT[MISC_WIDX + 2 * hp:MISC_WIDX + 2 * hp + 1, :]
            w1 = wT[MISC_WIDX + 2 * hp + 1:MISC_WIDX + 2 * hp + 2, :]
            acc = acc + jnp.maximum(lg[:, :Q_BLOCK], 0.0) * w0 + jnp.maximum(lg[:, Q_BLOCK:], 0.0) * w1
        bits = pltpu.bitcast(acc, I32)
        bits = jnp.where(bits == INT_MIN, 0, bits)
        key = jnp.where(bits < 0, bits ^ 0x7FFFFFFF, bits)
        key_pos = key0 + rc * ROW_CHUNK + row_in_chunk
        if rc * ROW_CHUNK < PAD_ROWS:
            key = jnp.where(key_pos >= 0, key, INT_MIN)
        if rc == n_chunks - 1:
            key = jnp.where(key_pos <= q_pos, key, INT_MIN)
        key_s[rows, :] = key

    def count_ones(ones):
        part = jnp.sum(ones.reshape(nkp // COUNT_ROWS, COUNT_ROWS, Q_BLOCK), axis=0)
        return jnp.sum(part, axis=0, keepdims=True)

    def count(pred):
        return count_ones(jnp.where(pred, 1.0, 0.0))

    def count16(pred):
        ones = jnp.where(pred, jnp.int16(1), jnp.int16(0))
        part = ones[0:COUNT_ROWS, :]
        for k in range(1, nkp // COUNT_ROWS):
            part = part + ones[k * COUNT_ROWS:(k + 1) * COUNT_ROWS, :]
        return jnp.sum(part.astype(F32), axis=0, keepdims=True)

    def half_search(half_ref, target):
        def step(bit, thr):
            cand = thr ^ jnp.left_shift(jnp.int32(1), 15 - bit)
            cand = jnp.where(bit == 0, cand + 2 ** 16, cand)
            hit = count16(half_ref[0:nkp, :] >= cand.astype(jnp.int16)) >= target
            return jnp.where(hit, cand, thr)
        return lax.fori_loop(0, 16, step, jnp.full((1, Q_BLOCK), -(2 ** 15), I32))

    keys = key_s[0:nkp, :]
    hi_s[0:nkp, :] = jnp.right_shift(keys, 16).astype(jnp.int16)
    thr_hi = half_search(hi_s, n_sel)
    hi = hi_s[0:nkp, :]
    thr_hi16 = thr_hi.astype(jnp.int16)
    above = count16(hi > thr_hi16)
    lo = ((keys & 0xFFFF) - 2 ** 15).astype(jnp.int16)
    lo_s[0:nkp, :] = jnp.where(hi == thr_hi16, lo, jnp.int16(-(2 ** 15)))
    thr_lo = half_search(lo_s, n_sel - above)
    thr = jnp.left_shift(thr_hi, 16) | (thr_lo + 2 ** 15)
    need = n_sel - count(keys > thr)
    split = count(keys == thr) > need

    j_s[...] = jnp.full((1, Q_BLOCK), 2 * nkp, I32)

    @pl.when(jnp.max(jnp.where(split, 1.0, 0.0)) > 0.5)
    def _():
        row = lax.broadcasted_iota(I32, (nkp, Q_BLOCK), 0)
        nbits = (2 * nkp - 1).bit_length()

        def index_step(bit, jmax):
            cand = jmax | jnp.left_shift(jnp.int32(1), nbits - 1 - bit)
            f = count_ones(jnp.where(key_s[0:nkp, :] == thr, jnp.where(row < cand, 1.0, 0.0), 0.0))
            return jnp.where(f <= need, cand, jmax)

        j_s[...] = lax.fori_loop(0, nbits, index_step, jnp.zeros((1, Q_BLOCK), I32))

    jmax = j_s[...]
    row_in_tile = lax.broadcasted_iota(I32, (Q_BLOCK, Q_BLOCK), 0)
    for t in range(n_tiles):
        k = key_s[t * Q_BLOCK:(t + 1) * Q_BLOCK, :]
        row = t * Q_BLOCK + row_in_tile
        add = jnp.where(k > thr, 0.0,
                        jnp.where(k == thr, jnp.where(row < jmax, 0.0, NEG_BIG), NEG_BIG))
        add = jnp.where(k == INT_MIN, NEG_BIG, add)
        cm_s[pl.ds(pl.multiple_of(start + t * Q_BLOCK, Q_BLOCK), Q_BLOCK),
             KV_LATENT:2 * KV_LATENT] = add.astype(BF16)

    tiles_per_chunk = ROW_CHUNK // Q_BLOCK

    def unit(logits=None, exps=None, values=None):
        if logits is not None:
            hp_a, lbuf_a = logits
            off = pl.multiple_of(hp_a * PAIR_LANES, PAIR_LANES)
            rhs = qT_s[:, pl.ds(off, PAIR_LANES)]
            m8 = jnp.full((8, PAIR_LANES), NEG_INF, F32)
        if exps is not None:
            lbuf_b, pbuf_b, m_b = exps
        if values is not None:
            hp_c, pbuf_c = values
            acc = None
        for rc in range(n_chunks):
            rows = slice(rc * ROW_CHUNK, (rc + 1) * ROW_CHUNK)
            win = pl.ds(pl.multiple_of(start + rc * ROW_CHUNK, Q_BLOCK), ROW_CHUNK)
            if logits is not None:
                lc = jnp.dot(cm_s[win, :], rhs, preferred_element_type=F32)
                for tt in range(tiles_per_chunk):
                    t = rc * tiles_per_chunk + tt
                    lt = lc[tt * Q_BLOCK:(tt + 1) * Q_BLOCK, :]
                    if t >= n_tiles - 2:
                        lt = lt + tab_ref[n_tiles - 1 - t, :, pl.ds(off, PAIR_LANES)]
                    lbuf_a[t * Q_BLOCK:(t + 1) * Q_BLOCK, :] = lt
                    m8 = jnp.maximum(m8, jnp.max(lt.reshape(Q_BLOCK // 8, 8, PAIR_LANES), axis=0))
            if exps is not None:
                pbuf_b[rows, :] = jnp.exp2(lbuf_b[rows, :] - m_b).astype(BF16)
            if values is not None:
                part = jnp.dot(ckvT_s[:, win], pbuf_c[rows, :], preferred_element_type=F32)
                acc = part if acc is None else acc + part
        if values is not None:
            off_c = pl.multiple_of(hp_c * PAIR_LANES, PAIR_LANES)
            ot_s[:, pl.ds(off_c, PAIR_LANES)] = (
                acc[0:KV_LATENT, :] / acc[KV_LATENT:KV_LATENT + 1, :]).astype(BF16)
        if logits is not None:
            return jnp.max(m8, axis=0, keepdims=True)

    n_pairs = ATT_HEADS // 2

    def two_units(jj, m_odd):
        hp = 2 * jj
        m_even = unit(logits=(hp + 2, l0_s), exps=(l1_s, p1_s, m_odd), values=(hp, p0_s))
        return unit(logits=(hp + 3, l1_s), exps=(l0_s, p0_s, m_even), values=(hp + 1, p1_s))

    m_even = unit(logits=(0, l0_s))
    m_odd = unit(logits=(1, l1_s), exps=(l0_s, p0_s, m_even))
    m_odd = lax.fori_loop(0, n_pairs // 2 - 1, two_units, m_odd)
    unit(exps=(l1_s, p1_s, m_odd), values=(n_pairs - 2, p0_s))
    unit(values=(n_pairs - 1, p1_s))

    for h in range(ATT_HEADS):
        attT_s[h * ATT_HEAD_DIM:(h + 1) * ATT_HEAD_DIM, :] = jnp.dot(
            wuv_ref[h], ot_s[:, h * Q_BLOCK:(h + 1) * Q_BLOCK], preferred_element_type=F32)


def _attn_kernel(q_ref, qi_ref, mq_ref, cr_ref, mk_ref, wukT_ref, wuv_ref, kvn_ref, tab_ref, o_ref,
                 cm_s, ckvT_s, kidx_s, qT_s, key_s, hi_s, lo_s, j_s, l0_s, l1_s, p0_s, p1_s, ot_s, attT_s):
    i = pl.program_id(1)
    seq = cr_ref.shape[0]

    @pl.when(i == 0)
    def _():
        c = _rms(cr_ref[...], kvn_ref[...])
        cm_s[0:PAD_ROWS, :] = jnp.zeros((PAD_ROWS, 2 * KV_LATENT), BF16)
        cm_s[PAD_ROWS:, 0:KV_LATENT] = c.astype(BF16)
        ckvT_s[0:KV_LATENT, 0:PAD_ROWS] = jnp.zeros((KV_LATENT, PAD_ROWS), BF16)
        ckvT_s[0:KV_LATENT, PAD_ROWS:] = c.T.astype(BF16)
        ckvT_s[KV_LATENT:, :] = jnp.ones((ONES_ROWS, seq + PAD_ROWS), BF16)
        kidx_s[0:PAD_ROWS, :] = jnp.zeros((PAD_ROWS, IDX_DIM), BF16)
        kidx_s[PAD_ROWS:, :] = mk_ref[:, MISC_KIDX:MISC_KIDX + IDX_DIM].astype(BF16)
        eye = jnp.where(lax.broadcasted_iota(I32, (Q_BLOCK, Q_BLOCK), 0)
                        == lax.broadcasted_iota(I32, (Q_BLOCK, Q_BLOCK), 1), 1.0, 0.0).astype(BF16)
        for h in range(ATT_HEADS):
            qT_s[KV_LATENT:, h * Q_BLOCK:(h + 1) * Q_BLOCK] = eye

    scale = ATT_HEAD_DIM ** -0.5 * LOG2E
    qb = q_ref[...].astype(BF16)
    for h in range(ATT_HEADS):
        qh = qb[:, h * ATT_HEAD_DIM:(h + 1) * ATT_HEAD_DIM]
        qt = lax.dot_general(wukT_ref[h], qh, (((1,), (1,)), ((), ())), preferred_element_type=F32)
        qT_s[0:KV_LATENT, h * Q_BLOCK:(h + 1) * Q_BLOCK] = (qt * scale).astype(BF16)

    qiT = qi_ref[...].T.astype(BF16)
    wT = mq_ref[...].T

    cls = lax.shift_right_logical(i, int(math.log2(TILES_PER_CLASS)))
    for k in range(seq // Q_BLOCK // TILES_PER_CLASS):
        @pl.when(cls == k)
        def _(k=k):
            _attn_block(TILES_PER_CLASS * (k + 1), i - TILES_PER_CLASS * k, i, qiT, wT, tab_ref,
                        wuv_ref, cm_s, ckvT_s, kidx_s, qT_s, key_s, hi_s, lo_s, j_s, l0_s, l1_s, p0_s, p1_s, ot_s, attT_s)

    o_ref[...] = attT_s[...].T.astype(BF16)


def _dsa_attention(proj, wukT, wuv, kv_norm, tab, batch, seq):
    nqb = seq // Q_BLOCK
    att_dim = ATT_HEADS * ATT_HEAD_DIM
    return pl.pallas_call(
        _attn_kernel,
        grid=(batch, nqb),
        in_specs=[
            pl.BlockSpec((Q_BLOCK, att_dim), lambda b, i: (b * nqb + i, COL_Q // att_dim)),
            pl.BlockSpec((Q_BLOCK, IDX_HEADS * IDX_DIM),
                         lambda b, i: (b * nqb + i, COL_QI // (IDX_HEADS * IDX_DIM))),
            pl.BlockSpec((Q_BLOCK, V7X_LANES), lambda b, i: (b * nqb + i, COL_MISC // V7X_LANES)),
            pl.BlockSpec((seq, KV_LATENT), lambda b, i: (b, COL_CR // KV_LATENT)),
            pl.BlockSpec((seq, V7X_LANES), lambda b, i: (b, COL_MISC // V7X_LANES)),
            pl.BlockSpec((ATT_HEADS, KV_LATENT, ATT_HEAD_DIM), lambda b, i: (0, 0, 0)),
            pl.BlockSpec((ATT_HEADS, ATT_HEAD_DIM, KV_LATENT), lambda b, i: (0, 0, 0)),
            pl.BlockSpec((1, KV_LATENT), lambda b, i: (0, 0)),
            pl.BlockSpec((2, Q_BLOCK, ATT_HEADS * Q_BLOCK), lambda b, i: (0, 0, 0)),
        ],
        out_specs=pl.BlockSpec((Q_BLOCK, att_dim), lambda b, i: (b * nqb + i, 0)),
        out_shape=jax.ShapeDtypeStruct((batch * seq, att_dim), BF16),
        scratch_shapes=[
            pltpu.VMEM((seq + PAD_ROWS, 2 * KV_LATENT), BF16),
            pltpu.VMEM((PV_ROWS, seq + PAD_ROWS), BF16),
            pltpu.VMEM((seq + PAD_ROWS, IDX_DIM), BF16),
            pltpu.VMEM((2 * KV_LATENT, ATT_HEADS * Q_BLOCK), BF16),
            pltpu.VMEM((seq, Q_BLOCK), I32),
            pltpu.VMEM((seq, Q_BLOCK), jnp.int16),
            pltpu.VMEM((seq, Q_BLOCK), jnp.int16),
            pltpu.VMEM((1, Q_BLOCK), I32),
            pltpu.VMEM((seq, PAIR_LANES), F32),
            pltpu.VMEM((seq, PAIR_LANES), F32),
            pltpu.VMEM((seq, PAIR_LANES), BF16),
            pltpu.VMEM((seq, PAIR_LANES), BF16),
            pltpu.VMEM((KV_LATENT, ATT_HEADS * Q_BLOCK), BF16),
            pltpu.VMEM((att_dim, Q_BLOCK), F32),
        ],
        compiler_params=pltpu.CompilerParams(
            dimension_semantics=("arbitrary", "arbitrary"), vmem_limit_bytes=VMEM_LIMIT),
        name="dsa_attn",
    )(proj, proj, proj, proj, proj, wukT, wuv, kv_norm, tab)


TAIL = 8
GROUP_LANES = SSM_HEADS_PER_GROUP * SSM_HEAD_DIM


def _expand_matrices():
    e64 = np.zeros((V7X_LANES, SSM_INNER), np.float32)
    e128 = np.zeros((V7X_LANES, SSM_HEADS * SSM_CHUNK), np.float32)
    for h in range(SSM_HEADS):
        for piece in range(3):
            lane = (MISC_DT + h + piece * SSM_HEADS) % V7X_LANES
            e64[lane, h * SSM_HEAD_DIM:(h + 1) * SSM_HEAD_DIM] = 1.0
            e128[lane, h * SSM_CHUNK:(h + 1) * SSM_CHUNK] = 1.0
    return jnp.asarray(e64, BF16), jnp.asarray(e128, BF16)


def _split3(v):
    hi = v.astype(BF16).astype(F32)
    r1 = v - hi
    mid = r1.astype(BF16).astype(F32)
    lo = r1 - mid
    packed = hi + pltpu.roll(mid, SSM_HEADS, axis=1) + pltpu.roll(lo, 2 * SSM_HEADS, axis=1)
    return packed.astype(BF16)


def _conv_silu(u_ref, tail_s, w_ref, b_ref):
    n = SSM_CHUNK
    u = u_ref[...]
    ext = jnp.concatenate([tail_s[...], u], axis=0)
    out = b_ref[...] + w_ref[SSM_CONV - 1:SSM_CONV, :] * u
    for k in range(1, SSM_CONV):
        out = out + w_ref[SSM_CONV - 1 - k:SSM_CONV - k, :] * pltpu.roll(ext, k, axis=0)[TAIL:, :]
    tail_s[...] = u[n - TAIL:n, :]
    return out * _sigmoid(out)


def _ssd_kernel(z_ref, xs_ref, bm_ref, cm_ref, misc_ref, cwx_ref, cwb_ref, cwc_ref, cbx_ref, cbb_ref,
                cbc_ref, dtb_ref, alog_ref, dsk_ref, gn_ref, e64_ref, e128_ref, o_ref,
                xtail_s, btail_s, ctail_s, state_s):
    n = SSM_CHUNK

    @pl.when(pl.program_id(1) == 0)
    def _():
        state_s[...] = jnp.zeros_like(state_s)
        xtail_s[...] = jnp.zeros_like(xtail_s)
        btail_s[...] = jnp.zeros_like(btail_s)
        ctail_s[...] = jnp.zeros_like(ctail_s)

    xs = _conv_silu(xs_ref, xtail_s, cwx_ref, cbx_ref)
    bm = _conv_silu(bm_ref, btail_s, cwb_ref, cbb_ref)
    cm = _conv_silu(cm_ref, ctail_s, cwc_ref, cbc_ref)

    lane = lax.broadcasted_iota(I32, (n, V7X_LANES), 1)
    row = lax.broadcasted_iota(I32, (n, V7X_LANES), 0)
    on_dt = (lane >= MISC_DT) & (lane < MISC_DT + SSM_HEADS)

    v = misc_ref[...] + dtb_ref[...]
    dt = jnp.maximum(v, 0.0) + jnp.log1p(jnp.exp(-jnp.abs(v)))
    dt = jnp.where(on_dt, dt, 0.0)
    a_neg = -jnp.exp(alog_ref[...])
    acum = dt * a_neg
    k = 1
    while k < n:
        acum = acum + jnp.where(row >= k, pltpu.roll(acum, k, axis=0), 0.0)
        k *= 2

    e64 = e64_ref[...]
    dt_e = jnp.dot(_split3(dt), e64, preferred_element_type=F32)
    acum_p = _split3(acum)
    acum_e = jnp.dot(acum_p, e64, preferred_element_type=F32)
    acum_cb = jnp.dot(acum_p, e128_ref[...], preferred_element_type=F32)
    acum_t = acum.T

    last = acum_e[n - 1:n, :]
    xdt = xs * dt_e
    xds_b = (xdt * jnp.exp(last - acum_e)).astype(BF16)
    chunk_decay = jnp.exp(last)
    ea_e = jnp.exp(acum_e)

    tri = row >= lane
    first_head = lane < SSM_HEAD_DIM
    y_groups = []
    for g in range(SSM_GROUPS):
        gl = slice(g * GROUP_LANES, (g + 1) * GROUP_LANES)
        bmg = bm[:, g * SSM_STATE:(g + 1) * SSM_STATE]
        cmg_b = cm[:, g * SSM_STATE:(g + 1) * SSM_STATE].astype(BF16)
        cb = lax.dot_general(cmg_b, bmg.astype(BF16), (((1,), (1,)), ((), ())),
                             preferred_element_type=F32)
        y_pairs = []
        for pr in range(SSM_HEADS_PER_GROUP // 2):
            h0 = g * SSM_HEADS_PER_GROUP + 2 * pr
            mats = []
            for h in (h0, h0 + 1):
                seg = acum_cb[:, h * n:(h + 1) * n] - acum_t[MISC_DT + h:MISC_DT + h + 1, :]
                mats.append((cb * jnp.exp(jnp.where(tri, seg, NEG_INF))).astype(BF16))
            xp = xdt[:, h0 * SSM_HEAD_DIM:(h0 + 2) * SSM_HEAD_DIM]
            rhs = jnp.concatenate([jnp.where(first_head, xp, 0.0), jnp.where(first_head, 0.0, xp)],
                                  axis=0).astype(BF16)
            y_pairs.append(jnp.dot(jnp.concatenate(mats, axis=1), rhs, preferred_element_type=F32))
        prev = state_s[g]
        y_off = jnp.dot(cmg_b, prev.astype(BF16), preferred_element_type=F32) * ea_e[:, gl]
        state_s[g] = prev * chunk_decay[:, gl] + jnp.dot(bmg.T.astype(BF16), xds_b[:, gl],
                                                         preferred_element_type=F32)
        y_groups.append(jnp.concatenate(y_pairs, axis=1) + y_off)

    z = z_ref[...]
    gz = z * _sigmoid(z)
    outs = []
    for g in range(SSM_GROUPS):
        gl = slice(g * GROUP_LANES, (g + 1) * GROUP_LANES)
        u = (y_groups[g] + dsk_ref[:, gl] * xs[:, gl]) * gz[:, gl]
        outs.append(u * lax.rsqrt(jnp.mean(u * u, axis=-1, keepdims=True) + EPS) * gn_ref[:, gl])
    o_ref[...] = jnp.concatenate(outs, axis=1).astype(BF16)


def _ssd(proj, conv_w, conv_b, dt_bias, a_log, d_skip, ssm_norm, batch, seq):
    nch = seq // SSM_CHUNK
    n = SSM_CHUNK
    cwx, cwb, cwc = (conv_w[:, :SSM_INNER], conv_w[:, SSM_INNER:SSM_INNER + SSM_BC],
                     conv_w[:, SSM_INNER + SSM_BC:])
    cbx, cbb, cbc = (conv_b[None, :SSM_INNER], conv_b[None, SSM_INNER:SSM_INNER + SSM_BC],
                     conv_b[None, SSM_INNER + SSM_BC:])
    pad_l, pad_r = MISC_DT, V7X_LANES - MISC_DT - SSM_HEADS
    dtb = jnp.pad(dt_bias, (pad_l, pad_r))[None, :]
    alog = jnp.pad(a_log, (pad_l, pad_r))[None, :]
    dsk = jnp.repeat(d_skip, SSM_HEAD_DIM)[None, :]
    gn = ssm_norm[None, :]
    e64, e128 = _expand_matrices()
    full = lambda shape: pl.BlockSpec(shape, lambda b, c: (0,) * len(shape))
    rows = lambda b, c: b * nch + c
    return pl.pallas_call(
        _ssd_kernel,
        grid=(batch, nch),
        in_specs=[
            pl.BlockSpec((n, SSM_INNER), lambda b, c: (rows(b, c), COL_Z // SSM_INNER)),
            pl.BlockSpec((n, SSM_INNER), lambda b, c: (rows(b, c), COL_XS // SSM_INNER)),
            pl.BlockSpec((n, SSM_BC), lambda b, c: (rows(b, c), COL_BM // SSM_BC)),
            pl.BlockSpec((n, SSM_BC), lambda b, c: (rows(b, c), COL_CM // SSM_BC)),
            pl.BlockSpec((n, V7X_LANES), lambda b, c: (rows(b, c), COL_MISC // V7X_LANES)),
            full((SSM_CONV, SSM_INNER)), full((SSM_CONV, SSM_BC)), full((SSM_CONV, SSM_BC)),
            full((1, SSM_INNER)), full((1, SSM_BC)), full((1, SSM_BC)),
            full((1, V7X_LANES)), full((1, V7X_LANES)),
            full((1, SSM_INNER)), full((1, SSM_INNER)),
            full((V7X_LANES, SSM_INNER)), full((V7X_LANES, SSM_HEADS * n)),
        ],
        out_specs=pl.BlockSpec((n, SSM_INNER), lambda b, c: (rows(b, c), 0)),
        out_shape=jax.ShapeDtypeStruct((batch * seq, SSM_INNER), BF16),
        scratch_shapes=[
            pltpu.VMEM((TAIL, SSM_INNER), F32),
            pltpu.VMEM((TAIL, SSM_BC), F32),
            pltpu.VMEM((TAIL, SSM_BC), F32),
            pltpu.VMEM((SSM_GROUPS, SSM_STATE, GROUP_LANES), F32),
        ],
        compiler_params=pltpu.CompilerParams(
            dimension_semantics=("arbitrary", "arbitrary"), vmem_limit_bytes=VMEM_LIMIT),
        name="ssd",
    )(proj, proj, proj, proj, proj, cwx, cwb, cwc, cbx, cbb, cbc, dtb, alog, dsk, gn, e64, e128)


MERGE_TM = 512


def _merge_kernel(att_ref, yn_ref, ga_ref, gs_ref, x_ref, wao_ref, wso_ref, wo_ref, gf_ref,
                  x1_ref, h2_ref):
    y_att = jnp.dot(att_ref[...], wao_ref[...], preferred_element_type=F32)
    y_ssm = jnp.dot(yn_ref[...], wso_ref[...], preferred_element_type=F32)
    merged = _sigmoid(ga_ref[...]) * y_att + _sigmoid(gs_ref[...]) * y_ssm
    x1 = x_ref[...] + jnp.dot(merged.astype(BF16), wo_ref[...], preferred_element_type=F32)
    x1_ref[...] = x1
    h2_ref[...] = _rms(x1, gf_ref[...]).astype(BF16)


def _merge(att, yn, proj, x2d, wao, wso, wo, norm_ffn):
    t = x2d.shape[0]
    tm = MERGE_TM
    full = lambda shape: pl.BlockSpec(shape, lambda i: (0,) * len(shape))
    return pl.pallas_call(
        _merge_kernel,
        grid=(t // tm,),
        in_specs=[
            pl.BlockSpec((tm, D_MODEL), lambda i: (i, 0)),
            pl.BlockSpec((tm, SSM_INNER), lambda i: (i, 0)),
            pl.BlockSpec((tm, D_MODEL), lambda i: (i, COL_GA // D_MODEL)),
            pl.BlockSpec((tm, D_MODEL), lambda i: (i, COL_GS // D_MODEL)),
            pl.BlockSpec((tm, D_MODEL), lambda i: (i, 0)),
            full((D_MODEL, D_MODEL)), full((SSM_INNER, D_MODEL)), full((D_MODEL, D_MODEL)),
            full((1, D_MODEL)),
        ],
        out_specs=[pl.BlockSpec((tm, D_MODEL), lambda i: (i, 0)),
                   pl.BlockSpec((tm, D_MODEL), lambda i: (i, 0))],
        out_shape=[jax.ShapeDtypeStruct((t, D_MODEL), F32),
                   jax.ShapeDtypeStruct((t, D_MODEL), BF16)],
        compiler_params=pltpu.CompilerParams(
            dimension_semantics=("arbitrary",), vmem_limit_bytes=VMEM_LIMIT),
        name="merge",
    )(att, yn, proj, proj, x2d, wao, wso, wo, norm_ffn)


FFN_TM = 512
FFN_HALO = 16
FFN_FC = V7X_MXU_WIDTH


def _ffn_kernel(h_ref, halo_ref, x1_ref, wup_ref, cw_ref, cb_ref, wdn_ref, gf_ref, o_ref,
                acc_s, *, tiles_per_seq):
    tm = FFN_TM
    keep = jnp.where(pl.program_id(0) % tiles_per_seq == 0, 0.0, 1.0)
    hcat = jnp.concatenate([halo_ref[...], h_ref[...]], axis=0)
    n_chunks = FFN_DIM // FFN_FC

    def up(c, base):
        return jnp.dot(hcat, wup_ref[:, base + c * FFN_FC:base + (c + 1) * FFN_FC],
                       preferred_element_type=F32)

    def conv(u, c, base):
        cols = slice(base + c * FFN_FC, base + (c + 1) * FFN_FC)
        u = jnp.concatenate([u[0:FFN_HALO, :] * keep, u[FFN_HALO:, :]], axis=0)
        out = cb_ref[:, cols] + cw_ref[FFN_CONV - 1:FFN_CONV, cols] * u[FFN_HALO:, :]
        for k in range(1, FFN_CONV):
            out = out + (cw_ref[FFN_CONV - 1 - k:FFN_CONV - k, cols]
                         * pltpu.roll(u, k, axis=0)[FFN_HALO:, :])
        return out

    ug, uv = up(0, 0), up(0, FFN_DIM)
    for c in range(n_chunks):
        last = c + 1 == n_chunks
        ug_next = None if last else up(c + 1, 0)
        gate = conv(ug, c, 0)
        uv_next = None if last else up(c + 1, FFN_DIM)
        val = conv(uv, c, FFN_DIM)
        ug, uv = ug_next, uv_next
        act = (gate * _sigmoid(gate) * val).astype(BF16)
        contrib = jnp.dot(act, wdn_ref[c * FFN_FC:(c + 1) * FFN_FC, :], preferred_element_type=F32)
        if c == 0:
            acc_s[...] = contrib
        else:
            acc_s[...] += contrib
    o_ref[...] = _rms(x1_ref[...] + acc_s[...], gf_ref[...])


def _ffn(h2, x1, wup, conv_w, conv_b, wdn, norm_final, seq):
    t = h2.shape[0]
    tm = FFN_TM
    halo_blocks = tm // FFN_HALO
    single = pl.Buffered(1)
    full = lambda shape, **kw: pl.BlockSpec(shape, lambda i: (0,) * len(shape), **kw)
    return pl.pallas_call(
        functools.partial(_ffn_kernel, tiles_per_seq=seq // tm),
        grid=(t // tm,),
        in_specs=[
            pl.BlockSpec((tm, D_MODEL), lambda i: (i, 0)),
            pl.BlockSpec((FFN_HALO, D_MODEL), lambda i: (jnp.maximum(i * halo_blocks - 1, 0), 0)),
            pl.BlockSpec((tm, D_MODEL), lambda i: (i, 0)),
            full((D_MODEL, 2 * FFN_DIM), pipeline_mode=single),
            full((FFN_CONV, 2 * FFN_DIM)),
            full((1, 2 * FFN_DIM)),
            full((FFN_DIM, D_MODEL), pipeline_mode=single),
            full((1, D_MODEL)),
        ],
        out_specs=pl.BlockSpec((tm, D_MODEL), lambda i: (i, 0)),
        out_shape=jax.ShapeDtypeStruct((t, D_MODEL), F32),
        scratch_shapes=[pltpu.VMEM((tm, D_MODEL), F32)],
        compiler_params=pltpu.CompilerParams(
            dimension_semantics=("arbitrary",), vmem_limit_bytes=VMEM_LIMIT),
        name="ffn",
    )(h2, h2, x1, wup, conv_w, conv_b, wdn, norm_final)


def kernel(x, rel_bias, norm_mix, w_in, kv_norm, w_uk, w_uv, conv_ssm_w, conv_ssm_b, dt_bias, a_log,
           d_skip, ssm_norm, w_att_out, w_ssm_out, w_out, norm_ffn, w_ffn_up, conv_ffn_w, conv_ffn_b,
           w_ffn_down, norm_final):
    batch, seq, _ = x.shape
    assert norm_mix.shape[0] == 1, "single layer"
    assert seq % (TILES_PER_CLASS * Q_BLOCK) == 0 and min(TOPK_MAX, seq // 4) == TOPK_MAX
    x2d = x.reshape(batch * seq, D_MODEL)

    tab = _bias_tiles(rel_bias)
    proj = _in_proj(x2d, norm_mix[0][None, :], _pack_w_in(w_in[0]))
    att = _dsa_attention(proj, jnp.swapaxes(w_uk[0], 1, 2).astype(BF16), w_uv[0].astype(BF16),
                         kv_norm[0][None, :], tab, batch, seq)
    yn = _ssd(proj, conv_ssm_w[0], conv_ssm_b[0], dt_bias[0], a_log[0], d_skip[0], ssm_norm[0],
              batch, seq)
    x1, h2 = _merge(att, yn, proj, x2d, w_att_out[0].astype(BF16), w_ssm_out[0].astype(BF16),
                    w_out[0].astype(BF16), norm_ffn[0][None, :])
    out = _ffn(h2, x1, w_ffn_up[0].astype(BF16), conv_ffn_w[0], conv_ffn_b[0][None, :],
               w_ffn_down[0].astype(BF16), norm_final[None, :], seq)
    return out.reshape(batch, seq, D_MODEL)
```

```python
import functools
import math

import numpy as np
import jax
import jax.numpy as jnp
from jax import lax
from jax.experimental import pallas as pl
from jax.experimental.pallas import tpu as pltpu

F32 = jnp.float32
BF16 = jnp.bfloat16
I32 = jnp.int32

D_MODEL = 1024
ATT_HEADS = 16
ATT_HEAD_DIM = 64
KV_LATENT = 128
IDX_HEADS = 8
IDX_DIM = 64
TOPK_MAX = 256
Q_BLOCK = 128
REL_BUCKETS = 32
REL_MAX_DIST = 128
SSM_INNER = 2 * D_MODEL
SSM_HEAD_DIM = 64
SSM_HEADS = SSM_INNER // SSM_HEAD_DIM
SSM_GROUPS = 4
SSM_HEADS_PER_GROUP = SSM_HEADS // SSM_GROUPS
SSM_STATE = 128
SSM_CONV = 4
SSM_CHUNK = 128
SSM_BC = SSM_GROUPS * SSM_STATE
FFN_DIM = 2816
FFN_CONV = 3
EPS = 1e-6

V7X_LANES = 128
V7X_MXU_WIDTH = 256
V7X_VMEM_BYTES = 64 * 1024 * 1024
VMEM_LIMIT = 56 * 1024 * 1024

INT_MIN = -(2 ** 31)
NEG_INF = float("-inf")
LOG2E = math.log2(math.e)

COL_Z = 0
COL_XS = COL_Z + SSM_INNER
COL_Q = COL_XS + SSM_INNER
COL_GA = COL_Q + D_MODEL
COL_GS = COL_GA + D_MODEL
COL_QI = COL_GS + D_MODEL
COL_BM = COL_QI + IDX_HEADS * IDX_DIM
COL_CM = COL_BM + SSM_BC
COL_CR = COL_CM + SSM_BC
COL_MISC = COL_CR + KV_LATENT
PROJ_COLS = COL_MISC + V7X_LANES
MISC_KIDX = 0
MISC_WIDX = IDX_DIM
MISC_DT = IDX_DIM + IDX_HEADS


def _pack_moves():
    sizes = (ATT_HEADS * ATT_HEAD_DIM, KV_LATENT, IDX_HEADS * IDX_DIM, IDX_DIM, IDX_HEADS,
             SSM_INNER, SSM_INNER, SSM_BC, SSM_BC, SSM_HEADS, D_MODEL, D_MODEL)
    src = np.cumsum((0,) + sizes)
    dst = (COL_Q, COL_CR, COL_QI, COL_MISC + MISC_KIDX, COL_MISC + MISC_WIDX, COL_Z, COL_XS, COL_BM,
           COL_CM, COL_MISC + MISC_DT, COL_GA, COL_GS)
    return [(int(s), int(d), int(n)) for s, d, n in zip(src[:-1], dst, sizes)]


PACK_ROWS = 128


def _pack_kernel(w_ref, o_ref):
    misc = []
    for s, d, n in _pack_moves():
        if d >= COL_MISC:
            assert d == COL_MISC + sum(p.shape[1] for p in misc)
            misc.append(w_ref[:, s:s + n])
        else:
            o_ref[:, d:d + n] = w_ref[:, s:s + n].astype(BF16)
    used = sum(p.shape[1] for p in misc)
    misc.append(jnp.zeros((PACK_ROWS, V7X_LANES - used), F32))
    o_ref[:, COL_MISC:PROJ_COLS] = jnp.concatenate(misc, axis=1).astype(BF16)


def _pack_w_in(w):
    k, cols = w.shape
    return pl.pallas_call(
        _pack_kernel,
        grid=(k // PACK_ROWS,),
        in_specs=[pl.BlockSpec((PACK_ROWS, cols), lambda i: (i, 0))],
        out_specs=pl.BlockSpec((PACK_ROWS, PROJ_COLS), lambda i: (i, 0)),
        out_shape=jax.ShapeDtypeStruct((k, PROJ_COLS), BF16),
        name="pack_w_in",
    )(w)


def _rms(x, gain):
    return x * lax.rsqrt(jnp.mean(x * x, axis=-1, keepdims=True) + EPS) * gain


def _sigmoid(x):
    return 1.0 / (1.0 + jnp.exp(-x))


def _bias_tiles_kernel(rb_ref, o_ref):
    h = pl.program_id(0)
    far = rb_ref[REL_BUCKETS - 1, h]
    key = lax.broadcasted_iota(I32, (Q_BLOCK, Q_BLOCK), 0)
    qry = lax.broadcasted_iota(I32, (Q_BLOCK, Q_BLOCK), 1)
    max_exact = REL_BUCKETS // 2
    for delta in range(2):
        n = jnp.maximum(delta * Q_BLOCK + qry - key, 0)
        nf = jnp.maximum(n, 1).astype(F32)
        large = max_exact + (jnp.log(nf / max_exact) / math.log(REL_MAX_DIST / max_exact)
                             * (REL_BUCKETS - max_exact)).astype(I32)
        large = jnp.minimum(large, REL_BUCKETS - 1)
        bucket = jnp.where(n < max_exact, n, large)
        val = jnp.zeros((Q_BLOCK, Q_BLOCK), F32)
        for b in range(REL_BUCKETS):
            val = jnp.where(bucket == b, rb_ref[b, h] - far, val)
        o_ref[delta] = val * LOG2E


def _bias_tiles(rel_bias):
    return pl.pallas_call(
        _bias_tiles_kernel,
        grid=(ATT_HEADS,),
        in_specs=[pl.BlockSpec(memory_space=pltpu.SMEM)],
        out_specs=pl.BlockSpec((2, Q_BLOCK, Q_BLOCK), lambda h: (0, 0, h)),
        out_shape=jax.ShapeDtypeStruct((2, Q_BLOCK, ATT_HEADS * Q_BLOCK), F32),
        name="bias_tiles",
    )(rel_bias)


INPROJ_TM = 256


def _inproj_kernel(x_ref, g_ref, w_ref, o_ref):
    h = _rms(x_ref[...], g_ref[...]).astype(BF16)
    o_ref[...] = jnp.dot(h, w_ref[...], preferred_element_type=F32)


def _in_proj(x2d, gain, w_packed):
    t = x2d.shape[0]
    return pl.pallas_call(
        _inproj_kernel,
        grid=(t // INPROJ_TM,),
        in_specs=[
            pl.BlockSpec((INPROJ_TM, D_MODEL), lambda i: (i, 0)),
            pl.BlockSpec((1, D_MODEL), lambda i: (0, 0)),
            pl.BlockSpec((D_MODEL, PROJ_COLS), lambda i: (0, 0), pipeline_mode=pl.Buffered(1)),
        ],
        out_specs=pl.BlockSpec((INPROJ_TM, PROJ_COLS), lambda i: (i, 0)),
        out_shape=jax.ShapeDtypeStruct((t, PROJ_COLS), F32),
        compiler_params=pltpu.CompilerParams(
            dimension_semantics=("arbitrary",), vmem_limit_bytes=VMEM_LIMIT),
        name="in_proj",
    )(x2d, gain, w_packed)


TILES_PER_CLASS = 2
PAD_TILES = TILES_PER_CLASS - 1
PAD_ROWS = PAD_TILES * Q_BLOCK
ROW_CHUNK = 256
COUNT_ROWS = 64
PAIR_LANES = 2 * Q_BLOCK
assert PAIR_LANES == V7X_MXU_WIDTH and ROW_CHUNK == TILES_PER_CLASS * Q_BLOCK
ONES_ROWS = 16
PV_ROWS = KV_LATENT + ONES_ROWS
NEG_BIG = -1e30


def _attn_block(n_tiles, first_tile, i, qiT, wT, tab_ref, wuv_ref, cm_s, ckvT_s, kidx_s, qT_s, key_s,
                hi_s, lo_s, j_s, l0_s, l1_s, p0_s, p1_s, ot_s, attT_s):
    nkp = n_tiles * Q_BLOCK
    n_sel = float(TOPK_MAX)
    start = pl.multiple_of(first_tile * Q_BLOCK, Q_BLOCK)
    key0 = (first_tile - PAD_TILES) * Q_BLOCK
    q_pos = i * Q_BLOCK + lax.broadcasted_iota(I32, (ROW_CHUNK, Q_BLOCK), 1)
    row_in_chunk = lax.broadcasted_iota(I32, (ROW_CHUNK, Q_BLOCK), 0)
    n_chunks = nkp // ROW_CHUNK

    for rc in range(n_chunks):
        rows = slice(rc * ROW_CHUNK, (rc + 1) * ROW_CHUNK)
        kc = kidx_s[pl.ds(pl.multiple_of(start + rc * ROW_CHUNK, Q_BLOCK), ROW_CHUNK), :]
        acc = jnp.zeros((ROW_CHUNK, Q_BLOCK), F32)
        for hp in range(IDX_HEADS // 2):
            rhs = jnp.concatenate(
                [qiT[(2 * hp) * IDX_DIM:(2 * hp + 1) * IDX_DIM, :],
                 qiT[(2 * hp + 1) * IDX_DIM:(2 * hp + 2) * IDX_DIM, :]], axis=1)
            lg = jnp.dot(kc, rhs, preferred_element_type=F32)
            w0 = wT[MISC_WIDX + 2 * hp:MISC_WIDX + 2 * hp + 1, :]
            w1 = wT[MISC_WIDX + 2 * hp + 1:MISC_WIDX + 2 * hp + 2, :]
            acc = acc + jnp.maximum(lg[:, :Q_BLOCK], 0.0) * w0 + jnp.maximum(lg[:, Q_BLOCK:], 0.0) * w1
        bits = pltpu.bitcast(acc, I32)
        bits = jnp.where(bits == INT_MIN, 0, bits)
        key = jnp.where(bits < 0, bits ^ 0x7FFFFFFF, bits)
        key_pos = key0 + rc * ROW_CHUNK + row_in_chunk
        if rc * ROW_CHUNK < PAD_ROWS:
            key = jnp.where(key_pos >= 0, key, INT_MIN)
        if rc == n_chunks - 1:
            key = jnp.where(key_pos <= q_pos, key, INT_MIN)
        key_s[rows, :] = key

    def count_ones(ones):
        part = jnp.sum(ones.reshape(nkp // COUNT_ROWS, COUNT_ROWS, Q_BLOCK), axis=0)
        return jnp.sum(part, axis=0, keepdims=True)

    def count(pred):
        return count_ones(jnp.where(pred, 1.0, 0.0))

    def count16(pred):
        ones = jnp.where(pred, jnp.int16(1), jnp.int16(0))
        part = ones[0:COUNT_ROWS, :]
        for k in range(1, nkp // COUNT_ROWS):
            part = part + ones[k * COUNT_ROWS:(k + 1) * COUNT_ROWS, :]
        return jnp.sum(part.astype(F32), axis=0, keepdims=True)

    def half_search(half_ref, target):
        def step(bit, thr):
            cand = thr ^ jnp.left_shift(jnp.int32(1), 15 - bit)
            cand = jnp.where(bit == 0, cand + 2 ** 16, cand)
            hit = count16(half_ref[0:nkp, :] >= cand.astype(jnp.int16)) >= target
            return jnp.where(hit, cand, thr)
        return lax.fori_loop(0, 16, step, jnp.full((1, Q_BLOCK), -(2 ** 15), I32))

    keys = key_s[0:nkp, :]

    def value_step(bit, thr):
        cand = thr ^ jnp.left_shift(jnp.int32(1), 31 - bit)
        return jnp.where(count(key_s[0:nkp, :] >= cand) >= n_sel, cand, thr)

    thr = lax.fori_loop(0, 32, value_step, jnp.full((1, Q_BLOCK), INT_MIN, I32))
    need = n_sel - count(keys > thr)
    split = count(keys == thr) > need

    j_s[...] = jnp.full((1, Q_BLOCK), 2 * nkp, I32)

    @pl.when(jnp.max(jnp.where(split, 1.0, 0.0)) > 0.5)
    def _():
        row = lax.broadcasted_iota(I32, (nkp, Q_BLOCK), 0)
        nbits = (2 * nkp - 1).bit_length()

        def index_step(bit, jmax):
            cand = jmax | jnp.left_shift(jnp.int32(1), nbits - 1 - bit)
            f = count_ones(jnp.where(key_s[0:nkp, :] == thr, jnp.where(row < cand, 1.0, 0.0), 0.0))
            return jnp.where(f <= need, cand, jmax)

        j_s[...] = lax.fori_loop(0, nbits, index_step, jnp.zeros((1, Q_BLOCK), I32))

    jmax = j_s[...]
    row_in_tile = lax.broadcasted_iota(I32, (Q_BLOCK, Q_BLOCK), 0)
    for t in range(n_tiles):
        k = key_s[t * Q_BLOCK:(t + 1) * Q_BLOCK, :]
        row = t * Q_BLOCK + row_in_tile
        add = jnp.where(k > thr, 0.0,
                        jnp.where(k == thr, jnp.where(row < jmax, 0.0, NEG_BIG), NEG_BIG))
        add = jnp.where(k == INT_MIN, NEG_BIG, add)
        cm_s[pl.ds(pl.multiple_of(start + t * Q_BLOCK, Q_BLOCK), Q_BLOCK),
             KV_LATENT:2 * KV_LATENT] = add.astype(BF16)

    tiles_per_chunk = ROW_CHUNK // Q_BLOCK

    def unit(logits=None, exps=None, values=None):
        if logits is not None:
            hp_a, lbuf_a = logits
            off = pl.multiple_of(hp_a * PAIR_LANES, PAIR_LANES)
            rhs = qT_s[:, pl.ds(off, PAIR_LANES)]
            m8 = jnp.full((8, PAIR_LANES), NEG_INF, F32)
        if exps is not None:
            lbuf_b, pbuf_b, m_b = exps
        if values is not None:
            hp_c, pbuf_c = values
            acc = None
        for rc in range(n_chunks):
            rows = slice(rc * ROW_CHUNK, (rc + 1) * ROW_CHUNK)
            win = pl.ds(pl.multiple_of(start + rc * ROW_CHUNK, Q_BLOCK), ROW_CHUNK)
            if logits is not None:
                lc = jnp.dot(cm_s[win, :], rhs, preferred_element_type=F32)
                for tt in range(tiles_per_chunk):
                    t = rc * tiles_per_chunk + tt
                    lt = lc[tt * Q_BLOCK:(tt + 1) * Q_BLOCK, :]
                    if t >= n_tiles - 2:
                        lt = lt + tab_ref[n_tiles - 1 - t, :, pl.ds(off, PAIR_LANES)]
                    lbuf_a[t * Q_BLOCK:(t + 1) * Q_BLOCK, :] = lt
                    m8 = jnp.maximum(m8, jnp.max(lt.reshape(Q_BLOCK // 8, 8, PAIR_LANES), axis=0))
            if exps is not None:
                pbuf_b[rows, :] = jnp.exp2(lbuf_b[rows, :] - m_b).astype(BF16)
            if values is not None:
                part = jnp.dot(ckvT_s[:, win], pbuf_c[rows, :], preferred_element_type=F32)
                acc = part if acc is None else acc + part
        if values is not None:
            off_c = pl.multiple_of(hp_c * PAIR_LANES, PAIR_LANES)
            ot_s[:, pl.ds(off_c, PAIR_LANES)] = (
                acc[0:KV_LATENT, :] / acc[KV_LATENT:KV_LATENT + 1, :]).astype(BF16)
        if logits is not None:
            return jnp.max(m8, axis=0, keepdims=True)

    n_pairs = ATT_HEADS // 2

    def two_units(jj, m_odd):
        hp = 2 * jj
        m_even = unit(logits=(hp + 2, l0_s), exps=(l1_s, p1_s, m_odd), values=(hp, p0_s))
        return unit(logits=(hp + 3, l1_s), exps=(l0_s, p0_s, m_even), values=(hp + 1, p1_s))

    m_even = unit(logits=(0, l0_s))
    m_odd = unit(logits=(1, l1_s), exps=(l0_s, p0_s, m_even))
    m_odd = lax.fori_loop(0, n_pairs // 2 - 1, two_units, m_odd)
    unit(exps=(l1_s, p1_s, m_odd), values=(n_pairs - 2, p0_s))
    unit(values=(n_pairs - 1, p1_s))

    for h in range(ATT_HEADS):
        attT_s[h * ATT_HEAD_DIM:(h + 1) * ATT_HEAD_DIM, :] = jnp.dot(
            wuv_ref[h], ot_s[:, h * Q_BLOCK:(h + 1) * Q_BLOCK], preferred_element_type=F32)


def _attn_kernel(q_ref, qi_ref, mq_ref, cr_ref, mk_ref, wukT_ref, wuv_ref, kvn_ref, tab_ref, o_ref,
                 cm_s, ckvT_s, kidx_s, qT_s, key_s, hi_s, lo_s, j_s, l0_s, l1_s, p0_s, p1_s, ot_s, attT_s):
    i = pl.program_id(1)
    seq = cr_ref.shape[0]

    @pl.when(i == 0)
    def _():
        c = _rms(cr_ref[...], kvn_ref[...])
        cm_s[0:PAD_ROWS, :] = jnp.zeros((PAD_ROWS, 2 * KV_LATENT), BF16)
        cm_s[PAD_ROWS:, 0:KV_LATENT] = c.astype(BF16)
        ckvT_s[0:KV_LATENT, 0:PAD_ROWS] = jnp.zeros((KV_LATENT, PAD_ROWS), BF16)
        ckvT_s[0:KV_LATENT, PAD_ROWS:] = c.T.astype(BF16)
        ckvT_s[KV_LATENT:, :] = jnp.ones((ONES_ROWS, seq + PAD_ROWS), BF16)
        kidx_s[0:PAD_ROWS, :] = jnp.zeros((PAD_ROWS, IDX_DIM), BF16)
        kidx_s[PAD_ROWS:, :] = mk_ref[:, MISC_KIDX:MISC_KIDX + IDX_DIM].astype(BF16)
        eye = jnp.where(lax.broadcasted_iota(I32, (Q_BLOCK, Q_BLOCK), 0)
                        == lax.broadcasted_iota(I32, (Q_BLOCK, Q_BLOCK), 1), 1.0, 0.0).astype(BF16)
        for h in range(ATT_HEADS):
            qT_s[KV_LATENT:, h * Q_BLOCK:(h + 1) * Q_BLOCK] = eye

    scale = ATT_HEAD_DIM ** -0.5 * LOG2E
    qb = q_ref[...].astype(BF16)
    for h in range(ATT_HEADS):
        qh = qb[:, h * ATT_HEAD_DIM:(h + 1) * ATT_HEAD_DIM]
        qt = lax.dot_general(wukT_ref[h], qh, (((1,), (1,)), ((), ())), preferred_element_type=F32)
        qT_s[0:KV_LATENT, h * Q_BLOCK:(h + 1) * Q_BLOCK] = (qt * scale).astype(BF16)

    qiT = qi_ref[...].T.astype(BF16)
    wT = mq_ref[...].T

    cls = lax.shift_right_logical(i, int(math.log2(TILES_PER_CLASS)))
    for k in range(seq // Q_BLOCK // TILES_PER_CLASS):
        @pl.when(cls == k)
        def _(k=k):
            _attn_block(TILES_PER_CLASS * (k + 1), i - TILES_PER_CLASS * k, i, qiT, wT, tab_ref,
                        wuv_ref, cm_s, ckvT_s, kidx_s, qT_s, key_s, hi_s, lo_s, j_s, l0_s, l1_s, p0_s, p1_s, ot_s, attT_s)

    o_ref[...] = attT_s[...].T.astype(BF16)


def _dsa_attention(proj, wukT, wuv, kv_norm, tab, batch, seq):
    nqb = seq // Q_BLOCK
    att_dim = ATT_HEADS * ATT_HEAD_DIM
    return pl.pallas_call(
        _attn_kernel,
        grid=(batch, nqb),
        in_specs=[
            pl.BlockSpec((Q_BLOCK, att_dim), lambda b, i: (b * nqb + i, COL_Q // att_dim)),
            pl.BlockSpec((Q_BLOCK, IDX_HEADS * IDX_DIM),
                         lambda b, i: (b * nqb + i, COL_QI // (IDX_HEADS * IDX_DIM))),
            pl.BlockSpec((Q_BLOCK, V7X_LANES), lambda b, i: (b * nqb + i, COL_MISC // V7X_LANES)),
            pl.BlockSpec((seq, KV_LATENT), lambda b, i: (b, COL_CR // KV_LATENT)),
            pl.BlockSpec((seq, V7X_LANES), lambda b, i: (b, COL_MISC // V7X_LANES)),
            pl.BlockSpec((ATT_HEADS, KV_LATENT, ATT_HEAD_DIM), lambda b, i: (0, 0, 0)),
            pl.BlockSpec((ATT_HEADS, ATT_HEAD_DIM, KV_LATENT), lambda b, i: (0, 0, 0)),
            pl.BlockSpec((1, KV_LATENT), lambda b, i: (0, 0)),
            pl.BlockSpec((2, Q_BLOCK, ATT_HEADS * Q_BLOCK), lambda b, i: (0, 0, 0)),
        ],
        out_specs=pl.BlockSpec((Q_BLOCK, att_dim), lambda b, i: (b * nqb + i, 0)),
        out_shape=jax.ShapeDtypeStruct((batch * seq, att_dim), BF16),
        scratch_shapes=[
            pltpu.VMEM((seq + PAD_ROWS, 2 * KV_LATENT), BF16),
            pltpu.VMEM((PV_ROWS, seq + PAD_ROWS), BF16),
            pltpu.VMEM((seq + PAD_ROWS, IDX_DIM), BF16),
            pltpu.VMEM((2 * KV_LATENT, ATT_HEADS * Q_BLOCK), BF16),
            pltpu.VMEM((seq, Q_BLOCK), I32),
            pltpu.VMEM((seq, Q_BLOCK), jnp.int16),
            pltpu.VMEM((seq, Q_BLOCK), jnp.int16),
            pltpu.VMEM((1, Q_BLOCK), I32),
            pltpu.VMEM((seq, PAIR_LANES), F32),
            pltpu.VMEM((seq, PAIR_LANES), F32),
            pltpu.VMEM((seq, PAIR_LANES), BF16),
            pltpu.VMEM((seq, PAIR_LANES), BF16),
            pltpu.VMEM((KV_LATENT, ATT_HEADS * Q_BLOCK), BF16),
            pltpu.VMEM((att_dim, Q_BLOCK), F32),
        ],
        compiler_params=pltpu.CompilerParams(
            dimension_semantics=("arbitrary", "arbitrary"), vmem_limit_bytes=VMEM_LIMIT),
        name="dsa_attn",
    )(proj, proj, proj, proj, proj, wukT, wuv, kv_norm, tab)


TAIL = 8
GROUP_LANES = SSM_HEADS_PER_GROUP * SSM_HEAD_DIM


def _expand_matrices():
    e64 = np.zeros((V7X_LANES, SSM_INNER), np.float32)
    e128 = np.zeros((V7X_LANES, SSM_HEADS * SSM_CHUNK), np.float32)
    for h in range(SSM_HEADS):
        for piece in range(3):
            lane = (MISC_DT + h + piece * SSM_HEADS) % V7X_LANES
            e64[lane, h * SSM_HEAD_DIM:(h + 1) * SSM_HEAD_DIM] = 1.0
            e128[lane, h * SSM_CHUNK:(h + 1) * SSM_CHUNK] = 1.0
    return jnp.asarray(e64, BF16), jnp.asarray(e128, BF16)


def _split3(v):
    hi = v.astype(BF16).astype(F32)
    r1 = v - hi
    mid = r1.astype(BF16).astype(F32)
    lo = r1 - mid
    packed = hi + pltpu.roll(mid, SSM_HEADS, axis=1) + pltpu.roll(lo, 2 * SSM_HEADS, axis=1)
    return packed.astype(BF16)


def _conv_silu(u_ref, tail_s, w_ref, b_ref):
    n = SSM_CHUNK
    u = u_ref[...]
    ext = jnp.concatenate([tail_s[...], u], axis=0)
    out = b_ref[...] + w_ref[SSM_CONV - 1:SSM_CONV, :] * u
    for k in range(1, SSM_CONV):
        out = out + w_ref[SSM_CONV - 1 - k:SSM_CONV - k, :] * pltpu.roll(ext, k, axis=0)[TAIL:, :]
    tail_s[...] = u[n - TAIL:n, :]
    return out * _sigmoid(out)


def _ssd_kernel(z_ref, xs_ref, bm_ref, cm_ref, misc_ref, cwx_ref, cwb_ref, cwc_ref, cbx_ref, cbb_ref,
                cbc_ref, dtb_ref, alog_ref, dsk_ref, gn_ref, e64_ref, e128_ref, o_ref,
                xtail_s, btail_s, ctail_s, state_s):
    n = SSM_CHUNK

    @pl.when(pl.program_id(1) == 0)
    def _():
        state_s[...] = jnp.zeros_like(state_s)
        xtail_s[...] = jnp.zeros_like(xtail_s)
        btail_s[...] = jnp.zeros_like(btail_s)
        ctail_s[...] = jnp.zeros_like(ctail_s)

    xs = _conv_silu(xs_ref, xtail_s, cwx_ref, cbx_ref)
    bm = _conv_silu(bm_ref, btail_s, cwb_ref, cbb_ref)
    cm = _conv_silu(cm_ref, ctail_s, cwc_ref, cbc_ref)

    lane = lax.broadcasted_iota(I32, (n, V7X_LANES), 1)
    row = lax.broadcasted_iota(I32, (n, V7X_LANES), 0)
    on_dt = (lane >= MISC_DT) & (lane < MISC_DT + SSM_HEADS)

    v = misc_ref[...] + dtb_ref[...]
    dt = jnp.maximum(v, 0.0) + jnp.log1p(jnp.exp(-jnp.abs(v)))
    dt = jnp.where(on_dt, dt, 0.0)
    a_neg = -jnp.exp(alog_ref[...])
    acum = dt * a_neg
    k = 1
    while k < n:
        acum = acum + jnp.where(row >= k, pltpu.roll(acum, k, axis=0), 0.0)
        k *= 2

    e64 = e64_ref[...]
    dt_e = jnp.dot(_split3(dt), e64, preferred_element_type=F32)
    acum_p = _split3(acum)
    acum_e = jnp.dot(acum_p, e64, preferred_element_type=F32)
    acum_cb = jnp.dot(acum_p, e128_ref[...], preferred_element_type=F32)
    acum_t = acum.T

    last = acum_e[n - 1:n, :]
    xdt = xs * dt_e
    xds_b = (xdt * jnp.exp(last - acum_e)).astype(BF16)
    chunk_decay = jnp.exp(last)
    ea_e = jnp.exp(acum_e)

    tri = row >= lane
    first_head = lane < SSM_HEAD_DIM
    y_groups = []
    for g in range(SSM_GROUPS):
        gl = slice(g * GROUP_LANES, (g + 1) * GROUP_LANES)
        bmg = bm[:, g * SSM_STATE:(g + 1) * SSM_STATE]
        cmg_b = cm[:, g * SSM_STATE:(g + 1) * SSM_STATE].astype(BF16)
        cb = lax.dot_general(cmg_b, bmg.astype(BF16), (((1,), (1,)), ((), ())),
                             preferred_element_type=F32)
        y_pairs = []
        for pr in range(SSM_HEADS_PER_GROUP // 2):
            h0 = g * SSM_HEADS_PER_GROUP + 2 * pr
            mats = []
            for h in (h0, h0 + 1):
                seg = acum_cb[:, h * n:(h + 1) * n] - acum_t[MISC_DT + h:MISC_DT + h + 1, :]
                mats.append((cb * jnp.exp(jnp.where(tri, seg, NEG_INF))).astype(BF16))
            xp = xdt[:, h0 * SSM_HEAD_DIM:(h0 + 2) * SSM_HEAD_DIM]
            rhs = jnp.concatenate([jnp.where(first_head, xp, 0.0), jnp.where(first_head, 0.0, xp)],
                                  axis=0).astype(BF16)
            y_pairs.append(jnp.dot(jnp.concatenate(mats, axis=1), rhs, preferred_element_type=F32))
        prev = state_s[g]
        y_off = jnp.dot(cmg_b, prev.astype(BF16), preferred_element_type=F32) * ea_e[:, gl]
        state_s[g] = prev * chunk_decay[:, gl] + jnp.dot(bmg.T.astype(BF16), xds_b[:, gl],
                                                         preferred_element_type=F32)
        y_groups.append(jnp.concatenate(y_pairs, axis=1) + y_off)

    z = z_ref[...]
    gz = z * _sigmoid(z)
    outs = []
    for g in range(SSM_GROUPS):
        gl = slice(g * GROUP_LANES, (g + 1) * GROUP_LANES)
        u = (y_groups[g] + dsk_ref[:, gl] * xs[:, gl]) * gz[:, gl]
        outs.append(u * lax.rsqrt(jnp.mean(u * u, axis=-1, keepdims=True) + EPS) * gn_ref[:, gl])
    o_ref[...] = jnp.concatenate(outs, axis=1).astype(BF16)


def _ssd(proj, conv_w, conv_b, dt_bias, a_log, d_skip, ssm_norm, batch, seq):
    nch = seq // SSM_CHUNK
    n = SSM_CHUNK
    cwx, cwb, cwc = (conv_w[:, :SSM_INNER], conv_w[:, SSM_INNER:SSM_INNER + SSM_BC],
                     conv_w[:, SSM_INNER + SSM_BC:])
    cbx, cbb, cbc = (conv_b[None, :SSM_INNER], conv_b[None, SSM_INNER:SSM_INNER + SSM_BC],
                     conv_b[None, SSM_INNER + SSM_BC:])
    pad_l, pad_r = MISC_DT, V7X_LANES - MISC_DT - SSM_HEADS
    dtb = jnp.pad(dt_bias, (pad_l, pad_r))[None, :]
    alog = jnp.pad(a_log, (pad_l, pad_r))[None, :]
    dsk = jnp.repeat(d_skip, SSM_HEAD_DIM)[None, :]
    gn = ssm_norm[None, :]
    e64, e128 = _expand_matrices()
    full = lambda shape: pl.BlockSpec(shape, lambda b, c: (0,) * len(shape))
    rows = lambda b, c: b * nch + c
    return pl.pallas_call(
        _ssd_kernel,
        grid=(batch, nch),
        in_specs=[
            pl.BlockSpec((n, SSM_INNER), lambda b, c: (rows(b, c), COL_Z // SSM_INNER)),
            pl.BlockSpec((n, SSM_INNER), lambda b, c: (rows(b, c), COL_XS // SSM_INNER)),
            pl.BlockSpec((n, SSM_BC), lambda b, c: (rows(b, c), COL_BM // SSM_BC)),
            pl.BlockSpec((n, SSM_BC), lambda b, c: (rows(b, c), COL_CM // SSM_BC)),
            pl.BlockSpec((n, V7X_LANES), lambda b, c: (rows(b, c), COL_MISC // V7X_LANES)),
            full((SSM_CONV, SSM_INNER)), full((SSM_CONV, SSM_BC)), full((SSM_CONV, SSM_BC)),
            full((1, SSM_INNER)), full((1, SSM_BC)), full((1, SSM_BC)),
            full((1, V7X_LANES)), full((1, V7X_LANES)),
            full((1, SSM_INNER)), full((1, SSM_INNER)),
            full((V7X_LANES, SSM_INNER)), full((V7X_LANES, SSM_HEADS * n)),
        ],
        out_specs=pl.BlockSpec((n, SSM_INNER), lambda b, c: (rows(b, c), 0)),
        out_shape=jax.ShapeDtypeStruct((batch * seq, SSM_INNER), BF16),
        scratch_shapes=[
            pltpu.VMEM((TAIL, SSM_INNER), F32),
            pltpu.VMEM((TAIL, SSM_BC), F32),
            pltpu.VMEM((TAIL, SSM_BC), F32),
            pltpu.VMEM((SSM_GROUPS, SSM_STATE, GROUP_LANES), F32),
        ],
        compiler_params=pltpu.CompilerParams(
            dimension_semantics=("arbitrary", "arbitrary"), vmem_limit_bytes=VMEM_LIMIT),
        name="ssd",
    )(proj, proj, proj, proj, proj, cwx, cwb, cwc, cbx, cbb, cbc, dtb, alog, dsk, gn, e64, e128)


MERGE_TM = 512


def _merge_kernel(att_ref, yn_ref, ga_ref, gs_ref, x_ref, wao_ref, wso_ref, wo_ref, gf_ref,
                  x1_ref, h2_ref):
    y_att = jnp.dot(att_ref[...], wao_ref[...], preferred_element_type=F32)
    y_ssm = jnp.dot(yn_ref[...], wso_ref[...], preferred_element_type=F32)
    merged = _sigmoid(ga_ref[...]) * y_att + _sigmoid(gs_ref[...]) * y_ssm
    x1 = x_ref[...] + jnp.dot(merged.astype(BF16), wo_ref[...], preferred_element_type=F32)
    x1_ref[...] = x1
    h2_ref[...] = _rms(x1, gf_ref[...]).astype(BF16)


def _merge(att, yn, proj, x2d, wao, wso, wo, norm_ffn):
    t = x2d.shape[0]
    tm = MERGE_TM
    full = lambda shape: pl.BlockSpec(shape, lambda i: (0,) * len(shape))
    return pl.pallas_call(
        _merge_kernel,
        grid=(t // tm,),
        in_specs=[
            pl.BlockSpec((tm, D_MODEL), lambda i: (i, 0)),
            pl.BlockSpec((tm, SSM_INNER), lambda i: (i, 0)),
            pl.BlockSpec((tm, D_MODEL), lambda i: (i, COL_GA // D_MODEL)),
            pl.BlockSpec((tm, D_MODEL), lambda i: (i, COL_GS // D_MODEL)),
            pl.BlockSpec((tm, D_MODEL), lambda i: (i, 0)),
            full((D_MODEL, D_MODEL)), full((SSM_INNER, D_MODEL)), full((D_MODEL, D_MODEL)),
            full((1, D_MODEL)),
        ],
        out_specs=[pl.BlockSpec((tm, D_MODEL), lambda i: (i, 0)),
                   pl.BlockSpec((tm, D_MODEL), lambda i: (i, 0))],
        out_shape=[jax.ShapeDtypeStruct((t, D_MODEL), F32),
                   jax.ShapeDtypeStruct((t, D_MODEL), BF16)],
        compiler_params=pltpu.CompilerParams(
            dimension_semantics=("arbitrary",), vmem_limit_bytes=VMEM_LIMIT),
        name="merge",
    )(att, yn, proj, proj, x2d, wao, wso, wo, norm_ffn)


FFN_TM = 512
FFN_HALO = 16
FFN_FC = V7X_MXU_WIDTH


def _ffn_kernel(h_ref, halo_ref, x1_ref, wup_ref, cw_ref, cb_ref, wdn_ref, gf_ref, o_ref,
                acc_s, *, tiles_per_seq):
    tm = FFN_TM
    keep = jnp.where(pl.program_id(0) % tiles_per_seq == 0, 0.0, 1.0)
    hcat = jnp.concatenate([halo_ref[...], h_ref[...]], axis=0)
    n_chunks = FFN_DIM // FFN_FC

    def up(c, base):
        return jnp.dot(hcat, wup_ref[:, base + c * FFN_FC:base + (c + 1) * FFN_FC],
                       preferred_element_type=F32)

    def conv(u, c, base):
        cols = slice(base + c * FFN_FC, base + (c + 1) * FFN_FC)
        u = jnp.concatenate([u[0:FFN_HALO, :] * keep, u[FFN_HALO:, :]], axis=0)
        out = cb_ref[:, cols] + cw_ref[FFN_CONV - 1:FFN_CONV, cols] * u[FFN_HALO:, :]
        for k in range(1, FFN_CONV):
            out = out + (cw_ref[FFN_CONV - 1 - k:FFN_CONV - k, cols]
                         * pltpu.roll(u, k, axis=0)[FFN_HALO:, :])
        return out

    ug, uv = up(0, 0), up(0, FFN_DIM)
    for c in range(n_chunks):
        last = c + 1 == n_chunks
        ug_next = None if last else up(c + 1, 0)
        gate = conv(ug, c, 0)
        uv_next = None if last else up(c + 1, FFN_DIM)
        val = conv(uv, c, FFN_DIM)
        ug, uv = ug_next, uv_next
        act = (gate * _sigmoid(gate) * val).astype(BF16)
        contrib = jnp.dot(act, wdn_ref[c * FFN_FC:(c + 1) * FFN_FC, :], preferred_element_type=F32)
        if c == 0:
            acc_s[...] = contrib
        else:
            acc_s[...] += contrib
    o_ref[...] = _rms(x1_ref[...] + acc_s[...], gf_ref[...])


def _ffn(h2, x1, wup, conv_w, conv_b, wdn, norm_final, seq):
    t = h2.shape[0]
    tm = FFN_TM
    halo_blocks = tm // FFN_HALO
    single = pl.Buffered(1)
    full = lambda shape, **kw: pl.BlockSpec(shape, lambda i: (0,) * len(shape), **kw)
    return pl.pallas_call(
        functools.partial(_ffn_kernel, tiles_per_seq=seq // tm),
        grid=(t // tm,),
        in_specs=[
            pl.BlockSpec((tm, D_MODEL), lambda i: (i, 0)),
            pl.BlockSpec((FFN_HALO, D_MODEL), lambda i: (jnp.maximum(i * halo_blocks - 1, 0), 0)),
            pl.BlockSpec((tm, D_MODEL), lambda i: (i, 0)),
            full((D_MODEL, 2 * FFN_DIM), pipeline_mode=single),
            full((FFN_CONV, 2 * FFN_DIM)),
            full((1, 2 * FFN_DIM)),
            full((FFN_DIM, D_MODEL), pipeline_mode=single),
            full((1, D_MODEL)),
        ],
        out_specs=pl.BlockSpec((tm, D_MODEL), lambda i: (i, 0)),
        out_shape=jax.ShapeDtypeStruct((t, D_MODEL), F32),
        scratch_shapes=[pltpu.VMEM((tm, D_MODEL), F32)],
        compiler_params=pltpu.CompilerParams(
            dimension_semantics=("arbitrary",), vmem_limit_bytes=VMEM_LIMIT),
        name="ffn",
    )(h2, h2, x1, wup, conv_w, conv_b, wdn, norm_final)


def kernel(x, rel_bias, norm_mix, w_in, kv_norm, w_uk, w_uv, conv_ssm_w, conv_ssm_b, dt_bias, a_log,
           d_skip, ssm_norm, w_att_out, w_ssm_out, w_out, norm_ffn, w_ffn_up, conv_ffn_w, conv_ffn_b,
           w_ffn_down, norm_final):
    batch, seq, _ = x.shape
    assert norm_mix.shape[0] == 1, "single layer"
    assert seq % (TILES_PER_CLASS * Q_BLOCK) == 0 and min(TOPK_MAX, seq // 4) == TOPK_MAX
    x2d = x.reshape(batch * seq, D_MODEL)

    tab = _bias_tiles(rel_bias)
    proj = _in_proj(x2d, norm_mix[0][None, :], _pack_w_in(w_in[0]))
    att = _dsa_attention(proj, jnp.swapaxes(w_uk[0], 1, 2).astype(BF16), w_uv[0].astype(BF16),
                         kv_norm[0][None, :], tab, batch, seq)
    yn = _ssd(proj, conv_ssm_w[0], conv_ssm_b[0], dt_bias[0], a_log[0], d_skip[0], ssm_norm[0],
              batch, seq)
    x1, h2 = _merge(att, yn, proj, x2d, w_att_out[0].astype(BF16), w_ssm_out[0].astype(BF16),
                    w_out[0].astype(BF16), norm_ffn[0][None, :])
    out = _ffn(h2, x1, w_ffn_up[0].astype(BF16), conv_ffn_w[0], conv_ffn_b[0][None, :],
               w_ffn_down[0].astype(BF16), norm_final[None, :], seq)
    return out.reshape(batch, seq, D_MODEL)
```

```python
import functools
import math

import numpy as np
import jax
import jax.numpy as jnp
from jax import lax
from jax.experimental import pallas as pl
from jax.experimental.pallas import tpu as pltpu

F32 = jnp.float32
BF16 = jnp.bfloat16
I32 = jnp.int32

D_MODEL = 1024
ATT_HEADS = 16
ATT_HEAD_DIM = 64
KV_LATENT = 128
IDX_HEADS = 8
IDX_DIM = 64
TOPK_MAX = 256
Q_BLOCK = 128
REL_BUCKETS = 32
REL_MAX_DIST = 128
SSM_INNER = 2 * D_MODEL
SSM_HEAD_DIM = 64
SSM_HEADS = SSM_INNER // SSM_HEAD_DIM
SSM_GROUPS = 4
SSM_HEADS_PER_GROUP = SSM_HEADS // SSM_GROUPS
SSM_STATE = 128
SSM_CONV = 4
SSM_CHUNK = 128
SSM_BC = SSM_GROUPS * SSM_STATE
FFN_DIM = 2816
FFN_CONV = 3
EPS = 1e-6

V7X_LANES = 128
V7X_MXU_WIDTH = 256
V7X_VMEM_BYTES = 64 * 1024 * 1024
VMEM_LIMIT = 56 * 1024 * 1024

INT_MIN = -(2 ** 31)
NEG_INF = float("-inf")
LOG2E = math.log2(math.e)

COL_Z = 0
COL_XS = COL_Z + SSM_INNER
COL_Q = COL_XS + SSM_INNER
COL_GA = COL_Q + D_MODEL
COL_GS = COL_GA + D_MODEL
COL_QI = COL_GS + D_MODEL
COL_BM = COL_QI + IDX_HEADS * IDX_DIM
COL_CM = COL_BM + SSM_BC
COL_CR = COL_CM + SSM_BC
COL_MISC = COL_CR + KV_LATENT
PROJ_COLS = COL_MISC + V7X_LANES
MISC_KIDX = 0
MISC_WIDX = IDX_DIM
MISC_DT = IDX_DIM + IDX_HEADS


def _pack_moves():
    sizes = (ATT_HEADS * ATT_HEAD_DIM, KV_LATENT, IDX_HEADS * IDX_DIM, IDX_DIM, IDX_HEADS,
             SSM_INNER, SSM_INNER, SSM_BC, SSM_BC, SSM_HEADS, D_MODEL, D_MODEL)
    src = np.cumsum((0,) + sizes)
    dst = (COL_Q, COL_CR, COL_QI, COL_MISC + MISC_KIDX, COL_MISC + MISC_WIDX, COL_Z, COL_XS, COL_BM,
           COL_CM, COL_MISC + MISC_DT, COL_GA, COL_GS)
    return [(int(s), int(d), int(n)) for s, d, n in zip(src[:-1], dst, sizes)]


PACK_ROWS = 128


def _pack_kernel(w_ref, o_ref):
    misc = []
    for s, d, n in _pack_moves():
        if d >= COL_MISC:
            assert d == COL_MISC + sum(p.shape[1] for p in misc)
            misc.append(w_ref[:, s:s + n])
        else:
            o_ref[:, d:d + n] = w_ref[:, s:s + n].astype(BF16)
    used = sum(p.shape[1] for p in misc)
    misc.append(jnp.zeros((PACK_ROWS, V7X_LANES - used), F32))
    o_ref[:, COL_MISC:PROJ_COLS] = jnp.concatenate(misc, axis=1).astype(BF16)


def _pack_w_in(w):
    k, cols = w.shape
    return pl.pallas_call(
        _pack_kernel,
        grid=(k // PACK_ROWS,),
        in_specs=[pl.BlockSpec((PACK_ROWS, cols), lambda i: (i, 0))],
        out_specs=pl.BlockSpec((PACK_ROWS, PROJ_COLS), lambda i: (i, 0)),
        out_shape=jax.ShapeDtypeStruct((k, PROJ_COLS), BF16),
        name="pack_w_in",
    )(w)


def _rms(x, gain):
    return x * lax.rsqrt(jnp.mean(x * x, axis=-1, keepdims=True) + EPS) * gain


def _sigmoid(x):
    return 1.0 / (1.0 + jnp.exp(-x))


def _bias_tiles_kernel(rb_ref, o_ref):
    h = pl.program_id(0)
    far = rb_ref[REL_BUCKETS - 1, h]
    key = lax.broadcasted_iota(I32, (Q_BLOCK, Q_BLOCK), 0)
    qry = lax.broadcasted_iota(I32, (Q_BLOCK, Q_BLOCK), 1)
    max_exact = REL_BUCKETS // 2
    for delta in range(2):
        n = jnp.maximum(delta * Q_BLOCK + qry - key, 0)
        nf = jnp.maximum(n, 1).astype(F32)
        large = max_exact + (jnp.log(nf / max_exact) / math.log(REL_MAX_DIST / max_exact)
                             * (REL_BUCKETS - max_exact)).astype(I32)
        large = jnp.minimum(large, REL_BUCKETS - 1)
        bucket = jnp.where(n < max_exact, n, large)
        val = jnp.zeros((Q_BLOCK, Q_BLOCK), F32)
        for b in range(REL_BUCKETS):
            val = jnp.where(bucket == b, rb_ref[b, h] - far, val)
        o_ref[delta] = val * LOG2E


def _bias_tiles(rel_bias):
    return pl.pallas_call(
        _bias_tiles_kernel,
        grid=(ATT_HEADS,),
        in_specs=[pl.BlockSpec(memory_space=pltpu.SMEM)],
        out_specs=pl.BlockSpec((2, Q_BLOCK, Q_BLOCK), lambda h: (0, 0, h)),
        out_shape=jax.ShapeDtypeStruct((2, Q_BLOCK, ATT_HEADS * Q_BLOCK), F32),
        name="bias_tiles",
    )(rel_bias)


INPROJ_TM = 256


def _inproj_kernel(x_ref, g_ref, w_ref, o_ref):
    h = _rms(x_ref[...], g_ref[...]).astype(BF16)
    o_ref[...] = jnp.dot(h, w_ref[...], preferred_element_type=F32)


def _in_proj(x2d, gain, w_packed):
    t = x2d.shape[0]
    return pl.pallas_call(
        _inproj_kernel,
        grid=(t // INPROJ_TM,),
        in_specs=[
            pl.BlockSpec((INPROJ_TM, D_MODEL), lambda i: (i, 0)),
            pl.BlockSpec((1, D_MODEL), lambda i: (0, 0)),
            pl.BlockSpec((D_MODEL, PROJ_COLS), lambda i: (0, 0), pipeline_mode=pl.Buffered(1)),
        ],
        out_specs=pl.BlockSpec((INPROJ_TM, PROJ_COLS), lambda i: (i, 0)),
        out_shape=jax.ShapeDtypeStruct((t, PROJ_COLS), F32),
        compiler_params=pltpu.CompilerParams(
            dimension_semantics=("arbitrary",), vmem_limit_bytes=VMEM_LIMIT),
        name="in_proj",
    )(x2d, gain, w_packed)


TILES_PER_CLASS = 2
PAD_TILES = TILES_PER_CLASS - 1
PAD_ROWS = PAD_TILES * Q_BLOCK
ROW_CHUNK = 256
COUNT_ROWS = 64
PAIR_LANES = 2 * Q_BLOCK
assert PAIR_LANES == V7X_MXU_WIDTH and ROW_CHUNK == TILES_PER_CLASS * Q_BLOCK
ONES_ROWS = 16
PV_ROWS = KV_LATENT + ONES_ROWS
NEG_BIG = -1e30


SUBS = TILES_PER_CLASS


def _aligned(x, m):
    return x if isinstance(x, int) else pl.multiple_of(x, m)


def _attn_step(k, qiT, wT, tab_ref, cm_s, ckvT_s, kidx_s, qT_s, key_s, j_s, l0_s, l1_s, p0_s, p1_s,
               ot_s):
    n_tiles = TILES_PER_CLASS * (k + 1)
    nkp = n_tiles * Q_BLOCK
    n_chunks = nkp // ROW_CHUNK
    n_sel = float(TOPK_MAX)
    lane = lax.broadcasted_iota(I32, (ROW_CHUNK, Q_BLOCK), 1)
    row_in_chunk = lax.broadcasted_iota(I32, (ROW_CHUNK, Q_BLOCK), 0)

    for sub in range(SUBS):
        start = sub * Q_BLOCK
        key0 = (sub - PAD_TILES) * Q_BLOCK
        q_pos = (SUBS * k + sub) * Q_BLOCK + lane
        ql = slice(sub * Q_BLOCK, (sub + 1) * Q_BLOCK)
        for rc in range(n_chunks):
            rows = slice(rc * ROW_CHUNK, (rc + 1) * ROW_CHUNK)
            kc = kidx_s[start + rc * ROW_CHUNK:start + (rc + 1) * ROW_CHUNK, :]
            acc = jnp.zeros((ROW_CHUNK, Q_BLOCK), F32)
            for hp in range(IDX_HEADS // 2):
                rhs = jnp.concatenate(
                    [qiT[(2 * hp) * IDX_DIM:(2 * hp + 1) * IDX_DIM, ql],
                     qiT[(2 * hp + 1) * IDX_DIM:(2 * hp + 2) * IDX_DIM, ql]], axis=1)
                lg = jnp.dot(kc, rhs, preferred_element_type=F32)
                w0 = wT[MISC_WIDX + 2 * hp:MISC_WIDX + 2 * hp + 1, ql]
                w1 = wT[MISC_WIDX + 2 * hp + 1:MISC_WIDX + 2 * hp + 2, ql]
                acc = (acc + jnp.maximum(lg[:, :Q_BLOCK], 0.0) * w0
                       + jnp.maximum(lg[:, Q_BLOCK:], 0.0) * w1)
            bits = pltpu.bitcast(acc, I32)
            bits = jnp.where(bits == INT_MIN, 0, bits)
            key = jnp.where(bits < 0, bits ^ 0x7FFFFFFF, bits)
            key_pos = key0 + rc * ROW_CHUNK + row_in_chunk
            if key0 + rc * ROW_CHUNK < 0:
                key = jnp.where(key_pos >= 0, key, INT_MIN)
            if rc == n_chunks - 1:
                key = jnp.where(key_pos <= q_pos, key, INT_MIN)
            key_s[sub, rows, :] = key

    def count_ones(ones):
        part = jnp.sum(ones.reshape(nkp // COUNT_ROWS, COUNT_ROWS, Q_BLOCK), axis=0)
        return jnp.sum(part, axis=0, keepdims=True)

    def count(pred):
        return count_ones(jnp.where(pred, 1.0, 0.0))

    def value_step(bit, thrs):
        flip = jnp.left_shift(jnp.int32(1), 31 - bit)
        cands = [thr ^ flip for thr in thrs]
        hits = [count(key_s[sub, 0:nkp, :] >= cands[sub]) >= n_sel for sub in range(SUBS)]
        return tuple(jnp.where(hits[sub], cands[sub], thrs[sub]) for sub in range(SUBS))

    thrs = lax.fori_loop(0, 32, value_step,
                         tuple(jnp.full((1, Q_BLOCK), INT_MIN, I32) for _ in range(SUBS)))

    row_in_tile = lax.broadcasted_iota(I32, (Q_BLOCK, Q_BLOCK), 0)
    for sub in range(SUBS):
        thr = thrs[sub]
        keys = key_s[sub, 0:nkp, :]
        need = n_sel - count(keys > thr)
        split = count(keys == thr) > need

        j_s[...] = jnp.full((1, Q_BLOCK), 2 * nkp, I32)

        @pl.when(jnp.max(jnp.where(split, 1.0, 0.0)) > 0.5)
        def _(sub=sub, thr=thr, need=need):
            row = lax.broadcasted_iota(I32, (nkp, Q_BLOCK), 0)
            nbits = (2 * nkp - 1).bit_length()

            def index_step(bit, jmax):
                cand = jmax | jnp.left_shift(jnp.int32(1), nbits - 1 - bit)
                f = count_ones(jnp.where(key_s[sub, 0:nkp, :] == thr,
                                         jnp.where(row < cand, 1.0, 0.0), 0.0))
                return jnp.where(f <= need, cand, jmax)

            j_s[...] = lax.fori_loop(0, nbits, index_step, jnp.zeros((1, Q_BLOCK), I32))

        jmax = j_s[...]
        for t in range(n_tiles):
            kt = key_s[sub, t * Q_BLOCK:(t + 1) * Q_BLOCK, :]
            row = t * Q_BLOCK + row_in_tile
            add = jnp.where(kt > thr, 0.0,
                            jnp.where(kt == thr, jnp.where(row < jmax, 0.0, NEG_BIG), NEG_BIG))
            add = jnp.where(kt == INT_MIN, NEG_BIG, add)
            cm_s[sub, (sub + t) * Q_BLOCK:(sub + t + 1) * Q_BLOCK,
                 KV_LATENT:2 * KV_LATENT] = add.astype(BF16)

    tiles_per_chunk = ROW_CHUNK // Q_BLOCK
    n_pairs = ATT_HEADS // 2
    n_units = SUBS * n_pairs
    pair_bits = n_pairs.bit_length() - 1
    assert n_pairs == 1 << pair_bits

    def unit(logits=None, exps=None, values=None):
        if logits is not None:
            u_a, lbuf_a = logits
            sub_a = u_a >> pair_bits
            start_a = sub_a * Q_BLOCK
            off_t = _aligned((u_a & (n_pairs - 1)) * PAIR_LANES, PAIR_LANES)
            rhs = qT_s[:, pl.ds(_aligned(u_a * PAIR_LANES, PAIR_LANES), PAIR_LANES)]
            m8 = jnp.full((8, PAIR_LANES), NEG_INF, F32)
        if exps is not None:
            lbuf_b, pbuf_b, m_b = exps
        if values is not None:
            u_c, pbuf_c = values
            start_c = (u_c >> pair_bits) * Q_BLOCK
            acc = None
        for rc in range(n_chunks):
            rows = slice(rc * ROW_CHUNK, (rc + 1) * ROW_CHUNK)
            if logits is not None:
                win = pl.ds(_aligned(start_a + rc * ROW_CHUNK, Q_BLOCK), ROW_CHUNK)
                lc = jnp.dot(cm_s[sub_a, win, :], rhs, preferred_element_type=F32)
                for tt in range(tiles_per_chunk):
                    t = rc * tiles_per_chunk + tt
                    lt = lc[tt * Q_BLOCK:(tt + 1) * Q_BLOCK, :]
                    if t >= n_tiles - 2:
                        lt = lt + tab_ref[n_tiles - 1 - t, :, pl.ds(off_t, PAIR_LANES)]
                    lbuf_a[t * Q_BLOCK:(t + 1) * Q_BLOCK, :] = lt
                    m8 = jnp.maximum(m8, jnp.max(lt.reshape(Q_BLOCK // 8, 8, PAIR_LANES), axis=0))
            if exps is not None:
                pbuf_b[rows, :] = jnp.exp2(lbuf_b[rows, :] - m_b).astype(BF16)
            if values is not None:
                win = pl.ds(_aligned(start_c + rc * ROW_CHUNK, Q_BLOCK), ROW_CHUNK)
                part = jnp.dot(ckvT_s[:, win], pbuf_c[rows, :], preferred_element_type=F32)
                acc = part if acc is None else acc + part
        if values is not None:
            ot_s[:, pl.ds(_aligned(u_c * PAIR_LANES, PAIR_LANES), PAIR_LANES)] = (
                acc[0:KV_LATENT, :] / acc[KV_LATENT:KV_LATENT + 1, :]).astype(BF16)
        if logits is not None:
            return jnp.max(m8, axis=0, keepdims=True)

    def two_units(jj, m_odd):
        u = 2 * jj
        m_even = unit(logits=(u + 2, l0_s), exps=(l1_s, p1_s, m_odd), values=(u, p0_s))
        return unit(logits=(u + 3, l1_s), exps=(l0_s, p0_s, m_even), values=(u + 1, p1_s))

    m_even = unit(logits=(0, l0_s))
    m_odd = unit(logits=(1, l1_s), exps=(l0_s, p0_s, m_even))
    m_odd = lax.fori_loop(0, n_units // 2 - 1, two_units, m_odd)
    unit(exps=(l1_s, p1_s, m_odd), values=(n_units - 2, p0_s))
    unit(values=(n_units - 1, p1_s))


def _attn_kernel(q_ref, qi_ref, mq_ref, cr_ref, mk_ref, wukT_ref, wuv_ref, kvn_ref, tab_ref, o_ref,
                 cm_s, ckvT_s, kidx_s, qT_s, key_s, j_s, l0_s, l1_s, p0_s, p1_s, ot_s, attT_s):
    step = pl.program_id(1)
    seq = cr_ref.shape[0]

    @pl.when(step == 0)
    def _():
        c = _rms(cr_ref[...], kvn_ref[...])
        for sub in range(SUBS):
            cm_s[sub, 0:PAD_ROWS, :] = jnp.zeros((PAD_ROWS, 2 * KV_LATENT), BF16)
            cm_s[sub, PAD_ROWS:, 0:KV_LATENT] = c.astype(BF16)
        ckvT_s[0:KV_LATENT, 0:PAD_ROWS] = jnp.zeros((KV_LATENT, PAD_ROWS), BF16)
        ckvT_s[0:KV_LATENT, PAD_ROWS:] = c.T.astype(BF16)
        ckvT_s[KV_LATENT:, :] = jnp.ones((ONES_ROWS, seq + PAD_ROWS), BF16)
        kidx_s[0:PAD_ROWS, :] = jnp.zeros((PAD_ROWS, IDX_DIM), BF16)
        kidx_s[PAD_ROWS:, :] = mk_ref[:, MISC_KIDX:MISC_KIDX + IDX_DIM].astype(BF16)
        eye = jnp.where(lax.broadcasted_iota(I32, (Q_BLOCK, Q_BLOCK), 0)
                        == lax.broadcasted_iota(I32, (Q_BLOCK, Q_BLOCK), 1), 1.0, 0.0).astype(BF16)
        for blk in range(SUBS * ATT_HEADS):
            qT_s[KV_LATENT:, blk * Q_BLOCK:(blk + 1) * Q_BLOCK] = eye

    scale = ATT_HEAD_DIM ** -0.5 * LOG2E
    qb = q_ref[...].astype(BF16)
    for sub in range(SUBS):
        for h in range(ATT_HEADS):
            qh = qb[sub * Q_BLOCK:(sub + 1) * Q_BLOCK, h * ATT_HEAD_DIM:(h + 1) * ATT_HEAD_DIM]
            qt = lax.dot_general(wukT_ref[h], qh, (((1,), (1,)), ((), ())),
                                 preferred_element_type=F32)
            blk = sub * ATT_HEADS + h
            qT_s[0:KV_LATENT, blk * Q_BLOCK:(blk + 1) * Q_BLOCK] = (qt * scale).astype(BF16)

    qiT = qi_ref[...].T.astype(BF16)
    wT = mq_ref[...].T

    for k in range(seq // Q_BLOCK // SUBS):
        @pl.when(step == k)
        def _(k=k):
            _attn_step(k, qiT, wT, tab_ref, cm_s, ckvT_s, kidx_s, qT_s, key_s, j_s, l0_s, l1_s,
                       p0_s, p1_s, ot_s)

    for sub in range(SUBS):
        for h in range(ATT_HEADS):
            blk = sub * ATT_HEADS + h
            attT_s[h * ATT_HEAD_DIM:(h + 1) * ATT_HEAD_DIM, sub * Q_BLOCK:(sub + 1) * Q_BLOCK] = (
                jnp.dot(wuv_ref[h], ot_s[:, blk * Q_BLOCK:(blk + 1) * Q_BLOCK],
                        preferred_element_type=F32))
    o_ref[...] = attT_s[...].T.astype(BF16)


def _dsa_attention(proj, wukT, wuv, kv_norm, tab, batch, seq):
    rows = SUBS * Q_BLOCK
    nsteps = seq // rows
    att_dim = ATT_HEADS * ATT_HEAD_DIM
    return pl.pallas_call(
        _attn_kernel,
        grid=(batch, nsteps),
        in_specs=[
            pl.BlockSpec((rows, att_dim), lambda b, i: (b * nsteps + i, COL_Q // att_dim)),
            pl.BlockSpec((rows, IDX_HEADS * IDX_DIM),
                         lambda b, i: (b * nsteps + i, COL_QI // (IDX_HEADS * IDX_DIM))),
            pl.BlockSpec((rows, V7X_LANES), lambda b, i: (b * nsteps + i, COL_MISC // V7X_LANES)),
            pl.BlockSpec((seq, KV_LATENT), lambda b, i: (b, COL_CR // KV_LATENT)),
            pl.BlockSpec((seq, V7X_LANES), lambda b, i: (b, COL_MISC // V7X_LANES)),
            pl.BlockSpec((ATT_HEADS, KV_LATENT, ATT_HEAD_DIM), lambda b, i: (0, 0, 0)),
            pl.BlockSpec((ATT_HEADS, ATT_HEAD_DIM, KV_LATENT), lambda b, i: (0, 0, 0)),
            pl.BlockSpec((1, KV_LATENT), lambda b, i: (0, 0)),
            pl.BlockSpec((2, Q_BLOCK, ATT_HEADS * Q_BLOCK), lambda b, i: (0, 0, 0)),
        ],
        out_specs=pl.BlockSpec((rows, att_dim), lambda b, i: (b * nsteps + i, 0)),
        out_shape=jax.ShapeDtypeStruct((batch * seq, att_dim), BF16),
        scratch_shapes=[
            pltpu.VMEM((SUBS, seq + PAD_ROWS, 2 * KV_LATENT), BF16),
            pltpu.VMEM((PV_ROWS, seq + PAD_ROWS), BF16),
            pltpu.VMEM((seq + PAD_ROWS, IDX_DIM), BF16),
            pltpu.VMEM((2 * KV_LATENT, SUBS * ATT_HEADS * Q_BLOCK), BF16),
            pltpu.VMEM((SUBS, seq, Q_BLOCK), I32),
            pltpu.VMEM((1, Q_BLOCK), I32),
            pltpu.VMEM((seq, PAIR_LANES), F32),
            pltpu.VMEM((seq, PAIR_LANES), F32),
            pltpu.VMEM((seq, PAIR_LANES), BF16),
            pltpu.VMEM((seq, PAIR_LANES), BF16),
            pltpu.VMEM((KV_LATENT, SUBS * ATT_HEADS * Q_BLOCK), BF16),
            pltpu.VMEM((att_dim, rows), F32),
        ],
        compiler_params=pltpu.CompilerParams(
            dimension_semantics=("arbitrary", "arbitrary"), vmem_limit_bytes=VMEM_LIMIT),
        name="dsa_attn",
    )(proj, proj, proj, proj, proj, wukT, wuv, kv_norm, tab)


TAIL = 8
GROUP_LANES = SSM_HEADS_PER_GROUP * SSM_HEAD_DIM


def _expand_matrices():
    e64 = np.zeros((V7X_LANES, SSM_INNER), np.float32)
    e128 = np.zeros((V7X_LANES, SSM_HEADS * SSM_CHUNK), np.float32)
    for h in range(SSM_HEADS):
        for piece in range(3):
            lane = (MISC_DT + h + piece * SSM_HEADS) % V7X_LANES
            e64[lane, h * SSM_HEAD_DIM:(h + 1) * SSM_HEAD_DIM] = 1.0
            e128[lane, h * SSM_CHUNK:(h + 1) * SSM_CHUNK] = 1.0
    return jnp.asarray(e64, BF16), jnp.asarray(e128, BF16)


def _split3(v):
    hi = v.astype(BF16).astype(F32)
    r1 = v - hi
    mid = r1.astype(BF16).astype(F32)
    lo = r1 - mid
    packed = hi + pltpu.roll(mid, SSM_HEADS, axis=1) + pltpu.roll(lo, 2 * SSM_HEADS, axis=1)
    return packed.astype(BF16)


def _conv_silu(u_ref, tail_s, w_ref, b_ref):
    n = SSM_CHUNK
    u = u_ref[...]
    ext = jnp.concatenate([tail_s[...], u], axis=0)
    out = b_ref[...] + w_ref[SSM_CONV - 1:SSM_CONV, :] * u
    for k in range(1, SSM_CONV):
        out = out + w_ref[SSM_CONV - 1 - k:SSM_CONV - k, :] * pltpu.roll(ext, k, axis=0)[TAIL:, :]
    tail_s[...] = u[n - TAIL:n, :]
    return out * _sigmoid(out)


def _ssd_kernel(z_ref, xs_ref, bm_ref, cm_ref, misc_ref, cwx_ref, cwb_ref, cwc_ref, cbx_ref, cbb_ref,
                cbc_ref, dtb_ref, alog_ref, dsk_ref, gn_ref, e64_ref, e128_ref, o_ref,
                xtail_s, btail_s, ctail_s, state_s):
    n = SSM_CHUNK

    @pl.when(pl.program_id(1) == 0)
    def _():
        state_s[...] = jnp.zeros_like(state_s)
        xtail_s[...] = jnp.zeros_like(xtail_s)
        btail_s[...] = jnp.zeros_like(btail_s)
        ctail_s[...] = jnp.zeros_like(ctail_s)

    xs = _conv_silu(xs_ref, xtail_s, cwx_ref, cbx_ref)
    bm = _conv_silu(bm_ref, btail_s, cwb_ref, cbb_ref)
    cm = _conv_silu(cm_ref, ctail_s, cwc_ref, cbc_ref)

    lane = lax.broadcasted_iota(I32, (n, V7X_LANES), 1)
    row = lax.broadcasted_iota(I32, (n, V7X_LANES), 0)
    on_dt = (lane >= MISC_DT) & (lane < MISC_DT + SSM_HEADS)

    v = misc_ref[...] + dtb_ref[...]
    dt = jnp.maximum(v, 0.0) + jnp.log1p(jnp.exp(-jnp.abs(v)))
    dt = jnp.where(on_dt, dt, 0.0)
    a_neg = -jnp.exp(alog_ref[...])
    acum = dt * a_neg
    k = 1
    while k < n:
        acum = acum + jnp.where(row >= k, pltpu.roll(acum, k, axis=0), 0.0)
        k *= 2

    e64 = e64_ref[...]
    dt_e = jnp.dot(_split3(dt), e64, preferred_element_type=F32)
    acum_p = _split3(acum)
    acum_e = jnp.dot(acum_p, e64, preferred_element_type=F32)
    acum_cb = jnp.dot(acum_p, e128_ref[...], preferred_element_type=F32)
    acum_t = acum.T

    last = acum_e[n - 1:n, :]
    xdt = xs * dt_e
    xds_b = (xdt * jnp.exp(last - acum_e)).astype(BF16)
    chunk_decay = jnp.exp(last)
    ea_e = jnp.exp(acum_e)

    tri = row >= lane
    first_head = lane < SSM_HEAD_DIM
    y_groups = []
    for g in range(SSM_GROUPS):
        gl = slice(g * GROUP_LANES, (g + 1) * GROUP_LANES)
        bmg = bm[:, g * SSM_STATE:(g + 1) * SSM_STATE]
        cmg_b = cm[:, g * SSM_STATE:(g + 1) * SSM_STATE].astype(BF16)
        cb = lax.dot_general(cmg_b, bmg.astype(BF16), (((1,), (1,)), ((), ())),
                             preferred_element_type=F32)
        y_pairs = []
        for pr in range(SSM_HEADS_PER_GROUP // 2):
            h0 = g * SSM_HEADS_PER_GROUP + 2 * pr
            mats = []
            for h in (h0, h0 + 1):
                seg = acum_cb[:, h * n:(h + 1) * n] - acum_t[MISC_DT + h:MISC_DT + h + 1, :]
                mats.append((cb * jnp.exp(jnp.where(tri, seg, NEG_INF))).astype(BF16))
            xp = xdt[:, h0 * SSM_HEAD_DIM:(h0 + 2) * SSM_HEAD_DIM]
            rhs = jnp.concatenate([jnp.where(first_head, xp, 0.0), jnp.where(first_head, 0.0, xp)],
                                  axis=0).astype(BF16)
            y_pairs.append(jnp.dot(jnp.concatenate(mats, axis=1), rhs, preferred_element_type=F32))
        prev = state_s[g]
        y_off = jnp.dot(cmg_b, prev.astype(BF16), preferred_element_type=F32) * ea_e[:, gl]
        state_s[g] = prev * chunk_decay[:, gl] + jnp.dot(bmg.T.astype(BF16), xds_b[:, gl],
                                                         preferred_element_type=F32)
        y_groups.append(jnp.concatenate(y_pairs, axis=1) + y_off)

    z = z_ref[...]
    gz = z * _sigmoid(z)
    outs = []
    for g in range(SSM_GROUPS):
        gl = slice(g * GROUP_LANES, (g + 1) * GROUP_LANES)
        u = (y_groups[g] + dsk_ref[:, gl] * xs[:, gl]) * gz[:, gl]
        outs.append(u * lax.rsqrt(jnp.mean(u * u, axis=-1, keepdims=True) + EPS) * gn_ref[:, gl])
    o_ref[...] = jnp.concatenate(outs, axis=1).astype(BF16)


def _ssd(proj, conv_w, conv_b, dt_bias, a_log, d_skip, ssm_norm, batch, seq):
    nch = seq // SSM_CHUNK
    n = SSM_CHUNK
    cwx, cwb, cwc = (conv_w[:, :SSM_INNER], conv_w[:, SSM_INNER:SSM_INNER + SSM_BC],
                     conv_w[:, SSM_INNER + SSM_BC:])
    cbx, cbb, cbc = (conv_b[None, :SSM_INNER], conv_b[None, SSM_INNER:SSM_INNER + SSM_BC],
                     conv_b[None, SSM_INNER + SSM_BC:])
    pad_l, pad_r = MISC_DT, V7X_LANES - MISC_DT - SSM_HEADS
    dtb = jnp.pad(dt_bias, (pad_l, pad_r))[None, :]
    alog = jnp.pad(a_log, (pad_l, pad_r))[None, :]
    dsk = jnp.repeat(d_skip, SSM_HEAD_DIM)[None, :]
    gn = ssm_norm[None, :]
    e64, e128 = _expand_matrices()
    full = lambda shape: pl.BlockSpec(shape, lambda b, c: (0,) * len(shape))
    rows = lambda b, c: b * nch + c
    return pl.pallas_call(
        _ssd_kernel,
        grid=(batch, nch),
        in_specs=[
            pl.BlockSpec((n, SSM_INNER), lambda b, c: (rows(b, c), COL_Z // SSM_INNER)),
            pl.BlockSpec((n, SSM_INNER), lambda b, c: (rows(b, c), COL_XS // SSM_INNER)),
            pl.BlockSpec((n, SSM_BC), lambda b, c: (rows(b, c), COL_BM // SSM_BC)),
            pl.BlockSpec((n, SSM_BC), lambda b, c: (rows(b, c), COL_CM // SSM_BC)),
            pl.BlockSpec((n, V7X_LANES), lambda b, c: (rows(b, c), COL_MISC // V7X_LANES)),
            full((SSM_CONV, SSM_INNER)), full((SSM_CONV, SSM_BC)), full((SSM_CONV, SSM_BC)),
            full((1, SSM_INNER)), full((1, SSM_BC)), full((1, SSM_BC)),
            full((1, V7X_LANES)), full((1, V7X_LANES)),
            full((1, SSM_INNER)), full((1, SSM_INNER)),
            full((V7X_LANES, SSM_INNER)), full((V7X_LANES, SSM_HEADS * n)),
        ],
        out_specs=pl.BlockSpec((n, SSM_INNER), lambda b, c: (rows(b, c), 0)),
        out_shape=jax.ShapeDtypeStruct((batch * seq, SSM_INNER), BF16),
        scratch_shapes=[
            pltpu.VMEM((TAIL, SSM_INNER), F32),
            pltpu.VMEM((TAIL, SSM_BC), F32),
            pltpu.VMEM((TAIL, SSM_BC), F32),
            pltpu.VMEM((SSM_GROUPS, SSM_STATE, GROUP_LANES), F32),
        ],
        compiler_params=pltpu.CompilerParams(
            dimension_semantics=("arbitrary", "arbitrary"), vmem_limit_bytes=VMEM_LIMIT),
        name="ssd",
    )(proj, proj, proj, proj, proj, cwx, cwb, cwc, cbx, cbb, cbc, dtb, alog, dsk, gn, e64, e128)


MERGE_TM = 512


def _merge_kernel(att_ref, yn_ref, ga_ref, gs_ref, x_ref, wao_ref, wso_ref, wo_ref, gf_ref,
                  x1_ref, h2_ref):
    y_att = jnp.dot(att_ref[...], wao_ref[...], preferred_element_type=F32)
    y_ssm = jnp.dot(yn_ref[...], wso_ref[...], preferred_element_type=F32)
    merged = _sigmoid(ga_ref[...]) * y_att + _sigmoid(gs_ref[...]) * y_ssm
    x1 = x_ref[...] + jnp.dot(merged.astype(BF16), wo_ref[...], preferred_element_type=F32)
    x1_ref[...] = x1
    h2_ref[...] = _rms(x1, gf_ref[...]).astype(BF16)


def _merge(att, yn, proj, x2d, wao, wso, wo, norm_ffn):
    t = x2d.shape[0]
    tm = MERGE_TM
    full = lambda shape: pl.BlockSpec(shape, lambda i: (0,) * len(shape))
    return pl.pallas_call(
        _merge_kernel,
        grid=(t // tm,),
        in_specs=[
            pl.BlockSpec((tm, D_MODEL), lambda i: (i, 0)),
            pl.BlockSpec((tm, SSM_INNER), lambda i: (i, 0)),
            pl.BlockSpec((tm, D_MODEL), lambda i: (i, COL_GA // D_MODEL)),
            pl.BlockSpec((tm, D_MODEL), lambda i: (i, COL_GS // D_MODEL)),
            pl.BlockSpec((tm, D_MODEL), lambda i: (i, 0)),
            full((D_MODEL, D_MODEL)), full((SSM_INNER, D_MODEL)), full((D_MODEL, D_MODEL)),
            full((1, D_MODEL)),
        ],
        out_specs=[pl.BlockSpec((tm, D_MODEL), lambda i: (i, 0)),
                   pl.BlockSpec((tm, D_MODEL), lambda i: (i, 0))],
        out_shape=[jax.ShapeDtypeStruct((t, D_MODEL), F32),
                   jax.ShapeDtypeStruct((t, D_MODEL), BF16)],
        compiler_params=pltpu.CompilerParams(
            dimension_semantics=("arbitrary",), vmem_limit_bytes=VMEM_LIMIT),
        name="merge",
    )(att, yn, proj, proj, x2d, wao, wso, wo, norm_ffn)


FFN_TM = 512
FFN_HALO = 16
FFN_FC = V7X_MXU_WIDTH


def _ffn_kernel(h_ref, halo_ref, x1_ref, wup_ref, cw_ref, cb_ref, wdn_ref, gf_ref, o_ref,
                acc_s, *, tiles_per_seq):
    keep = jnp.where(pl.program_id(0) % tiles_per_seq == 0, 0.0, 1.0)
    hcat = jnp.concatenate([halo_ref[...], h_ref[...]], axis=0)
    n_chunks = FFN_DIM // FFN_FC

    def up(c, base):
        return jnp.dot(hcat, wup_ref[:, base + c * FFN_FC:base + (c + 1) * FFN_FC],
                       preferred_element_type=F32)

    def conv(u, c, base):
        cols = slice(base + c * FFN_FC, base + (c + 1) * FFN_FC)
        u = jnp.concatenate([u[0:FFN_HALO, :] * keep, u[FFN_HALO:, :]], axis=0)
        out = cb_ref[:, cols] + cw_ref[FFN_CONV - 1:FFN_CONV, cols] * u[FFN_HALO:, :]
        for k in range(1, FFN_CONV):
            out = out + (cw_ref[FFN_CONV - 1 - k:FFN_CONV - k, cols]
                         * pltpu.roll(u, k, axis=0)[FFN_HALO:, :])
        return out

    ug, uv = up(0, 0), up(0, FFN_DIM)
    for c in range(n_chunks):
        last = c + 1 == n_chunks
        ug_next = None if last else up(c + 1, 0)
        gate = conv(ug, c, 0)
        uv_next = None if last else up(c + 1, FFN_DIM)
        val = conv(uv, c, FFN_DIM)
        ug, uv = ug_next, uv_next
        act = (gate * _sigmoid(gate) * val).astype(BF16)
        contrib = jnp.dot(act, wdn_ref[c * FFN_FC:(c + 1) * FFN_FC, :], preferred_element_type=F32)
        if c == 0:
            acc_s[...] = contrib
        else:
            acc_s[...] += contrib
    o_ref[...] = _rms(x1_ref[...] + acc_s[...], gf_ref[...])


def _ffn(h2, x1, wup, conv_w, conv_b, wdn, norm_final, seq):
    t = h2.shape[0]
    tm = FFN_TM
    halo_blocks = tm // FFN_HALO
    single = pl.Buffered(1)
    full = lambda shape, **kw: pl.BlockSpec(shape, lambda i: (0,) * len(shape), **kw)
    return pl.pallas_call(
        functools.partial(_ffn_kernel, tiles_per_seq=seq // tm),
        grid=(t // tm,),
        in_specs=[
            pl.BlockSpec((tm, D_MODEL), lambda i: (i, 0)),
            pl.BlockSpec((FFN_HALO, D_MODEL), lambda i: (jnp.maximum(i * halo_blocks - 1, 0), 0)),
            pl.BlockSpec((tm, D_MODEL), lambda i: (i, 0)),
            full((D_MODEL, 2 * FFN_DIM), pipeline_mode=single),
            full((FFN_CONV, 2 * FFN_DIM)),
            full((1, 2 * FFN_DIM)),
            full((FFN_DIM, D_MODEL), pipeline_mode=single),
            full((1, D_MODEL)),
        ],
        out_specs=pl.BlockSpec((tm, D_MODEL), lambda i: (i, 0)),
        out_shape=jax.ShapeDtypeStruct((t, D_MODEL), F32),
        scratch_shapes=[pltpu.VMEM((tm, D_MODEL), F32)],
        compiler_params=pltpu.CompilerParams(
            dimension_semantics=("arbitrary",), vmem_limit_bytes=VMEM_LIMIT),
        name="ffn",
    )(h2, h2, x1, wup, conv_w, conv_b, wdn, norm_final)


def kernel(x, rel_bias, norm_mix, w_in, kv_norm, w_uk, w_uv, conv_ssm_w, conv_ssm_b, dt_bias, a_log,
           d_skip, ssm_norm, w_att_out, w_ssm_out, w_out, norm_ffn, w_ffn_up, conv_ffn_w, conv_ffn_b,
           w_ffn_down, norm_final):
    batch, seq, _ = x.shape
    assert norm_mix.shape[0] == 1, "single layer"
    assert seq % (TILES_PER_CLASS * Q_BLOCK) == 0 and min(TOPK_MAX, seq // 4) == TOPK_MAX
    x2d = x.reshape(batch * seq, D_MODEL)

    tab = _bias_tiles(rel_bias)
    proj = _in_proj(x2d, norm_mix[0][None, :], _pack_w_in(w_in[0]))
    att = _dsa_attention(proj, jnp.swapaxes(w_uk[0], 1, 2).astype(BF16), w_uv[0].astype(BF16),
                         kv_norm[0][None, :], tab, batch, seq)
    yn = _ssd(proj, conv_ssm_w[0], conv_ssm_b[0], dt_bias[0], a_log[0], d_skip[0], ssm_norm[0],
              batch, seq)
    x1, h2 = _merge(att, yn, proj, x2d, w_att_out[0].astype(BF16), w_ssm_out[0].astype(BF16),
                    w_out[0].astype(BF16), norm_ffn[0][None, :])
    out = _ffn(h2, x1, w_ffn_up[0].astype(BF16), conv_ffn_w[0], conv_ffn_b[0][None, :],
               w_ffn_down[0].astype(BF16), norm_final[None, :], seq)
    return out.reshape(batch, seq, D_MODEL)
```

```python
import functools
import math

import numpy as np
import jax
import jax.numpy as jnp
from jax import lax
from jax.experimental import pallas as pl
from jax.experimental.pallas import tpu as pltpu

F32 = jnp.float32
BF16 = jnp.bfloat16
I32 = jnp.int32

D_MODEL = 1024
ATT_HEADS = 16
ATT_HEAD_DIM = 64
KV_LATENT = 128
IDX_HEADS = 8
IDX_DIM = 64
TOPK_MAX = 256
Q_BLOCK = 128
REL_BUCKETS = 32
REL_MAX_DIST = 128
SSM_INNER = 2 * D_MODEL
SSM_HEAD_DIM = 64
SSM_HEADS = SSM_INNER // SSM_HEAD_DIM
SSM_GROUPS = 4
SSM_HEADS_PER_GROUP = SSM_HEADS // SSM_GROUPS
SSM_STATE = 128
SSM_CONV = 4
SSM_CHUNK = 128
SSM_BC = SSM_GROUPS * SSM_STATE
FFN_DIM = 2816
FFN_CONV = 3
EPS = 1e-6

V7X_LANES = 128
V7X_MXU_WIDTH = 256
V7X_VMEM_BYTES = 64 * 1024 * 1024
VMEM_LIMIT = 56 * 1024 * 1024

INT_MIN = -(2 ** 31)
NEG_INF = float("-inf")
LOG2E = math.log2(math.e)

COL_Z = 0
COL_XS = COL_Z + SSM_INNER
COL_Q = COL_XS + SSM_INNER
COL_GA = COL_Q + D_MODEL
COL_GS = COL_GA + D_MODEL
COL_QI = COL_GS + D_MODEL
COL_BM = COL_QI + IDX_HEADS * IDX_DIM
COL_CM = COL_BM + SSM_BC
COL_CR = COL_CM + SSM_BC
COL_MISC = COL_CR + KV_LATENT
PROJ_COLS = COL_MISC + V7X_LANES
MISC_KIDX = 0
MISC_WIDX = IDX_DIM
MISC_DT = IDX_DIM + IDX_HEADS


def _pack_moves():
    sizes = (ATT_HEADS * ATT_HEAD_DIM, KV_LATENT, IDX_HEADS * IDX_DIM, IDX_DIM, IDX_HEADS,
             SSM_INNER, SSM_INNER, SSM_BC, SSM_BC, SSM_HEADS, D_MODEL, D_MODEL)
    src = np.cumsum((0,) + sizes)
    dst = (COL_Q, COL_CR, COL_QI, COL_MISC + MISC_KIDX, COL_MISC + MISC_WIDX, COL_Z, COL_XS, COL_BM,
           COL_CM, COL_MISC + MISC_DT, COL_GA, COL_GS)
    return [(int(s), int(d), int(n)) for s, d, n in zip(src[:-1], dst, sizes)]


PACK_ROWS = 128


def _pack_kernel(w_ref, o_ref):
    misc = []
    for s, d, n in _pack_moves():
        if d >= COL_MISC:
            assert d == COL_MISC + sum(p.shape[1] for p in misc)
            misc.append(w_ref[:, s:s + n])
        else:
            o_ref[:, d:d + n] = w_ref[:, s:s + n].astype(BF16)
    used = sum(p.shape[1] for p in misc)
    misc.append(jnp.zeros((PACK_ROWS, V7X_LANES - used), F32))
    o_ref[:, COL_MISC:PROJ_COLS] = jnp.concatenate(misc, axis=1).astype(BF16)


def _pack_w_in(w):
    _, k, cols = w.shape
    return pl.pallas_call(
        _pack_kernel,
        grid=(k // PACK_ROWS,),
        in_specs=[pl.BlockSpec((None, PACK_ROWS, cols), lambda i: (0, i, 0))],
        out_specs=pl.BlockSpec((PACK_ROWS, PROJ_COLS), lambda i: (i, 0)),
        out_shape=jax.ShapeDtypeStruct((k, PROJ_COLS), BF16),
        name="pack_w_in",
    )(w)


def _rms(x, gain):
    return x * lax.rsqrt(jnp.mean(x * x, axis=-1, keepdims=True) + EPS) * gain


def _sigmoid(x):
    return 0.5 + 0.5 * jnp.tanh(0.5 * x)


def _silu(x):
    h = 0.5 * x
    return h + h * jnp.tanh(h)


def _bias_tiles_kernel(rb_ref, o_ref):
    h = pl.program_id(0)
    far = rb_ref[REL_BUCKETS - 1, h]
    key = lax.broadcasted_iota(I32, (Q_BLOCK, Q_BLOCK), 0)
    qry = lax.broadcasted_iota(I32, (Q_BLOCK, Q_BLOCK), 1)
    max_exact = REL_BUCKETS // 2
    for delta in range(2):
        n = jnp.maximum(delta * Q_BLOCK + qry - key, 0)
        nf = jnp.maximum(n, 1).astype(F32)
        large = max_exact + (jnp.log(nf / max_exact) / math.log(REL_MAX_DIST / max_exact)
                             * (REL_BUCKETS - max_exact)).astype(I32)
        large = jnp.minimum(large, REL_BUCKETS - 1)
        bucket = jnp.where(n < max_exact, n, large)
        val = jnp.zeros((Q_BLOCK, Q_BLOCK), F32)
        for b in range(REL_BUCKETS):
            val = jnp.where(bucket == b, rb_ref[b, h] - far, val)
        o_ref[delta] = val * LOG2E


def _bias_tiles(rel_bias):
    return pl.pallas_call(
        _bias_tiles_kernel,
        grid=(ATT_HEADS,),
        in_specs=[pl.BlockSpec(memory_space=pltpu.SMEM)],
        out_specs=pl.BlockSpec((2, Q_BLOCK, Q_BLOCK), lambda h: (0, 0, h)),
        out_shape=jax.ShapeDtypeStruct((2, Q_BLOCK, ATT_HEADS * Q_BLOCK), F32),
        name="bias_tiles",
    )(rel_bias)


INPROJ_TM = 256


def _inproj_kernel(x_ref, g_ref, w_ref, o_ref):
    h = _rms(x_ref[...], g_ref[...]).astype(BF16)
    o_ref[...] = jnp.dot(h, w_ref[...], preferred_element_type=F32)


def _in_proj(x2d, gain, w_packed):
    t = x2d.shape[0]
    return pl.pallas_call(
        _inproj_kernel,
        grid=(t // INPROJ_TM,),
        in_specs=[
            pl.BlockSpec((INPROJ_TM, D_MODEL), lambda i: (i, 0)),
            pl.BlockSpec((1, D_MODEL), lambda i: (0, 0)),
            pl.BlockSpec((D_MODEL, PROJ_COLS), lambda i: (0, 0), pipeline_mode=pl.Buffered(1)),
        ],
        out_specs=pl.BlockSpec((INPROJ_TM, PROJ_COLS), lambda i: (i, 0)),
        out_shape=jax.ShapeDtypeStruct((t, PROJ_COLS), F32),
        compiler_params=pltpu.CompilerParams(
            dimension_semantics=("arbitrary",), vmem_limit_bytes=VMEM_LIMIT),
        name="in_proj",
    )(x2d, gain, w_packed)


TILES_PER_CLASS = 2
PAD_TILES = TILES_PER_CLASS - 1
PAD_ROWS = PAD_TILES * Q_BLOCK
ROW_CHUNK = 256
COUNT_ROWS = 64
PAIR_LANES = 2 * Q_BLOCK
assert PAIR_LANES == V7X_MXU_WIDTH and ROW_CHUNK == TILES_PER_CLASS * Q_BLOCK
ONES_ROWS = 16
PV_ROWS = KV_LATENT + ONES_ROWS
NEG_BIG = -1e30


SUBS = TILES_PER_CLASS


def _aligned(x, m):
    return x if isinstance(x, int) else pl.multiple_of(x, m)


def _attn_step(k, qiT, wT, tab_ref, cm_s, ckvT_s, kidx_s, qT_s, key_s, j_s, l0_s, l1_s, p0_s, p1_s,
               ot_s):
    n_tiles = TILES_PER_CLASS * (k + 1)
    nkp = n_tiles * Q_BLOCK
    n_chunks = nkp // ROW_CHUNK
    n_sel = float(TOPK_MAX)
    lane = lax.broadcasted_iota(I32, (ROW_CHUNK, Q_BLOCK), 1)
    row_in_chunk = lax.broadcasted_iota(I32, (ROW_CHUNK, Q_BLOCK), 0)

    for sub in range(SUBS):
        start = sub * Q_BLOCK
        key0 = (sub - PAD_TILES) * Q_BLOCK
        q_pos = (SUBS * k + sub) * Q_BLOCK + lane
        ql = slice(sub * Q_BLOCK, (sub + 1) * Q_BLOCK)
        for rc in range(n_chunks):
            rows = slice(rc * ROW_CHUNK, (rc + 1) * ROW_CHUNK)
            kc = kidx_s[start + rc * ROW_CHUNK:start + (rc + 1) * ROW_CHUNK, :]
            acc = jnp.zeros((ROW_CHUNK, Q_BLOCK), F32)
            for hp in range(IDX_HEADS // 2):
                rhs = jnp.concatenate(
                    [qiT[(2 * hp) * IDX_DIM:(2 * hp + 1) * IDX_DIM, ql],
                     qiT[(2 * hp + 1) * IDX_DIM:(2 * hp + 2) * IDX_DIM, ql]], axis=1)
                lg = jnp.dot(kc, rhs, preferred_element_type=F32)
                w0 = wT[MISC_WIDX + 2 * hp:MISC_WIDX + 2 * hp + 1, ql]
                w1 = wT[MISC_WIDX + 2 * hp + 1:MISC_WIDX + 2 * hp + 2, ql]
                acc = (acc + jnp.maximum(lg[:, :Q_BLOCK], 0.0) * w0
                       + jnp.maximum(lg[:, Q_BLOCK:], 0.0) * w1)
            bits = pltpu.bitcast(acc, I32)
            bits = jnp.where(bits == INT_MIN, 0, bits)
            key = jnp.where(bits < 0, bits ^ 0x7FFFFFFF, bits)
            key_pos = key0 + rc * ROW_CHUNK + row_in_chunk
            if key0 + rc * ROW_CHUNK < 0:
                key = jnp.where(key_pos >= 0, key, INT_MIN)
            if rc == n_chunks - 1:
                key = jnp.where(key_pos <= q_pos, key, INT_MIN)
            key_s[sub, rows, :] = key

    def count_ones(ones):
        part = jnp.sum(ones.reshape(nkp // COUNT_ROWS, COUNT_ROWS, Q_BLOCK), axis=0)
        return jnp.sum(part, axis=0, keepdims=True)

    def count(pred):
        return count_ones(jnp.where(pred, 1.0, 0.0))

    def value_step(bit, thrs):
        flip = jnp.left_shift(jnp.int32(1), 31 - bit)
        cands = [thr ^ flip for thr in thrs]
        hits = [count(key_s[sub, 0:nkp, :] >= cands[sub]) >= n_sel for sub in range(SUBS)]
        return tuple(jnp.where(hits[sub], cands[sub], thrs[sub]) for sub in range(SUBS))

    thrs = lax.fori_loop(0, 32, value_step,
                         tuple(jnp.full((1, Q_BLOCK), INT_MIN, I32) for _ in range(SUBS)))

    row_in_tile = lax.broadcasted_iota(I32, (Q_BLOCK, Q_BLOCK), 0)
    for sub in range(SUBS):
        thr = thrs[sub]
        keys = key_s[sub, 0:nkp, :]
        need = n_sel - count(keys > thr)
        split = count(keys == thr) > need

        j_s[...] = jnp.full((1, Q_BLOCK), 2 * nkp, I32)

        @pl.when(jnp.max(jnp.where(split, 1.0, 0.0)) > 0.5)
        def _(sub=sub, thr=thr, need=need):
            row = lax.broadcasted_iota(I32, (nkp, Q_BLOCK), 0)
            nbits = (2 * nkp - 1).bit_length()

            def index_step(bit, jmax):
                cand = jmax | jnp.left_shift(jnp.int32(1), nbits - 1 - bit)
                f = count_ones(jnp.where(key_s[sub, 0:nkp, :] == thr,
                                         jnp.where(row < cand, 1.0, 0.0), 0.0))
                return jnp.where(f <= need, cand, jmax)

            j_s[...] = lax.fori_loop(0, nbits, index_step, jnp.zeros((1, Q_BLOCK), I32))

        jmax = j_s[...]
        for t in range(n_tiles):
            kt = key_s[sub, t * Q_BLOCK:(t + 1) * Q_BLOCK, :]
            row = t * Q_BLOCK + row_in_tile
            add = jnp.where(kt > thr, 0.0,
                            jnp.where(kt == thr, jnp.where(row < jmax, 0.0, NEG_BIG), NEG_BIG))
            add = jnp.where(kt == INT_MIN, NEG_BIG, add)
            cm_s[sub, (sub + t) * Q_BLOCK:(sub + t + 1) * Q_BLOCK,
                 KV_LATENT:2 * KV_LATENT] = add.astype(BF16)

    tiles_per_chunk = ROW_CHUNK // Q_BLOCK
    n_pairs = ATT_HEADS // 2
    n_units = SUBS * n_pairs
    pair_bits = n_pairs.bit_length() - 1
    assert n_pairs == 1 << pair_bits

    def unit(logits=None, exps=None, values=None):
        if logits is not None:
            u_a, lbuf_a = logits
            sub_a = u_a >> pair_bits
            start_a = sub_a * Q_BLOCK
            off_t = _aligned((u_a & (n_pairs - 1)) * PAIR_LANES, PAIR_LANES)
            rhs = qT_s[:, pl.ds(_aligned(u_a * PAIR_LANES, PAIR_LANES), PAIR_LANES)]
            m8 = jnp.full((8, PAIR_LANES), NEG_INF, F32)
        if exps is not None:
            lbuf_b, pbuf_b, m_b = exps
        if values is not None:
            u_c, pbuf_c = values
            start_c = (u_c >> pair_bits) * Q_BLOCK
            acc = None
        for rc in range(n_chunks):
            rows = slice(rc * ROW_CHUNK, (rc + 1) * ROW_CHUNK)
            if logits is not None:
                win = pl.ds(_aligned(start_a + rc * ROW_CHUNK, Q_BLOCK), ROW_CHUNK)
                lc = jnp.dot(cm_s[sub_a, win, :], rhs, preferred_element_type=F32)
                for tt in range(tiles_per_chunk):
                    t = rc * tiles_per_chunk + tt
                    lt = lc[tt * Q_BLOCK:(tt + 1) * Q_BLOCK, :]
                    if t >= n_tiles - 2:
                        lt = lt + tab_ref[n_tiles - 1 - t, :, pl.ds(off_t, PAIR_LANES)]
                    lbuf_a[t * Q_BLOCK:(t + 1) * Q_BLOCK, :] = lt
                    m8 = jnp.maximum(m8, jnp.max(lt.reshape(Q_BLOCK // 8, 8, PAIR_LANES), axis=0))
            if exps is not None:
                pbuf_b[rows, :] = jnp.exp2(lbuf_b[rows, :] - m_b).astype(BF16)
            if values is not None:
                win = pl.ds(_aligned(start_c + rc * ROW_CHUNK, Q_BLOCK), ROW_CHUNK)
                part = jnp.dot(ckvT_s[:, win], pbuf_c[rows, :], preferred_element_type=F32)
                acc = part if acc is None else acc + part
        if values is not None:
            ot_s[:, pl.ds(_aligned(u_c * PAIR_LANES, PAIR_LANES), PAIR_LANES)] = (
                acc[0:KV_LATENT, :] / acc[KV_LATENT:KV_LATENT + 1, :]).astype(BF16)
        if logits is not None:
            return jnp.max(m8, axis=0, keepdims=True)

    def two_units(jj, m_odd):
        u = 2 * jj
        m_even = unit(logits=(u + 2, l0_s), exps=(l1_s, p1_s, m_odd), values=(u, p0_s))
        return unit(logits=(u + 3, l1_s), exps=(l0_s, p0_s, m_even), values=(u + 1, p1_s))

    m_even = unit(logits=(0, l0_s))
    m_odd = unit(logits=(1, l1_s), exps=(l0_s, p0_s, m_even))
    m_odd = lax.fori_loop(0, n_units // 2 - 1, two_units, m_odd)
    unit(exps=(l1_s, p1_s, m_odd), values=(n_units - 2, p0_s))
    unit(values=(n_units - 1, p1_s))


def _attn_kernel(q_ref, qi_ref, mq_ref, cr_ref, mk_ref, wukT_ref, wuv_ref, kvn_ref, tab_ref, o_ref,
                 cm_s, ckvT_s, kidx_s, qT_s, key_s, j_s, l0_s, l1_s, p0_s, p1_s, ot_s, attT_s):
    step = pl.program_id(1)
    seq = cr_ref.shape[0]

    @pl.when(step == 0)
    def _():
        c = _rms(cr_ref[...], kvn_ref[...])
        for sub in range(SUBS):
            cm_s[sub, 0:PAD_ROWS, :] = jnp.zeros((PAD_ROWS, 2 * KV_LATENT), BF16)
            cm_s[sub, PAD_ROWS:, 0:KV_LATENT] = c.astype(BF16)
        ckvT_s[0:KV_LATENT, 0:PAD_ROWS] = jnp.zeros((KV_LATENT, PAD_ROWS), BF16)
        ckvT_s[0:KV_LATENT, PAD_ROWS:] = c.T.astype(BF16)
        ckvT_s[KV_LATENT:, :] = jnp.ones((ONES_ROWS, seq + PAD_ROWS), BF16)
        kidx_s[0:PAD_ROWS, :] = jnp.zeros((PAD_ROWS, IDX_DIM), BF16)
        kidx_s[PAD_ROWS:, :] = mk_ref[:, MISC_KIDX:MISC_KIDX + IDX_DIM].astype(BF16)
        eye = jnp.where(lax.broadcasted_iota(I32, (Q_BLOCK, Q_BLOCK), 0)
                        == lax.broadcasted_iota(I32, (Q_BLOCK, Q_BLOCK), 1), 1.0, 0.0).astype(BF16)
        for blk in range(SUBS * ATT_HEADS):
            qT_s[KV_LATENT:, blk * Q_BLOCK:(blk + 1) * Q_BLOCK] = eye

    scale = ATT_HEAD_DIM ** -0.5 * LOG2E
    qb = q_ref[...].astype(BF16)
    for sub in range(SUBS):
        for h in range(ATT_HEADS):
            qh = qb[sub * Q_BLOCK:(sub + 1) * Q_BLOCK, h * ATT_HEAD_DIM:(h + 1) * ATT_HEAD_DIM]
            qt = lax.dot_general(wukT_ref[h], qh, (((1,), (1,)), ((), ())),
                                 preferred_element_type=F32)
            blk = sub * ATT_HEADS + h
            qT_s[0:KV_LATENT, blk * Q_BLOCK:(blk + 1) * Q_BLOCK] = (qt * scale).astype(BF16)

    qiT = qi_ref[...].T.astype(BF16)
    wT = mq_ref[...].T

    for k in range(seq // Q_BLOCK // SUBS):
        @pl.when(step == k)
        def _(k=k):
            _attn_step(k, qiT, wT, tab_ref, cm_s, ckvT_s, kidx_s, qT_s, key_s, j_s, l0_s, l1_s,
                       p0_s, p1_s, ot_s)

    for sub in range(SUBS):
        for h in range(ATT_HEADS):
            blk = sub * ATT_HEADS + h
            attT_s[h * ATT_HEAD_DIM:(h + 1) * ATT_HEAD_DIM, sub * Q_BLOCK:(sub + 1) * Q_BLOCK] = (
                jnp.dot(wuv_ref[h], ot_s[:, blk * Q_BLOCK:(blk + 1) * Q_BLOCK],
                        preferred_element_type=F32))
    o_ref[...] = attT_s[...].T.astype(BF16)


def _dsa_attention(proj, wukT, wuv, kv_norm, tab, batch, seq):
    rows = SUBS * Q_BLOCK
    nsteps = seq // rows
    att_dim = ATT_HEADS * ATT_HEAD_DIM
    return pl.pallas_call(
        _attn_kernel,
        grid=(batch, nsteps),
        in_specs=[
            pl.BlockSpec((rows, att_dim), lambda b, i: (b * nsteps + i, COL_Q // att_dim)),
            pl.BlockSpec((rows, IDX_HEADS * IDX_DIM),
                         lambda b, i: (b * nsteps + i, COL_QI // (IDX_HEADS * IDX_DIM))),
            pl.BlockSpec((rows, V7X_LANES), lambda b, i: (b * nsteps + i, COL_MISC // V7X_LANES)),
            pl.BlockSpec((seq, KV_LATENT), lambda b, i: (b, COL_CR // KV_LATENT)),
            pl.BlockSpec((seq, V7X_LANES), lambda b, i: (b, COL_MISC // V7X_LANES)),
            pl.BlockSpec((ATT_HEADS, KV_LATENT, ATT_HEAD_DIM), lambda b, i: (0, 0, 0)),
            pl.BlockSpec((ATT_HEADS, ATT_HEAD_DIM, KV_LATENT), lambda b, i: (0, 0, 0)),
            pl.BlockSpec((1, KV_LATENT), lambda b, i: (0, 0)),
            pl.BlockSpec((2, Q_BLOCK, ATT_HEADS * Q_BLOCK), lambda b, i: (0, 0, 0)),
        ],
        out_specs=pl.BlockSpec((rows, att_dim), lambda b, i: (b * nsteps + i, 0)),
        out_shape=jax.ShapeDtypeStruct((batch * seq, att_dim), BF16),
        scratch_shapes=[
            pltpu.VMEM((SUBS, seq + PAD_ROWS, 2 * KV_LATENT), BF16),
            pltpu.VMEM((PV_ROWS, seq + PAD_ROWS), BF16),
            pltpu.VMEM((seq + PAD_ROWS, IDX_DIM), BF16),
            pltpu.VMEM((2 * KV_LATENT, SUBS * ATT_HEADS * Q_BLOCK), BF16),
            pltpu.VMEM((SUBS, seq, Q_BLOCK), I32),
            pltpu.VMEM((1, Q_BLOCK), I32),
            pltpu.VMEM((seq, PAIR_LANES), F32),
            pltpu.VMEM((seq, PAIR_LANES), F32),
            pltpu.VMEM((seq, PAIR_LANES), BF16),
            pltpu.VMEM((seq, PAIR_LANES), BF16),
            pltpu.VMEM((KV_LATENT, SUBS * ATT_HEADS * Q_BLOCK), BF16),
            pltpu.VMEM((att_dim, rows), F32),
        ],
        compiler_params=pltpu.CompilerParams(
            dimension_semantics=("arbitrary", "arbitrary"), vmem_limit_bytes=VMEM_LIMIT),
        name="dsa_attn",
    )(proj, proj, proj, proj, proj, wukT, wuv, kv_norm, tab)


TAIL = 8
GROUP_LANES = SSM_HEADS_PER_GROUP * SSM_HEAD_DIM


def _expand_matrices():
    e64 = np.zeros((V7X_LANES, SSM_INNER), np.float32)
    e128 = np.zeros((V7X_LANES, SSM_HEADS * SSM_CHUNK), np.float32)
    for h in range(SSM_HEADS):
        for piece in range(3):
            lane = (MISC_DT + h + piece * SSM_HEADS) % V7X_LANES
            e64[lane, h * SSM_HEAD_DIM:(h + 1) * SSM_HEAD_DIM] = 1.0
            e128[lane, h * SSM_CHUNK:(h + 1) * SSM_CHUNK] = 1.0
    return jnp.asarray(e64, BF16), jnp.asarray(e128, BF16)


def _split3(v):
    hi = v.astype(BF16).astype(F32)
    r1 = v - hi
    mid = r1.astype(BF16).astype(F32)
    lo = r1 - mid
    packed = hi + pltpu.roll(mid, SSM_HEADS, axis=1) + pltpu.roll(lo, 2 * SSM_HEADS, axis=1)
    return packed.astype(BF16)


def _conv_silu(u_ref, tail_s, w_ref, b_ref):
    n = SSM_CHUNK
    u = u_ref[...]
    ext = jnp.concatenate([tail_s[...], u], axis=0)
    out = b_ref[...] + w_ref[SSM_CONV - 1:SSM_CONV, :] * u
    for k in range(1, SSM_CONV):
        out = out + w_ref[SSM_CONV - 1 - k:SSM_CONV - k, :] * pltpu.roll(ext, k, axis=0)[TAIL:, :]
    tail_s[...] = u[n - TAIL:n, :]
    return _silu(out)


def _ssd_kernel(z_ref, xs_ref, bm_ref, cm_ref, misc_ref, cwx_ref, cwb_ref, cwc_ref, cbx_ref, cbb_ref,
                cbc_ref, dtb_ref, alog_ref, dsk_ref, gn_ref, e64_ref, e128_ref, o_ref,
                xtail_s, btail_s, ctail_s, state_s):
    n = SSM_CHUNK

    @pl.when(pl.program_id(1) == 0)
    def _():
        state_s[...] = jnp.zeros_like(state_s)
        xtail_s[...] = jnp.zeros_like(xtail_s)
        btail_s[...] = jnp.zeros_like(btail_s)
        ctail_s[...] = jnp.zeros_like(ctail_s)

    xs = _conv_silu(xs_ref, xtail_s, cwx_ref, cbx_ref)
    bm = _conv_silu(bm_ref, btail_s, cwb_ref, cbb_ref)
    cm = _conv_silu(cm_ref, ctail_s, cwc_ref, cbc_ref)

    lane = lax.broadcasted_iota(I32, (n, V7X_LANES), 1)
    row = lax.broadcasted_iota(I32, (n, V7X_LANES), 0)
    on_dt = (lane >= MISC_DT) & (lane < MISC_DT + SSM_HEADS)

    v = misc_ref[...] + dtb_ref[...]
    dt = jnp.maximum(v, 0.0) + jnp.log1p(jnp.exp(-jnp.abs(v)))
    dt = jnp.where(on_dt, dt, 0.0)
    a_neg = -jnp.exp(alog_ref[...])
    acum = dt * a_neg
    k = 1
    while k < n:
        acum = acum + jnp.where(row >= k, pltpu.roll(acum, k, axis=0), 0.0)
        k *= 2

    e64 = e64_ref[...]
    dt_e = jnp.dot(_split3(dt), e64, preferred_element_type=F32)
    acum_p = _split3(acum)
    acum_e = jnp.dot(acum_p, e64, preferred_element_type=F32)
    acum_cb = jnp.dot(acum_p, e128_ref[...], preferred_element_type=F32)
    acum_t = acum.T

    last = acum_e[n - 1:n, :]
    xdt = xs * dt_e
    xds_b = (xdt * jnp.exp(last - acum_e)).astype(BF16)
    chunk_decay = jnp.exp(last)
    ea_e = jnp.exp(acum_e)

    tri = row >= lane
    first_head = lane < SSM_HEAD_DIM
    y_groups = []
    for g in range(SSM_GROUPS):
        gl = slice(g * GROUP_LANES, (g + 1) * GROUP_LANES)
        bmg = bm[:, g * SSM_STATE:(g + 1) * SSM_STATE]
        cmg_b = cm[:, g * SSM_STATE:(g + 1) * SSM_STATE].astype(BF16)
        cb = lax.dot_general(cmg_b, bmg.astype(BF16), (((1,), (1,)), ((), ())),
                             preferred_element_type=F32)
        y_pairs = []
        for pr in range(SSM_HEADS_PER_GROUP // 2):
            h0 = g * SSM_HEADS_PER_GROUP + 2 * pr
            mats = []
            for h in (h0, h0 + 1):
                seg = acum_cb[:, h * n:(h + 1) * n] - acum_t[MISC_DT + h:MISC_DT + h + 1, :]
                mats.append((cb * jnp.exp(jnp.where(tri, seg, NEG_INF))).astype(BF16))
            xp = xdt[:, h0 * SSM_HEAD_DIM:(h0 + 2) * SSM_HEAD_DIM]
            rhs = jnp.concatenate([jnp.where(first_head, xp, 0.0), jnp.where(first_head, 0.0, xp)],
                                  axis=0).astype(BF16)
            y_pairs.append(jnp.dot(jnp.concatenate(mats, axis=1), rhs, preferred_element_type=F32))
        prev = state_s[g]
        y_off = jnp.dot(cmg_b, prev.astype(BF16), preferred_element_type=F32) * ea_e[:, gl]
        state_s[g] = prev * chunk_decay[:, gl] + jnp.dot(bmg.T.astype(BF16), xds_b[:, gl],
                                                         preferred_element_type=F32)
        y_groups.append(jnp.concatenate(y_pairs, axis=1) + y_off)

    z = z_ref[...]
    gz = _silu(z)
    outs = []
    for g in range(SSM_GROUPS):
        gl = slice(g * GROUP_LANES, (g + 1) * GROUP_LANES)
        u = (y_groups[g] + dsk_ref[:, gl] * xs[:, gl]) * gz[:, gl]
        outs.append(u * lax.rsqrt(jnp.mean(u * u, axis=-1, keepdims=True) + EPS) * gn_ref[:, gl])
    o_ref[...] = jnp.concatenate(outs, axis=1).astype(BF16)


def _ssd(proj, conv_w, conv_b, dt_bias, a_log, d_skip, ssm_norm, batch, seq):
    nch = seq // SSM_CHUNK
    n = SSM_CHUNK
    cwx, cwb, cwc = (conv_w[:, :SSM_INNER], conv_w[:, SSM_INNER:SSM_INNER + SSM_BC],
                     conv_w[:, SSM_INNER + SSM_BC:])
    cbx, cbb, cbc = (conv_b[None, :SSM_INNER], conv_b[None, SSM_INNER:SSM_INNER + SSM_BC],
                     conv_b[None, SSM_INNER + SSM_BC:])
    pad_l, pad_r = MISC_DT, V7X_LANES - MISC_DT - SSM_HEADS
    dtb = jnp.pad(dt_bias, (pad_l, pad_r))[None, :]
    alog = jnp.pad(a_log, (pad_l, pad_r))[None, :]
    dsk = jnp.repeat(d_skip, SSM_HEAD_DIM)[None, :]
    gn = ssm_norm[None, :]
    e64, e128 = _expand_matrices()
    full = lambda shape: pl.BlockSpec(shape, lambda b, c: (0,) * len(shape))
    rows = lambda b, c: b * nch + c
    return pl.pallas_call(
        _ssd_kernel,
        grid=(batch, nch),
        in_specs=[
            pl.BlockSpec((n, SSM_INNER), lambda b, c: (rows(b, c), COL_Z // SSM_INNER)),
            pl.BlockSpec((n, SSM_INNER), lambda b, c: (rows(b, c), COL_XS // SSM_INNER)),
            pl.BlockSpec((n, SSM_BC), lambda b, c: (rows(b, c), COL_BM // SSM_BC)),
            pl.BlockSpec((n, SSM_BC), lambda b, c: (rows(b, c), COL_CM // SSM_BC)),
            pl.BlockSpec((n, V7X_LANES), lambda b, c: (rows(b, c), COL_MISC // V7X_LANES)),
            full((SSM_CONV, SSM_INNER)), full((SSM_CONV, SSM_BC)), full((SSM_CONV, SSM_BC)),
            full((1, SSM_INNER)), full((1, SSM_BC)), full((1, SSM_BC)),
            full((1, V7X_LANES)), full((1, V7X_LANES)),
            full((1, SSM_INNER)), full((1, SSM_INNER)),
            full((V7X_LANES, SSM_INNER)), full((V7X_LANES, SSM_HEADS * n)),
        ],
        out_specs=pl.BlockSpec((n, SSM_INNER), lambda b, c: (rows(b, c), 0)),
        out_shape=jax.ShapeDtypeStruct((batch * seq, SSM_INNER), BF16),
        scratch_shapes=[
            pltpu.VMEM((TAIL, SSM_INNER), F32),
            pltpu.VMEM((TAIL, SSM_BC), F32),
            pltpu.VMEM((TAIL, SSM_BC), F32),
            pltpu.VMEM((SSM_GROUPS, SSM_STATE, GROUP_LANES), F32),
        ],
        compiler_params=pltpu.CompilerParams(
            dimension_semantics=("arbitrary", "arbitrary"), vmem_limit_bytes=VMEM_LIMIT),
        name="ssd",
    )(proj, proj, proj, proj, proj, cwx, cwb, cwc, cbx, cbb, cbc, dtb, alog, dsk, gn, e64, e128)


MERGE_TM = 512


def _merge_kernel(att_ref, yn_ref, ga_ref, gs_ref, x_ref, wao_ref, wso_ref, wo_ref, gf_ref,
                  x1_ref, h2_ref):
    y_att = jnp.dot(att_ref[...], wao_ref[...], preferred_element_type=F32)
    y_ssm = jnp.dot(yn_ref[...], wso_ref[...], preferred_element_type=F32)
    merged = _sigmoid(ga_ref[...]) * y_att + _sigmoid(gs_ref[...]) * y_ssm
    x1 = x_ref[...] + jnp.dot(merged.astype(BF16), wo_ref[...], preferred_element_type=F32)
    x1_ref[...] = x1
    h2_ref[...] = _rms(x1, gf_ref[...]).astype(BF16)


def _merge(att, yn, proj, x2d, wao, wso, wo, norm_ffn):
    t = x2d.shape[0]
    tm = MERGE_TM
    full = lambda shape: pl.BlockSpec(shape, lambda i: (0,) * len(shape))
    return pl.pallas_call(
        _merge_kernel,
        grid=(t // tm,),
        in_specs=[
            pl.BlockSpec((tm, D_MODEL), lambda i: (i, 0)),
            pl.BlockSpec((tm, SSM_INNER), lambda i: (i, 0)),
            pl.BlockSpec((tm, D_MODEL), lambda i: (i, COL_GA // D_MODEL)),
            pl.BlockSpec((tm, D_MODEL), lambda i: (i, COL_GS // D_MODEL)),
            pl.BlockSpec((tm, D_MODEL), lambda i: (i, 0)),
            full((D_MODEL, D_MODEL)), full((SSM_INNER, D_MODEL)), full((D_MODEL, D_MODEL)),
            full((1, D_MODEL)),
        ],
        out_specs=[pl.BlockSpec((tm, D_MODEL), lambda i: (i, 0)),
                   pl.BlockSpec((tm, D_MODEL), lambda i: (i, 0))],
        out_shape=[jax.ShapeDtypeStruct((t, D_MODEL), F32),
                   jax.ShapeDtypeStruct((t, D_MODEL), BF16)],
        compiler_params=pltpu.CompilerParams(
            dimension_semantics=("arbitrary",), vmem_limit_bytes=VMEM_LIMIT),
        name="merge",
    )(att, yn, proj, proj, x2d, wao, wso, wo, norm_ffn)


FFN_TM = 512
FFN_HALO = 16
FFN_FC = V7X_MXU_WIDTH


def _ffn_kernel(h_ref, halo_ref, x1_ref, wup_ref, cw_ref, cb_ref, wdn_ref, gf_ref, o_ref,
                acc_s, *, tiles_per_seq):
    keep = jnp.where(pl.program_id(0) % tiles_per_seq == 0, 0.0, 1.0)
    hcat = jnp.concatenate([halo_ref[...], h_ref[...]], axis=0)
    n_chunks = FFN_DIM // FFN_FC

    def up(c, base):
        return jnp.dot(hcat, wup_ref[:, base + c * FFN_FC:base + (c + 1) * FFN_FC],
                       preferred_element_type=F32)

    def conv(u, c, base):
        cols = slice(base + c * FFN_FC, base + (c + 1) * FFN_FC)
        u = jnp.concatenate([u[0:FFN_HALO, :] * keep, u[FFN_HALO:, :]], axis=0)
        out = cb_ref[:, cols] + cw_ref[FFN_CONV - 1:FFN_CONV, cols] * u[FFN_HALO:, :]
        for k in range(1, FFN_CONV):
            out = out + (cw_ref[FFN_CONV - 1 - k:FFN_CONV - k, cols]
                         * pltpu.roll(u, k, axis=0)[FFN_HALO:, :])
        return out

    ug, uv = up(0, 0), up(0, FFN_DIM)
    for c in range(n_chunks):
        last = c + 1 == n_chunks
        ug_next = None if last else up(c + 1, 0)
        gate = conv(ug, c, 0)
        uv_next = None if last else up(c + 1, FFN_DIM)
        val = conv(uv, c, FFN_DIM)
        ug, uv = ug_next, uv_next
        act = (_silu(gate) * val).astype(BF16)
        contrib = jnp.dot(act, wdn_ref[c * FFN_FC:(c + 1) * FFN_FC, :], preferred_element_type=F32)
        if c == 0:
            acc_s[...] = contrib
        else:
            acc_s[...] += contrib
    o_ref[...] = _rms(x1_ref[...] + acc_s[...], gf_ref[...])


def _ffn(h2, x1, wup, conv_w, conv_b, wdn, norm_final, seq):
    t = h2.shape[0]
    tm = FFN_TM
    halo_blocks = tm // FFN_HALO
    single = pl.Buffered(1)
    full = lambda shape, **kw: pl.BlockSpec(shape, lambda i: (0,) * len(shape), **kw)
    return pl.pallas_call(
        functools.partial(_ffn_kernel, tiles_per_seq=seq // tm),
        grid=(t // tm,),
        in_specs=[
            pl.BlockSpec((tm, D_MODEL), lambda i: (i, 0)),
            pl.BlockSpec((FFN_HALO, D_MODEL), lambda i: (jnp.maximum(i * halo_blocks - 1, 0), 0)),
            pl.BlockSpec((tm, D_MODEL), lambda i: (i, 0)),
            full((D_MODEL, 2 * FFN_DIM), pipeline_mode=single),
            full((FFN_CONV, 2 * FFN_DIM)),
            full((1, 2 * FFN_DIM)),
            full((FFN_DIM, D_MODEL), pipeline_mode=single),
            full((1, D_MODEL)),
        ],
        out_specs=pl.BlockSpec((tm, D_MODEL), lambda i: (i, 0)),
        out_shape=jax.ShapeDtypeStruct((t, D_MODEL), F32),
        scratch_shapes=[pltpu.VMEM((tm, D_MODEL), F32)],
        compiler_params=pltpu.CompilerParams(
            dimension_semantics=("arbitrary",), vmem_limit_bytes=VMEM_LIMIT),
        name="ffn",
    )(h2, h2, x1, wup, conv_w, conv_b, wdn, norm_final)


def kernel(x, rel_bias, norm_mix, w_in, kv_norm, w_uk, w_uv, conv_ssm_w, conv_ssm_b, dt_bias, a_log,
           d_skip, ssm_norm, w_att_out, w_ssm_out, w_out, norm_ffn, w_ffn_up, conv_ffn_w, conv_ffn_b,
           w_ffn_down, norm_final):
    batch, seq, _ = x.shape
    assert norm_mix.shape[0] == 1, "single layer"
    assert seq % (TILES_PER_CLASS * Q_BLOCK) == 0 and min(TOPK_MAX, seq // 4) == TOPK_MAX
    x2d = x.reshape(batch * seq, D_MODEL)

    tab = _bias_tiles(rel_bias)
    proj = _in_proj(x2d, norm_mix[0][None, :], _pack_w_in(w_in))
    att = _dsa_attention(proj, jnp.swapaxes(w_uk[0], 1, 2).astype(BF16), w_uv[0].astype(BF16),
                         kv_norm[0][None, :], tab, batch, seq)
    yn = _ssd(proj, conv_ssm_w[0], conv_ssm_b[0], dt_bias[0], a_log[0], d_skip[0], ssm_norm[0],
              batch, seq)
    x1, h2 = _merge(att, yn, proj, x2d, w_att_out[0].astype(BF16), w_ssm_out[0].astype(BF16),
                    w_out[0].astype(BF16), norm_ffn[0][None, :])
    out = _ffn(h2, x1, w_ffn_up[0].astype(BF16), conv_ffn_w[0], conv_ffn_b[0][None, :],
               w_ffn_down[0].astype(BF16), norm_final[None, :], seq)
    return out.reshape(batch, seq, D_MODEL)
```

```python
import functools
import math

import numpy as np
import jax
import jax.numpy as jnp
from jax import lax
from jax.experimental import pallas as pl
from jax.experimental.pallas import tpu as pltpu

F32 = jnp.float32
BF16 = jnp.bfloat16
I32 = jnp.int32

D_MODEL = 1024
ATT_HEADS = 16
ATT_HEAD_DIM = 64
KV_LATENT = 128
IDX_HEADS = 8
IDX_DIM = 64
TOPK_MAX = 256
Q_BLOCK = 128
REL_BUCKETS = 32
REL_MAX_DIST = 128
SSM_INNER = 2 * D_MODEL
SSM_HEAD_DIM = 64
SSM_HEADS = SSM_INNER // SSM_HEAD_DIM
SSM_GROUPS = 4
SSM_HEADS_PER_GROUP = SSM_HEADS // SSM_GROUPS
SSM_STATE = 128
SSM_CONV = 4
SSM_CHUNK = 128
SSM_BC = SSM_GROUPS * SSM_STATE
FFN_DIM = 2816
FFN_CONV = 3
EPS = 1e-6

V7X_LANES = 128
V7X_MXU_WIDTH = 256
V7X_VMEM_BYTES = 64 * 1024 * 1024
VMEM_LIMIT = 56 * 1024 * 1024

INT_MIN = -(2 ** 31)
NEG_INF = float("-inf")
LOG2E = math.log2(math.e)

COL_Z = 0
COL_XS = COL_Z + SSM_INNER
COL_Q = COL_XS + SSM_INNER
COL_GA = COL_Q + D_MODEL
COL_GS = COL_GA + D_MODEL
COL_QI = COL_GS + D_MODEL
COL_BM = COL_QI + IDX_HEADS * IDX_DIM
COL_CM = COL_BM + SSM_BC
COL_CR = COL_CM + SSM_BC
COL_MISC = COL_CR + KV_LATENT
PROJ_COLS = COL_MISC + V7X_LANES
MISC_KIDX = 0
MISC_WIDX = IDX_DIM
MISC_DT = IDX_DIM + IDX_HEADS


def _pack_moves():
    sizes = (ATT_HEADS * ATT_HEAD_DIM, KV_LATENT, IDX_HEADS * IDX_DIM, IDX_DIM, IDX_HEADS,
             SSM_INNER, SSM_INNER, SSM_BC, SSM_BC, SSM_HEADS, D_MODEL, D_MODEL)
    src = np.cumsum((0,) + sizes)
    dst = (COL_Q, COL_CR, COL_QI, COL_MISC + MISC_KIDX, COL_MISC + MISC_WIDX, COL_Z, COL_XS, COL_BM,
           COL_CM, COL_MISC + MISC_DT, COL_GA, COL_GS)
    return [(int(s), int(d), int(n)) for s, d, n in zip(src[:-1], dst, sizes)]


PACK_ROWS = 128


def _pack_kernel(wt_ref, o_ref):
    misc = []
    for s, d, n in _pack_moves():
        if d >= COL_MISC:
            assert d == COL_MISC + sum(p.shape[0] for p in misc)
            misc.append(wt_ref[s:s + n, :])
        else:
            o_ref[:, d:d + n] = wt_ref[s:s + n, :].T.astype(BF16)
    used = sum(p.shape[0] for p in misc)
    misc.append(jnp.zeros((V7X_LANES - used, PACK_ROWS), F32))
    o_ref[:, COL_MISC:PROJ_COLS] = jnp.concatenate(misc, axis=0).T.astype(BF16)


def _pack_w_in(w):
    _, k, cols = w.shape
    wt = jnp.swapaxes(w, 1, 2)
    assert all(s % 8 == 0 for s, _, _ in _pack_moves())
    return pl.pallas_call(
        _pack_kernel,
        grid=(k // PACK_ROWS,),
        in_specs=[pl.BlockSpec((None, cols, PACK_ROWS), lambda i: (0, 0, i))],
        out_specs=pl.BlockSpec((PACK_ROWS, PROJ_COLS), lambda i: (i, 0)),
        out_shape=jax.ShapeDtypeStruct((k, PROJ_COLS), BF16),
        name="pack_w_in",
    )(wt)


def _rms(x, gain):
    return x * lax.rsqrt(jnp.mean(x * x, axis=-1, keepdims=True) + EPS) * gain


def _sigmoid(x):
    return 0.5 + 0.5 * jnp.tanh(0.5 * x)


def _silu(x):
    h = 0.5 * x
    return h + h * jnp.tanh(h)


def _bias_tiles_kernel(rb_ref, o_ref):
    h = pl.program_id(0)
    far = rb_ref[REL_BUCKETS - 1, h]
    key = lax.broadcasted_iota(I32, (Q_BLOCK, Q_BLOCK), 0)
    qry = lax.broadcasted_iota(I32, (Q_BLOCK, Q_BLOCK), 1)
    max_exact = REL_BUCKETS // 2
    for delta in range(2):
        n = jnp.maximum(delta * Q_BLOCK + qry - key, 0)
        nf = jnp.maximum(n, 1).astype(F32)
        large = max_exact + (jnp.log(nf / max_exact) / math.log(REL_MAX_DIST / max_exact)
                             * (REL_BUCKETS - max_exact)).astype(I32)
        large = jnp.minimum(large, REL_BUCKETS - 1)
        bucket = jnp.where(n < max_exact, n, large)
        val = jnp.zeros((Q_BLOCK, Q_BLOCK), F32)
        for b in range(REL_BUCKETS):
            val = jnp.where(bucket == b, rb_ref[b, h] - far, val)
        o_ref[delta] = val * LOG2E


def _bias_tiles(rel_bias):
    return pl.pallas_call(
        _bias_tiles_kernel,
        grid=(ATT_HEADS,),
        in_specs=[pl.BlockSpec(memory_space=pltpu.SMEM)],
        out_specs=pl.BlockSpec((2, Q_BLOCK, Q_BLOCK), lambda h: (0, 0, h)),
        out_shape=jax.ShapeDtypeStruct((2, Q_BLOCK, ATT_HEADS * Q_BLOCK), F32),
        name="bias_tiles",
    )(rel_bias)


INPROJ_TM = 256


def _inproj_kernel(x_ref, g_ref, w_ref, o_ref):
    h = _rms(x_ref[...], g_ref[...]).astype(BF16)
    o_ref[...] = jnp.dot(h, w_ref[...], preferred_element_type=F32)


def _in_proj(x2d, gain, w_packed):
    t = x2d.shape[0]
    return pl.pallas_call(
        _inproj_kernel,
        grid=(t // INPROJ_TM,),
        in_specs=[
            pl.BlockSpec((INPROJ_TM, D_MODEL), lambda i: (i, 0)),
            pl.BlockSpec((1, D_MODEL), lambda i: (0, 0)),
            pl.BlockSpec((D_MODEL, PROJ_COLS), lambda i: (0, 0), pipeline_mode=pl.Buffered(1)),
        ],
        out_specs=pl.BlockSpec((INPROJ_TM, PROJ_COLS), lambda i: (i, 0)),
        out_shape=jax.ShapeDtypeStruct((t, PROJ_COLS), F32),
        compiler_params=pltpu.CompilerParams(
            dimension_semantics=("arbitrary",), vmem_limit_bytes=VMEM_LIMIT),
        name="in_proj",
    )(x2d, gain, w_packed)


TILES_PER_CLASS = 2
PAD_TILES = TILES_PER_CLASS - 1
PAD_ROWS = PAD_TILES * Q_BLOCK
ROW_CHUNK = 256
COUNT_ROWS = 64
PAIR_LANES = 2 * Q_BLOCK
assert PAIR_LANES == V7X_MXU_WIDTH and (TILES_PER_CLASS * Q_BLOCK) % ROW_CHUNK == 0
ONES_ROWS = 16
PV_ROWS = KV_LATENT + ONES_ROWS
NEG_BIG = -1e30


SUBS = TILES_PER_CLASS


def _aligned(x, m):
    return x if isinstance(x, int) else pl.multiple_of(x, m)


def _attn_step(k, qiT, wT, tab_ref, cm_s, ckvT_s, kidx_s, qT_s, key_s, j_s, l0_s, l1_s, p0_s, p1_s,
               ot_s):
    n_tiles = TILES_PER_CLASS * (k + 1)
    nkp = n_tiles * Q_BLOCK
    n_chunks = nkp // ROW_CHUNK
    n_sel = float(TOPK_MAX)
    lane = lax.broadcasted_iota(I32, (ROW_CHUNK, Q_BLOCK), 1)
    row_in_chunk = lax.broadcasted_iota(I32, (ROW_CHUNK, Q_BLOCK), 0)

    for sub in range(SUBS):
        start = sub * Q_BLOCK
        key0 = (sub - PAD_TILES) * Q_BLOCK
        q_pos = (SUBS * k + sub) * Q_BLOCK + lane
        ql = slice(sub * Q_BLOCK, (sub + 1) * Q_BLOCK)
        for rc in range(n_chunks):
            rows = slice(rc * ROW_CHUNK, (rc + 1) * ROW_CHUNK)
            kc = kidx_s[start + rc * ROW_CHUNK:start + (rc + 1) * ROW_CHUNK, :]
            acc = jnp.zeros((ROW_CHUNK, Q_BLOCK), F32)
            for hp in range(IDX_HEADS // 2):
                rhs = jnp.concatenate(
                    [qiT[(2 * hp) * IDX_DIM:(2 * hp + 1) * IDX_DIM, ql],
                     qiT[(2 * hp + 1) * IDX_DIM:(2 * hp + 2) * IDX_DIM, ql]], axis=1)
                lg = jnp.dot(kc, rhs, preferred_element_type=F32)
                w0 = wT[MISC_WIDX + 2 * hp:MISC_WIDX + 2 * hp + 1, ql]
                w1 = wT[MISC_WIDX + 2 * hp + 1:MISC_WIDX + 2 * hp + 2, ql]
                acc = (acc + jnp.maximum(lg[:, :Q_BLOCK], 0.0) * w0
                       + jnp.maximum(lg[:, Q_BLOCK:], 0.0) * w1)
            bits = pltpu.bitcast(acc, I32)
            bits = jnp.where(bits == INT_MIN, 0, bits)
            key = jnp.where(bits < 0, bits ^ 0x7FFFFFFF, bits)
            key_pos = key0 + rc * ROW_CHUNK + row_in_chunk
            if key0 + rc * ROW_CHUNK < 0:
                key = jnp.where(key_pos >= 0, key, INT_MIN)
            if rc == n_chunks - 1:
                key = jnp.where(key_pos <= q_pos, key, INT_MIN)
            key_s[sub, rows, :] = key

    def count_ones(ones):
        part = jnp.sum(ones.reshape(nkp // COUNT_ROWS, COUNT_ROWS, Q_BLOCK), axis=0)
        return jnp.sum(part, axis=0, keepdims=True)

    def count(pred):
        return count_ones(jnp.where(pred, 1.0, 0.0))

    def value_step(bit, thrs):
        flip = jnp.left_shift(jnp.int32(1), 31 - bit)
        cands = [thr ^ flip for thr in thrs]
        hits = [count(key_s[sub, 0:nkp, :] >= cands[sub]) >= n_sel for sub in range(SUBS)]
        return tuple(jnp.where(hits[sub], cands[sub], thrs[sub]) for sub in range(SUBS))

    thrs = lax.fori_loop(0, 32, value_step,
                         tuple(jnp.full((1, Q_BLOCK), INT_MIN, I32) for _ in range(SUBS)))

    row_in_tile = lax.broadcasted_iota(I32, (Q_BLOCK, Q_BLOCK), 0)
    for sub in range(SUBS):
        thr = thrs[sub]
        keys = key_s[sub, 0:nkp, :]
        need = n_sel - count(keys > thr)
        split = count(keys == thr) > need

        j_s[...] = jnp.full((1, Q_BLOCK), 2 * nkp, I32)

        @pl.when(jnp.max(jnp.where(split, 1.0, 0.0)) > 0.5)
        def _(sub=sub, thr=thr, need=need):
            row = lax.broadcasted_iota(I32, (nkp, Q_BLOCK), 0)
            nbits = (2 * nkp - 1).bit_length()

            def index_step(bit, jmax):
                cand = jmax | jnp.left_shift(jnp.int32(1), nbits - 1 - bit)
                f = count_ones(jnp.where(key_s[sub, 0:nkp, :] == thr,
                                         jnp.where(row < cand, 1.0, 0.0), 0.0))
                return jnp.where(f <= need, cand, jmax)

            j_s[...] = lax.fori_loop(0, nbits, index_step, jnp.zeros((1, Q_BLOCK), I32))

        jmax = j_s[...]
        for t in range(n_tiles):
            kt = key_s[sub, t * Q_BLOCK:(t + 1) * Q_BLOCK, :]
            row = t * Q_BLOCK + row_in_tile
            add = jnp.where(kt > thr, 0.0,
                            jnp.where(kt == thr, jnp.where(row < jmax, 0.0, NEG_BIG), NEG_BIG))
            add = jnp.where(kt == INT_MIN, NEG_BIG, add)
            cm_s[sub, (sub + t) * Q_BLOCK:(sub + t + 1) * Q_BLOCK,
                 KV_LATENT:2 * KV_LATENT] = add.astype(BF16)

    tiles_per_chunk = ROW_CHUNK // Q_BLOCK
    n_pairs = ATT_HEADS // 2
    n_units = SUBS * n_pairs
    pair_bits = n_pairs.bit_length() - 1
    assert n_pairs == 1 << pair_bits

    def unit(logits=None, exps=None, values=None):
        if logits is not None:
            u_a, lbuf_a = logits
            sub_a = u_a >> pair_bits
            start_a = sub_a * Q_BLOCK
            off_t = _aligned((u_a & (n_pairs - 1)) * PAIR_LANES, PAIR_LANES)
            rhs = qT_s[:, pl.ds(_aligned(u_a * PAIR_LANES, PAIR_LANES), PAIR_LANES)]
            m8 = jnp.full((8, PAIR_LANES), NEG_INF, F32)
        if exps is not None:
            lbuf_b, pbuf_b, m_b = exps
        if values is not None:
            u_c, pbuf_c = values
            start_c = (u_c >> pair_bits) * Q_BLOCK
            acc = None
        for rc in range(n_chunks):
            rows = slice(rc * ROW_CHUNK, (rc + 1) * ROW_CHUNK)
            if logits is not None:
                win = pl.ds(_aligned(start_a + rc * ROW_CHUNK, Q_BLOCK), ROW_CHUNK)
                lc = jnp.dot(cm_s[sub_a, win, :], rhs, preferred_element_type=F32)
                for tt in range(tiles_per_chunk):
                    t = rc * tiles_per_chunk + tt
                    lt = lc[tt * Q_BLOCK:(tt + 1) * Q_BLOCK, :]
                    if t >= n_tiles - 2:
                        lt = lt + tab_ref[n_tiles - 1 - t, :, pl.ds(off_t, PAIR_LANES)]
                    lbuf_a[t * Q_BLOCK:(t + 1) * Q_BLOCK, :] = lt
                    m8 = jnp.maximum(m8, jnp.max(lt.reshape(Q_BLOCK // 8, 8, PAIR_LANES), axis=0))
            if exps is not None:
                pbuf_b[rows, :] = jnp.exp2(lbuf_b[rows, :] - m_b).astype(BF16)
            if values is not None:
                win = pl.ds(_aligned(start_c + rc * ROW_CHUNK, Q_BLOCK), ROW_CHUNK)
                part = jnp.dot(ckvT_s[:, win], pbuf_c[rows, :], preferred_element_type=F32)
                acc = part if acc is None else acc + part
        if values is not None:
            ot_s[:, pl.ds(_aligned(u_c * PAIR_LANES, PAIR_LANES), PAIR_LANES)] = (
                acc[0:KV_LATENT, :] / acc[KV_LATENT:KV_LATENT + 1, :]).astype(BF16)
        if logits is not None:
            return jnp.max(m8, axis=0, keepdims=True)

    def two_units(jj, m_odd):
        u = 2 * jj
        m_even = unit(logits=(u + 2, l0_s), exps=(l1_s, p1_s, m_odd), values=(u, p0_s))
        return unit(logits=(u + 3, l1_s), exps=(l0_s, p0_s, m_even), values=(u + 1, p1_s))

    m_even = unit(logits=(0, l0_s))
    m_odd = unit(logits=(1, l1_s), exps=(l0_s, p0_s, m_even))
    m_odd = lax.fori_loop(0, n_units // 2 - 1, two_units, m_odd)
    unit(exps=(l1_s, p1_s, m_odd), values=(n_units - 2, p0_s))
    unit(values=(n_units - 1, p1_s))


def _attn_kernel(q_ref, qi_ref, mq_ref, cr_ref, mk_ref, wukT_ref, wuv_ref, kvn_ref, tab_ref, o_ref,
                 cm_s, ckvT_s, kidx_s, qT_s, key_s, j_s, l0_s, l1_s, p0_s, p1_s, ot_s, attT_s):
    step = pl.program_id(1)
    seq = cr_ref.shape[0]

    @pl.when(step == 0)
    def _():
        c = _rms(cr_ref[...], kvn_ref[...])
        for sub in range(SUBS):
            cm_s[sub, 0:PAD_ROWS, :] = jnp.zeros((PAD_ROWS, 2 * KV_LATENT), BF16)
            cm_s[sub, PAD_ROWS:, 0:KV_LATENT] = c.astype(BF16)
        ckvT_s[0:KV_LATENT, 0:PAD_ROWS] = jnp.zeros((KV_LATENT, PAD_ROWS), BF16)
        ckvT_s[0:KV_LATENT, PAD_ROWS:] = c.T.astype(BF16)
        ckvT_s[KV_LATENT:, :] = jnp.ones((ONES_ROWS, seq + PAD_ROWS), BF16)
        kidx_s[0:PAD_ROWS, :] = jnp.zeros((PAD_ROWS, IDX_DIM), BF16)
        kidx_s[PAD_ROWS:, :] = mk_ref[:, MISC_KIDX:MISC_KIDX + IDX_DIM].astype(BF16)
        eye = jnp.where(lax.broadcasted_iota(I32, (Q_BLOCK, Q_BLOCK), 0)
                        == lax.broadcasted_iota(I32, (Q_BLOCK, Q_BLOCK), 1), 1.0, 0.0).astype(BF16)
        for blk in range(SUBS * ATT_HEADS):
            qT_s[KV_LATENT:, blk * Q_BLOCK:(blk + 1) * Q_BLOCK] = eye

    scale = ATT_HEAD_DIM ** -0.5 * LOG2E
    qb = q_ref[...].astype(BF16)
    for sub in range(SUBS):
        for h in range(ATT_HEADS):
            qh = qb[sub * Q_BLOCK:(sub + 1) * Q_BLOCK, h * ATT_HEAD_DIM:(h + 1) * ATT_HEAD_DIM]
            qt = lax.dot_general(wukT_ref[h], qh, (((1,), (1,)), ((), ())),
                                 preferred_element_type=F32)
            blk = sub * ATT_HEADS + h
            qT_s[0:KV_LATENT, blk * Q_BLOCK:(blk + 1) * Q_BLOCK] = (qt * scale).astype(BF16)

    qiT = qi_ref[...].T.astype(BF16)
    wT = mq_ref[...].T

    for k in range(seq // Q_BLOCK // SUBS):
        @pl.when(step == k)
        def _(k=k):
            _attn_step(k, qiT, wT, tab_ref, cm_s, ckvT_s, kidx_s, qT_s, key_s, j_s, l0_s, l1_s,
                       p0_s, p1_s, ot_s)

    for sub in range(SUBS):
        for h in range(ATT_HEADS):
            blk = sub * ATT_HEADS + h
            attT_s[h * ATT_HEAD_DIM:(h + 1) * ATT_HEAD_DIM, sub * Q_BLOCK:(sub + 1) * Q_BLOCK] = (
                jnp.dot(wuv_ref[h], ot_s[:, blk * Q_BLOCK:(blk + 1) * Q_BLOCK],
                        preferred_element_type=F32))
    o_ref[...] = attT_s[...].T.astype(BF16)


def _dsa_attention(proj, wukT, wuv, kv_norm, tab, batch, seq):
    rows = SUBS * Q_BLOCK
    nsteps = seq // rows
    att_dim = ATT_HEADS * ATT_HEAD_DIM
    return pl.pallas_call(
        _attn_kernel,
        grid=(batch, nsteps),
        in_specs=[
            pl.BlockSpec((rows, att_dim), lambda b, i: (b * nsteps + i, COL_Q // att_dim)),
            pl.BlockSpec((rows, IDX_HEADS * IDX_DIM),
                         lambda b, i: (b * nsteps + i, COL_QI // (IDX_HEADS * IDX_DIM))),
            pl.BlockSpec((rows, V7X_LANES), lambda b, i: (b * nsteps + i, COL_MISC // V7X_LANES)),
            pl.BlockSpec((seq, KV_LATENT), lambda b, i: (b, COL_CR // KV_LATENT)),
            pl.BlockSpec((seq, V7X_LANES), lambda b, i: (b, COL_MISC // V7X_LANES)),
            pl.BlockSpec((ATT_HEADS, KV_LATENT, ATT_HEAD_DIM), lambda b, i: (0, 0, 0)),
            pl.BlockSpec((ATT_HEADS, ATT_HEAD_DIM, KV_LATENT), lambda b, i: (0, 0, 0)),
            pl.BlockSpec((1, KV_LATENT), lambda b, i: (0, 0)),
            pl.BlockSpec((2, Q_BLOCK, ATT_HEADS * Q_BLOCK), lambda b, i: (0, 0, 0)),
        ],
        out_specs=pl.BlockSpec((rows, att_dim), lambda b, i: (b * nsteps + i, 0)),
        out_shape=jax.ShapeDtypeStruct((batch * seq, att_dim), BF16),
        scratch_shapes=[
            pltpu.VMEM((SUBS, seq + PAD_ROWS, 2 * KV_LATENT), BF16),
            pltpu.VMEM((PV_ROWS, seq + PAD_ROWS), BF16),
            pltpu.VMEM((seq + PAD_ROWS, IDX_DIM), BF16),
            pltpu.VMEM((2 * KV_LATENT, SUBS * ATT_HEADS * Q_BLOCK), BF16),
            pltpu.VMEM((SUBS, seq, Q_BLOCK), I32),
            pltpu.VMEM((1, Q_BLOCK), I32),
            pltpu.VMEM((seq, PAIR_LANES), F32),
            pltpu.VMEM((seq, PAIR_LANES), F32),
            pltpu.VMEM((seq, PAIR_LANES), BF16),
            pltpu.VMEM((seq, PAIR_LANES), BF16),
            pltpu.VMEM((KV_LATENT, SUBS * ATT_HEADS * Q_BLOCK), BF16),
            pltpu.VMEM((att_dim, rows), F32),
        ],
        compiler_params=pltpu.CompilerParams(
            dimension_semantics=("arbitrary", "arbitrary"), vmem_limit_bytes=VMEM_LIMIT),
        name="dsa_attn",
    )(proj, proj, proj, proj, proj, wukT, wuv, kv_norm, tab)


TAIL = 8
GROUP_LANES = SSM_HEADS_PER_GROUP * SSM_HEAD_DIM


def _expand_matrices():
    e64 = np.zeros((V7X_LANES, SSM_INNER), np.float32)
    e128 = np.zeros((V7X_LANES, SSM_HEADS * SSM_CHUNK), np.float32)
    for h in range(SSM_HEADS):
        for piece in range(3):
            lane = (MISC_DT + h + piece * SSM_HEADS) % V7X_LANES
            e64[lane, h * SSM_HEAD_DIM:(h + 1) * SSM_HEAD_DIM] = 1.0
            e128[lane, h * SSM_CHUNK:(h + 1) * SSM_CHUNK] = 1.0
    return jnp.asarray(e64, BF16), jnp.asarray(e128, BF16)


def _split3(v):
    hi = v.astype(BF16).astype(F32)
    r1 = v - hi
    mid = r1.astype(BF16).astype(F32)
    lo = r1 - mid
    packed = hi + pltpu.roll(mid, SSM_HEADS, axis=1) + pltpu.roll(lo, 2 * SSM_HEADS, axis=1)
    return packed.astype(BF16)


def _conv_silu(u_ref, tail_s, w_ref, b_ref):
    n = SSM_CHUNK
    u = u_ref[...]
    ext = jnp.concatenate([tail_s[...], u], axis=0)
    out = b_ref[...] + w_ref[SSM_CONV - 1:SSM_CONV, :] * u
    for k in range(1, SSM_CONV):
        out = out + w_ref[SSM_CONV - 1 - k:SSM_CONV - k, :] * pltpu.roll(ext, k, axis=0)[TAIL:, :]
    tail_s[...] = u[n - TAIL:n, :]
    return _silu(out)


SSD_CHUNKS_PER_STEP = 2


def _ssd_kernel(z_ref, xs_ref, bm_ref, cm_ref, misc_ref, *rest):
    o_ref, xtail_s, btail_s, ctail_s, state_s = rest[-5:]

    @pl.when(pl.program_id(1) == 0)
    def _():
        state_s[...] = jnp.zeros_like(state_s)
        xtail_s[...] = jnp.zeros_like(xtail_s)
        btail_s[...] = jnp.zeros_like(btail_s)
        ctail_s[...] = jnp.zeros_like(ctail_s)

    for c in range(SSD_CHUNKS_PER_STEP):
        rows = pl.ds(c * SSM_CHUNK, SSM_CHUNK)
        _ssd_chunk(z_ref.at[rows], xs_ref.at[rows], bm_ref.at[rows], cm_ref.at[rows],
                   misc_ref.at[rows], *rest[:-5], o_ref.at[rows], xtail_s, btail_s, ctail_s, state_s)


def _ssd_chunk(z_ref, xs_ref, bm_ref, cm_ref, misc_ref, cwx_ref, cwb_ref, cwc_ref, cbx_ref, cbb_ref,
               cbc_ref, dtb_ref, alog_ref, dsk_ref, gn_ref, e64_ref, e128_ref, o_ref,
               xtail_s, btail_s, ctail_s, state_s):
    n = SSM_CHUNK
    xs = _conv_silu(xs_ref, xtail_s, cwx_ref, cbx_ref)
    bm = _conv_silu(bm_ref, btail_s, cwb_ref, cbb_ref)
    cm = _conv_silu(cm_ref, ctail_s, cwc_ref, cbc_ref)

    lane = lax.broadcasted_iota(I32, (n, V7X_LANES), 1)
    row = lax.broadcasted_iota(I32, (n, V7X_LANES), 0)
    on_dt = (lane >= MISC_DT) & (lane < MISC_DT + SSM_HEADS)

    v = misc_ref[...] + dtb_ref[...]
    dt = jnp.maximum(v, 0.0) + jnp.log1p(jnp.exp(-jnp.abs(v)))
    dt = jnp.where(on_dt, dt, 0.0)
    a_neg = -jnp.exp(alog_ref[...])
    acum = dt * a_neg
    k = 1
    while k < n:
        acum = acum + jnp.where(row >= k, pltpu.roll(acum, k, axis=0), 0.0)
        k *= 2

    e64 = e64_ref[...]
    dt_e = jnp.dot(_split3(dt), e64, preferred_element_type=F32)
    acum_p = _split3(acum)
    acum_e = jnp.dot(acum_p, e64, preferred_element_type=F32)
    acum_cb = jnp.dot(acum_p, e128_ref[...], preferred_element_type=F32)
    acum_t = acum.T

    last = acum_e[n - 1:n, :]
    xdt = xs * dt_e
    xds_b = (xdt * jnp.exp(last - acum_e)).astype(BF16)
    chunk_decay = jnp.exp(last)
    ea_e = jnp.exp(acum_e)

    tri = row >= lane
    first_head = lane < SSM_HEAD_DIM
    y_groups = []
    for g in range(SSM_GROUPS):
        gl = slice(g * GROUP_LANES, (g + 1) * GROUP_LANES)
        bmg = bm[:, g * SSM_STATE:(g + 1) * SSM_STATE]
        cmg_b = cm[:, g * SSM_STATE:(g + 1) * SSM_STATE].astype(BF16)
        cb = lax.dot_general(cmg_b, bmg.astype(BF16), (((1,), (1,)), ((), ())),
                             preferred_element_type=F32)
        y_pairs = []
        for pr in range(SSM_HEADS_PER_GROUP // 2):
            h0 = g * SSM_HEADS_PER_GROUP + 2 * pr
            mats = []
            for h in (h0, h0 + 1):
                seg = acum_cb[:, h * n:(h + 1) * n] - acum_t[MISC_DT + h:MISC_DT + h + 1, :]
                mats.append((cb * jnp.exp(jnp.where(tri, seg, NEG_INF))).astype(BF16))
            xp = xdt[:, h0 * SSM_HEAD_DIM:(h0 + 2) * SSM_HEAD_DIM]
            rhs = jnp.concatenate([jnp.where(first_head, xp, 0.0), jnp.where(first_head, 0.0, xp)],
                                  axis=0).astype(BF16)
            y_pairs.append(jnp.dot(jnp.concatenate(mats, axis=1), rhs, preferred_element_type=F32))
        prev = state_s[g]
        y_off = jnp.dot(cmg_b, prev.astype(BF16), preferred_element_type=F32) * ea_e[:, gl]
        state_s[g] = prev * chunk_decay[:, gl] + jnp.dot(bmg.T.astype(BF16), xds_b[:, gl],
                                                         preferred_element_type=F32)
        y_groups.append(jnp.concatenate(y_pairs, axis=1) + y_off)

    z = z_ref[...]
    gz = _silu(z)
    outs = []
    for g in range(SSM_GROUPS):
        gl = slice(g * GROUP_LANES, (g + 1) * GROUP_LANES)
        u = (y_groups[g] + dsk_ref[:, gl] * xs[:, gl]) * gz[:, gl]
        outs.append(u * lax.rsqrt(jnp.mean(u * u, axis=-1, keepdims=True) + EPS) * gn_ref[:, gl])
    o_ref[...] = jnp.concatenate(outs, axis=1).astype(BF16)


def _ssd(proj, conv_w, conv_b, dt_bias, a_log, d_skip, ssm_norm, batch, seq):
    n = SSM_CHUNK * SSD_CHUNKS_PER_STEP
    nch = seq // n
    cwx, cwb, cwc = (conv_w[:, :SSM_INNER], conv_w[:, SSM_INNER:SSM_INNER + SSM_BC],
                     conv_w[:, SSM_INNER + SSM_BC:])
    cbx, cbb, cbc = (conv_b[None, :SSM_INNER], conv_b[None, SSM_INNER:SSM_INNER + SSM_BC],
                     conv_b[None, SSM_INNER + SSM_BC:])
    pad_l, pad_r = MISC_DT, V7X_LANES - MISC_DT - SSM_HEADS
    dtb = jnp.pad(dt_bias, (pad_l, pad_r))[None, :]
    alog = jnp.pad(a_log, (pad_l, pad_r))[None, :]
    dsk = jnp.repeat(d_skip, SSM_HEAD_DIM)[None, :]
    gn = ssm_norm[None, :]
    e64, e128 = _expand_matrices()
    full = lambda shape: pl.BlockSpec(shape, lambda b, c: (0,) * len(shape))
    rows = lambda b, c: b * nch + c
    return pl.pallas_call(
        _ssd_kernel,
        grid=(batch, nch),
        in_specs=[
            pl.BlockSpec((n, SSM_INNER), lambda b, c: (rows(b, c), COL_Z // SSM_INNER)),
            pl.BlockSpec((n, SSM_INNER), lambda b, c: (rows(b, c), COL_XS // SSM_INNER)),
            pl.BlockSpec((n, SSM_BC), lambda b, c: (rows(b, c), COL_BM // SSM_BC)),
            pl.BlockSpec((n, SSM_BC), lambda b, c: (rows(b, c), COL_CM // SSM_BC)),
            pl.BlockSpec((n, V7X_LANES), lambda b, c: (rows(b, c), COL_MISC // V7X_LANES)),
            full((SSM_CONV, SSM_INNER)), full((SSM_CONV, SSM_BC)), full((SSM_CONV, SSM_BC)),
            full((1, SSM_INNER)), full((1, SSM_BC)), full((1, SSM_BC)),
            full((1, V7X_LANES)), full((1, V7X_LANES)),
            full((1, SSM_INNER)), full((1, SSM_INNER)),
            full((V7X_LANES, SSM_INNER)), full((V7X_LANES, SSM_HEADS * SSM_CHUNK)),
        ],
        out_specs=pl.BlockSpec((n, SSM_INNER), lambda b, c: (rows(b, c), 0)),
        out_shape=jax.ShapeDtypeStruct((batch * seq, SSM_INNER), BF16),
        scratch_shapes=[
            pltpu.VMEM((TAIL, SSM_INNER), F32),
            pltpu.VMEM((TAIL, SSM_BC), F32),
            pltpu.VMEM((TAIL, SSM_BC), F32),
            pltpu.VMEM((SSM_GROUPS, SSM_STATE, GROUP_LANES), F32),
        ],
        compiler_params=pltpu.CompilerParams(
            dimension_semantics=("arbitrary", "arbitrary"), vmem_limit_bytes=VMEM_LIMIT),
        name="ssd",
    )(proj, proj, proj, proj, proj, cwx, cwb, cwc, cbx, cbb, cbc, dtb, alog, dsk, gn, e64, e128)


MERGE_TM = 512


def _merge_kernel(att_ref, yn_ref, ga_ref, gs_ref, x_ref, wao_ref, wso_ref, wo_ref, gf_ref,
                  x1_ref, h2_ref):
    y_att = jnp.dot(att_ref[...], wao_ref[...], preferred_element_type=F32)
    y_ssm = jnp.dot(yn_ref[...], wso_ref[...], preferred_element_type=F32)
    merged = _sigmoid(ga_ref[...]) * y_att + _sigmoid(gs_ref[...]) * y_ssm
    x1 = x_ref[...] + jnp.dot(merged.astype(BF16), wo_ref[...], preferred_element_type=F32)
    x1_ref[...] = x1
    h2_ref[...] = _rms(x1, gf_ref[...]).astype(BF16)


def _merge(att, yn, proj, x2d, wao, wso, wo, norm_ffn):
    t = x2d.shape[0]
    tm = MERGE_TM
    full = lambda shape: pl.BlockSpec(shape, lambda i: (0,) * len(shape))
    return pl.pallas_call(
        _merge_kernel,
        grid=(t // tm,),
        in_specs=[
            pl.BlockSpec((tm, D_MODEL), lambda i: (i, 0)),
            pl.BlockSpec((tm, SSM_INNER), lambda i: (i, 0)),
            pl.BlockSpec((tm, D_MODEL), lambda i: (i, COL_GA // D_MODEL)),
            pl.BlockSpec((tm, D_MODEL), lambda i: (i, COL_GS // D_MODEL)),
            pl.BlockSpec((tm, D_MODEL), lambda i: (i, 0)),
            full((D_MODEL, D_MODEL)), full((SSM_INNER, D_MODEL)), full((D_MODEL, D_MODEL)),
            full((1, D_MODEL)),
        ],
        out_specs=[pl.BlockSpec((tm, D_MODEL), lambda i: (i, 0)),
                   pl.BlockSpec((tm, D_MODEL), lambda i: (i, 0))],
        out_shape=[jax.ShapeDtypeStruct((t, D_MODEL), F32),
                   jax.ShapeDtypeStruct((t, D_MODEL), BF16)],
        compiler_params=pltpu.CompilerParams(
            dimension_semantics=("arbitrary",), vmem_limit_bytes=VMEM_LIMIT),
        name="merge",
    )(att, yn, proj, proj, x2d, wao, wso, wo, norm_ffn)


FFN_TM = 512
FFN_HALO = 16
FFN_FC = V7X_MXU_WIDTH


def _ffn_kernel(h_ref, halo_ref, x1_ref, wup_ref, cw_ref, cb_ref, wdn_ref, gf_ref, o_ref,
                acc_s, *, tiles_per_seq):
    keep = jnp.where(pl.program_id(0) % tiles_per_seq == 0, 0.0, 1.0)
    hcat = jnp.concatenate([halo_ref[...], h_ref[...]], axis=0)
    n_chunks = FFN_DIM // FFN_FC

    def up(c, base):
        return jnp.dot(hcat, wup_ref[:, base + c * FFN_FC:base + (c + 1) * FFN_FC],
                       preferred_element_type=F32)

    def conv(u, c, base):
        cols = slice(base + c * FFN_FC, base + (c + 1) * FFN_FC)
        u = jnp.concatenate([u[0:FFN_HALO, :] * keep, u[FFN_HALO:, :]], axis=0)
        out = cb_ref[:, cols] + cw_ref[FFN_CONV - 1:FFN_CONV, cols] * u[FFN_HALO:, :]
        for k in range(1, FFN_CONV):
            out = out + (cw_ref[FFN_CONV - 1 - k:FFN_CONV - k, cols]
                         * pltpu.roll(u, k, axis=0)[FFN_HALO:, :])
        return out

    ug, uv = up(0, 0), up(0, FFN_DIM)
    for c in range(n_chunks):
        last = c + 1 == n_chunks
        ug_next = None if last else up(c + 1, 0)
        gate = conv(ug, c, 0)
        uv_next = None if last else up(c + 1, FFN_DIM)
        val = conv(uv, c, FFN_DIM)
        ug, uv = ug_next, uv_next
        act = (_silu(gate) * val).astype(BF16)
        contrib = jnp.dot(act, wdn_ref[c * FFN_FC:(c + 1) * FFN_FC, :], preferred_element_type=F32)
        if c == 0:
            acc_s[...] = contrib
        else:
            acc_s[...] += contrib
    o_ref[...] = _rms(x1_ref[...] + acc_s[...], gf_ref[...])


def _ffn(h2, x1, wup, conv_w, conv_b, wdn, norm_final, seq):
    t = h2.shape[0]
    tm = FFN_TM
    halo_blocks = tm // FFN_HALO
    single = pl.Buffered(1)
    full = lambda shape, **kw: pl.BlockSpec(shape, lambda i: (0,) * len(shape), **kw)
    return pl.pallas_call(
        functools.partial(_ffn_kernel, tiles_per_seq=seq // tm),
        grid=(t // tm,),
        in_specs=[
            pl.BlockSpec((tm, D_MODEL), lambda i: (i, 0)),
            pl.BlockSpec((FFN_HALO, D_MODEL), lambda i: (jnp.maximum(i * halo_blocks - 1, 0), 0)),
            pl.BlockSpec((tm, D_MODEL), lambda i: (i, 0)),
            full((D_MODEL, 2 * FFN_DIM), pipeline_mode=single),
            full((FFN_CONV, 2 * FFN_DIM)),
            full((1, 2 * FFN_DIM)),
            full((FFN_DIM, D_MODEL), pipeline_mode=single),
            full((1, D_MODEL)),
        ],
        out_specs=pl.BlockSpec((tm, D_MODEL), lambda i: (i, 0)),
        out_shape=jax.ShapeDtypeStruct((t, D_MODEL), F32),
        scratch_shapes=[pltpu.VMEM((tm, D_MODEL), F32)],
        compiler_params=pltpu.CompilerParams(
            dimension_semantics=("arbitrary",), vmem_limit_bytes=VMEM_LIMIT),
        name="ffn",
    )(h2, h2, x1, wup, conv_w, conv_b, wdn, norm_final)


def kernel(x, rel_bias, norm_mix, w_in, kv_norm, w_uk, w_uv, conv_ssm_w, conv_ssm_b, dt_bias, a_log,
           d_skip, ssm_norm, w_att_out, w_ssm_out, w_out, norm_ffn, w_ffn_up, conv_ffn_w, conv_ffn_b,
           w_ffn_down, norm_final):
    batch, seq, _ = x.shape
    assert norm_mix.shape[0] == 1, "single layer"
    assert seq % (TILES_PER_CLASS * Q_BLOCK) == 0 and min(TOPK_MAX, seq // 4) == TOPK_MAX
    x2d = x.reshape(batch * seq, D_MODEL)

    tab = _bias_tiles(rel_bias)
    proj = _in_proj(x2d, norm_mix[0][None, :], _pack_w_in(w_in))
    att = _dsa_attention(proj, jnp.swapaxes(w_uk[0], 1, 2).astype(BF16), w_uv[0].astype(BF16),
                         kv_norm[0][None, :], tab, batch, seq)
    yn = _ssd(proj, conv_ssm_w[0], conv_ssm_b[0], dt_bias[0], a_log[0], d_skip[0], ssm_norm[0],
              batch, seq)
    x1, h2 = _merge(att, yn, proj, x2d, w_att_out[0].astype(BF16), w_ssm_out[0].astype(BF16),
                    w_out[0].astype(BF16), norm_ffn[0][None, :])
    out = _ffn(h2, x1, w_ffn_up[0].astype(BF16), conv_ffn_w[0], conv_ffn_b[0][None, :],
               w_ffn_down[0].astype(BF16), norm_final[None, :], seq)
    return out.reshape(batch, seq, D_MODEL)
```

```python
import functools
import math

import numpy as np
import jax
import jax.numpy as jnp
from jax import lax
from jax.experimental import pallas as pl
from jax.experimental.pallas import tpu as pltpu

F32 = jnp.float32
BF16 = jnp.bfloat16
I32 = jnp.int32

D_MODEL = 1024
ATT_HEADS = 16
ATT_HEAD_DIM = 64
KV_LATENT = 128
IDX_HEADS = 8
IDX_DIM = 64
TOPK_MAX = 256
Q_BLOCK = 128
REL_BUCKETS = 32
REL_MAX_DIST = 128
SSM_INNER = 2 * D_MODEL
SSM_HEAD_DIM = 64
SSM_HEADS = SSM_INNER // SSM_HEAD_DIM
SSM_GROUPS = 4
SSM_HEADS_PER_GROUP = SSM_HEADS // SSM_GROUPS
SSM_STATE = 128
SSM_CONV = 4
SSM_CHUNK = 128
SSM_BC = SSM_GROUPS * SSM_STATE
FFN_DIM = 2816
FFN_CONV = 3
EPS = 1e-6

V7X_LANES = 128
V7X_MXU_WIDTH = 256
V7X_VMEM_BYTES = 64 * 1024 * 1024
VMEM_LIMIT = 56 * 1024 * 1024

INT_MIN = -(2 ** 31)
NEG_INF = float("-inf")
LOG2E = math.log2(math.e)

COL_Q = 0
COL_GA = COL_Q + D_MODEL
COL_GS = COL_GA + D_MODEL
COL_QI = COL_GS + D_MODEL
COL_CR = COL_QI + IDX_HEADS * IDX_DIM
COL_MISC = COL_CR + KV_LATENT
PROJ_COLS = COL_MISC + V7X_LANES
COL_Z = PROJ_COLS
COL_XS = COL_Z + SSM_INNER
COL_BM = COL_XS + SSM_INNER
COL_CM = COL_BM + SSM_BC
W_COLS = COL_CM + SSM_BC
SSD_COLS = W_COLS - PROJ_COLS
MISC_KIDX = 0
MISC_WIDX = IDX_DIM
MISC_DT = IDX_DIM + IDX_HEADS


def _pack_moves():
    sizes = (ATT_HEADS * ATT_HEAD_DIM, KV_LATENT, IDX_HEADS * IDX_DIM, IDX_DIM, IDX_HEADS,
             SSM_INNER, SSM_INNER, SSM_BC, SSM_BC, SSM_HEADS, D_MODEL, D_MODEL)
    src = np.cumsum((0,) + sizes)
    dst = (COL_Q, COL_CR, COL_QI, COL_MISC + MISC_KIDX, COL_MISC + MISC_WIDX, COL_Z, COL_XS, COL_BM,
           COL_CM, COL_MISC + MISC_DT, COL_GA, COL_GS)
    return [(int(s), int(d), int(n)) for s, d, n in zip(src[:-1], dst, sizes)]


PACK_ROWS = 128


def _pack_kernel(wt_ref, o_ref):
    misc = []
    for s, d, n in _pack_moves():
        if COL_MISC <= d < PROJ_COLS:
            assert d == COL_MISC + sum(p.shape[0] for p in misc)
            misc.append(wt_ref[s:s + n, :])
        else:
            o_ref[:, d:d + n] = wt_ref[s:s + n, :].T.astype(BF16)
    used = sum(p.shape[0] for p in misc)
    misc.append(jnp.zeros((V7X_LANES - used, PACK_ROWS), F32))
    o_ref[:, COL_MISC:PROJ_COLS] = jnp.concatenate(misc, axis=0).T.astype(BF16)


def _pack_w_in(w):
    _, k, cols = w.shape
    wt = jnp.swapaxes(w, 1, 2)
    assert all(s % 8 == 0 for s, _, _ in _pack_moves())
    return pl.pallas_call(
        _pack_kernel,
        grid=(k // PACK_ROWS,),
        in_specs=[pl.BlockSpec((None, cols, PACK_ROWS), lambda i: (0, 0, i))],
        out_specs=pl.BlockSpec((PACK_ROWS, W_COLS), lambda i: (i, 0)),
        out_shape=jax.ShapeDtypeStruct((k, W_COLS), BF16),
        name="pack_w_in",
    )(wt)


def _rms(x, gain):
    return x * lax.rsqrt(jnp.mean(x * x, axis=-1, keepdims=True) + EPS) * gain


def _sigmoid(x):
    return 0.5 + 0.5 * jnp.tanh(0.5 * x)


def _silu(x):
    h = 0.5 * x
    return h + h * jnp.tanh(h)


def _bias_tiles_kernel(rb_ref, o_ref):
    h = pl.program_id(0)
    far = rb_ref[REL_BUCKETS - 1, h]
    key = lax.broadcasted_iota(I32, (Q_BLOCK, Q_BLOCK), 0)
    qry = lax.broadcasted_iota(I32, (Q_BLOCK, Q_BLOCK), 1)
    max_exact = REL_BUCKETS // 2
    for delta in range(2):
        n = jnp.maximum(delta * Q_BLOCK + qry - key, 0)
        nf = jnp.maximum(n, 1).astype(F32)
        large = max_exact + (jnp.log(nf / max_exact) / math.log(REL_MAX_DIST / max_exact)
                             * (REL_BUCKETS - max_exact)).astype(I32)
        large = jnp.minimum(large, REL_BUCKETS - 1)
        bucket = jnp.where(n < max_exact, n, large)
        val = jnp.zeros((Q_BLOCK, Q_BLOCK), F32)
        for b in range(REL_BUCKETS):
            val = jnp.where(bucket == b, rb_ref[b, h] - far, val)
        o_ref[delta] = val * LOG2E


def _bias_tiles(rel_bias):
    return pl.pallas_call(
        _bias_tiles_kernel,
        grid=(ATT_HEADS,),
        in_specs=[pl.BlockSpec(memory_space=pltpu.SMEM)],
        out_specs=pl.BlockSpec((2, Q_BLOCK, Q_BLOCK), lambda h: (0, 0, h)),
        out_shape=jax.ShapeDtypeStruct((2, Q_BLOCK, ATT_HEADS * Q_BLOCK), F32),
        name="bias_tiles",
    )(rel_bias)


TILES_PER_CLASS = 2
PAD_TILES = TILES_PER_CLASS - 1
PAD_ROWS = PAD_TILES * Q_BLOCK
ROW_CHUNK = 256
COUNT_ROWS = 64
PAIR_LANES = 2 * Q_BLOCK
assert PAIR_LANES == V7X_MXU_WIDTH and (TILES_PER_CLASS * Q_BLOCK) % ROW_CHUNK == 0
ONES_ROWS = 16
PV_ROWS = KV_LATENT + ONES_ROWS
NEG_BIG = -1e30


SUBS = TILES_PER_CLASS


def _aligned(x, m):
    return x if isinstance(x, int) else pl.multiple_of(x, m)


def _attn_step(k, qiT, wT, tab_ref, cm_s, ckvT_s, kidx_s, qT_s, key_s, j_s, l0_s, l1_s, p0_s, p1_s,
               ot_s):
    n_tiles = TILES_PER_CLASS * (k + 1)
    nkp = n_tiles * Q_BLOCK
    n_chunks = nkp // ROW_CHUNK
    n_sel = float(TOPK_MAX)
    lane = lax.broadcasted_iota(I32, (ROW_CHUNK, Q_BLOCK), 1)
    row_in_chunk = lax.broadcasted_iota(I32, (ROW_CHUNK, Q_BLOCK), 0)

    for sub in range(SUBS):
        start = sub * Q_BLOCK
        key0 = (sub - PAD_TILES) * Q_BLOCK
        q_pos = (SUBS * k + sub) * Q_BLOCK + lane
        ql = slice(sub * Q_BLOCK, (sub + 1) * Q_BLOCK)
        for rc in range(n_chunks):
            rows = slice(rc * ROW_CHUNK, (rc + 1) * ROW_CHUNK)
            kc = kidx_s[start + rc * ROW_CHUNK:start + (rc + 1) * ROW_CHUNK, :]
            acc = jnp.zeros((ROW_CHUNK, Q_BLOCK), F32)
            for hp in range(IDX_HEADS // 2):
                rhs = jnp.concatenate(
                    [qiT[(2 * hp) * IDX_DIM:(2 * hp + 1) * IDX_DIM, ql],
                     qiT[(2 * hp + 1) * IDX_DIM:(2 * hp + 2) * IDX_DIM, ql]], axis=1)
                lg = jnp.dot(kc, rhs, preferred_element_type=F32)
                w0 = wT[MISC_WIDX + 2 * hp:MISC_WIDX + 2 * hp + 1, ql]
                w1 = wT[MISC_WIDX + 2 * hp + 1:MISC_WIDX + 2 * hp + 2, ql]
                acc = (acc + jnp.maximum(lg[:, :Q_BLOCK], 0.0) * w0
                       + jnp.maximum(lg[:, Q_BLOCK:], 0.0) * w1)
            bits = pltpu.bitcast(acc, I32)
            bits = jnp.where(bits == INT_MIN, 0, bits)
            key = jnp.where(bits < 0, bits ^ 0x7FFFFFFF, bits)
            key_pos = key0 + rc * ROW_CHUNK + row_in_chunk
            if key0 + rc * ROW_CHUNK < 0:
                key = jnp.where(key_pos >= 0, key, INT_MIN)
            if rc == n_chunks - 1:
                key = jnp.where(key_pos <= q_pos, key, INT_MIN)
            key_s[sub, rows, :] = key

    def count_ones(ones):
        part = jnp.sum(ones.reshape(nkp // COUNT_ROWS, COUNT_ROWS, Q_BLOCK), axis=0)
        return jnp.sum(part, axis=0, keepdims=True)

    def count(pred):
        return count_ones(jnp.where(pred, 1.0, 0.0))

    def value_step(bit, thrs):
        flip = jnp.left_shift(jnp.int32(1), 31 - bit)
        cands = [thr ^ flip for thr in thrs]
        hits = [count(key_s[sub, 0:nkp, :] >= cands[sub]) >= n_sel for sub in range(SUBS)]
        return tuple(jnp.where(hits[sub], cands[sub], thrs[sub]) for sub in range(SUBS))

    thrs = lax.fori_loop(0, 32, value_step,
                         tuple(jnp.full((1, Q_BLOCK), INT_MIN, I32) for _ in range(SUBS)))

    row_in_tile = lax.broadcasted_iota(I32, (Q_BLOCK, Q_BLOCK), 0)
    for sub in range(SUBS):
        thr = thrs[sub]
        keys = key_s[sub, 0:nkp, :]
        need = n_sel - count(keys > thr)
        split = count(keys == thr) > need

        j_s[...] = jnp.full((1, Q_BLOCK), 2 * nkp, I32)

        @pl.when(jnp.max(jnp.where(split, 1.0, 0.0)) > 0.5)
        def _(sub=sub, thr=thr, need=need):
            row = lax.broadcasted_iota(I32, (nkp, Q_BLOCK), 0)
            nbits = (2 * nkp - 1).bit_length()

            def index_step(bit, jmax):
                cand = jmax | jnp.left_shift(jnp.int32(1), nbits - 1 - bit)
                f = count_ones(jnp.where(key_s[sub, 0:nkp, :] == thr,
                                         jnp.where(row < cand, 1.0, 0.0), 0.0))
                return jnp.where(f <= need, cand, jmax)

            j_s[...] = lax.fori_loop(0, nbits, index_step, jnp.zeros((1, Q_BLOCK), I32))

        jmax = j_s[...]
        for t in range(n_tiles):
            kt = key_s[sub, t * Q_BLOCK:(t + 1) * Q_BLOCK, :]
            row = t * Q_BLOCK + row_in_tile
            add = jnp.where(kt > thr, 0.0,
                            jnp.where(kt == thr, jnp.where(row < jmax, 0.0, NEG_BIG), NEG_BIG))
            add = jnp.where(kt == INT_MIN, NEG_BIG, add)
            cm_s[sub, (sub + t) * Q_BLOCK:(sub + t + 1) * Q_BLOCK,
                 KV_LATENT:2 * KV_LATENT] = add.astype(BF16)

    tiles_per_chunk = ROW_CHUNK // Q_BLOCK
    n_pairs = ATT_HEADS // 2
    n_units = SUBS * n_pairs
    pair_bits = n_pairs.bit_length() - 1
    assert n_pairs == 1 << pair_bits

    def unit(logits=None, exps=None, values=None):
        if logits is not None:
            u_a, lbuf_a = logits
            sub_a = u_a >> pair_bits
            start_a = sub_a * Q_BLOCK
            off_t = _aligned((u_a & (n_pairs - 1)) * PAIR_LANES, PAIR_LANES)
            rhs = qT_s[:, pl.ds(_aligned(u_a * PAIR_LANES, PAIR_LANES), PAIR_LANES)]
            m8 = jnp.full((8, PAIR_LANES), NEG_INF, F32)
        if exps is not None:
            lbuf_b, pbuf_b, m_b = exps
        if values is not None:
            u_c, pbuf_c = values
            start_c = (u_c >> pair_bits) * Q_BLOCK
            acc = None
        for rc in range(n_chunks):
            rows = slice(rc * ROW_CHUNK, (rc + 1) * ROW_CHUNK)
            if logits is not None:
                win = pl.ds(_aligned(start_a + rc * ROW_CHUNK, Q_BLOCK), ROW_CHUNK)
                lc = jnp.dot(cm_s[sub_a, win, :], rhs, preferred_element_type=F32)
                for tt in range(tiles_per_chunk):
                    t = rc * tiles_per_chunk + tt
                    lt = lc[tt * Q_BLOCK:(tt + 1) * Q_BLOCK, :]
                    if t >= n_tiles - 2:
                        lt = lt + tab_ref[n_tiles - 1 - t, :, pl.ds(off_t, PAIR_LANES)]
                    lbuf_a[t * Q_BLOCK:(t + 1) * Q_BLOCK, :] = lt
                    m8 = jnp.maximum(m8, jnp.max(lt.reshape(Q_BLOCK // 8, 8, PAIR_LANES), axis=0))
            if exps is not None:
                pbuf_b[rows, :] = jnp.exp2(lbuf_b[rows, :] - m_b).astype(BF16)
            if values is not None:
                win = pl.ds(_aligned(start_c + rc * ROW_CHUNK, Q_BLOCK), ROW_CHUNK)
                part = jnp.dot(ckvT_s[:, win], pbuf_c[rows, :], preferred_element_type=F32)
                acc = part if acc is None else acc + part
        if values is not None:
            ot_s[:, pl.ds(_aligned(u_c * PAIR_LANES, PAIR_LANES), PAIR_LANES)] = (
                acc[0:KV_LATENT, :] / acc[KV_LATENT:KV_LATENT + 1, :]).astype(BF16)
        if logits is not None:
            return jnp.max(m8, axis=0, keepdims=True)

    def two_units(jj, m_odd):
        u = 2 * jj
        m_even = unit(logits=(u + 2, l0_s), exps=(l1_s, p1_s, m_odd), values=(u, p0_s))
        return unit(logits=(u + 3, l1_s), exps=(l0_s, p0_s, m_even), values=(u + 1, p1_s))

    m_even = unit(logits=(0, l0_s))
    m_odd = unit(logits=(1, l1_s), exps=(l0_s, p0_s, m_even))
    m_odd = lax.fori_loop(0, n_units // 2 - 1, two_units, m_odd)
    unit(exps=(l1_s, p1_s, m_odd), values=(n_units - 2, p0_s))
    unit(values=(n_units - 1, p1_s))


def _attn_kernel(q_ref, qi_ref, mq_ref, cr_ref, mk_ref, wukT_ref, wuv_ref, kvn_ref, tab_ref, o_ref,
                 cm_s, ckvT_s, kidx_s, qT_s, key_s, j_s, l0_s, l1_s, p0_s, p1_s, ot_s, attT_s):
    step = pl.program_id(1)
    seq = cr_ref.shape[0]

    @pl.when(step == 0)
    def _():
        c = _rms(cr_ref[...], kvn_ref[...])
        for sub in range(SUBS):
            cm_s[sub, 0:PAD_ROWS, :] = jnp.zeros((PAD_ROWS, 2 * KV_LATENT), BF16)
            cm_s[sub, PAD_ROWS:, 0:KV_LATENT] = c.astype(BF16)
        ckvT_s[0:KV_LATENT, 0:PAD_ROWS] = jnp.zeros((KV_LATENT, PAD_ROWS), BF16)
        ckvT_s[0:KV_LATENT, PAD_ROWS:] = c.T.astype(BF16)
        ckvT_s[KV_LATENT:, :] = jnp.ones((ONES_ROWS, seq + PAD_ROWS), BF16)
        kidx_s[0:PAD_ROWS, :] = jnp.zeros((PAD_ROWS, IDX_DIM), BF16)
        kidx_s[PAD_ROWS:, :] = mk_ref[:, MISC_KIDX:MISC_KIDX + IDX_DIM].astype(BF16)
        eye = jnp.where(lax.broadcasted_iota(I32, (Q_BLOCK, Q_BLOCK), 0)
                        == lax.broadcasted_iota(I32, (Q_BLOCK, Q_BLOCK), 1), 1.0, 0.0).astype(BF16)
        for blk in range(SUBS * ATT_HEADS):
            qT_s[KV_LATENT:, blk * Q_BLOCK:(blk + 1) * Q_BLOCK] = eye

    scale = ATT_HEAD_DIM ** -0.5 * LOG2E
    qb = q_ref[...].astype(BF16)
    for sub in range(SUBS):
        for h in range(ATT_HEADS):
            qh = qb[sub * Q_BLOCK:(sub + 1) * Q_BLOCK, h * ATT_HEAD_DIM:(h + 1) * ATT_HEAD_DIM]
            qt = lax.dot_general(wukT_ref[h], qh, (((1,), (1,)), ((), ())),
                                 preferred_element_type=F32)
            blk = sub * ATT_HEADS + h
            qT_s[0:KV_LATENT, blk * Q_BLOCK:(blk + 1) * Q_BLOCK] = (qt * scale).astype(BF16)

    qiT = qi_ref[...].T.astype(BF16)
    wT = mq_ref[...].T

    for k in range(seq // Q_BLOCK // SUBS):
        @pl.when(step == k)
        def _(k=k):
            _attn_step(k, qiT, wT, tab_ref, cm_s, ckvT_s, kidx_s, qT_s, key_s, j_s, l0_s, l1_s,
                       p0_s, p1_s, ot_s)

    for sub in range(SUBS):
        for h in range(ATT_HEADS):
            blk = sub * ATT_HEADS + h
            attT_s[h * ATT_HEAD_DIM:(h + 1) * ATT_HEAD_DIM, sub * Q_BLOCK:(sub + 1) * Q_BLOCK] = (
                jnp.dot(wuv_ref[h], ot_s[:, blk * Q_BLOCK:(blk + 1) * Q_BLOCK],
                        preferred_element_type=F32))
    o_ref[...] = attT_s[...].T.astype(BF16)


def _dsa_attention(proj, wukT, wuv, kv_norm, tab, batch, seq):
    rows = SUBS * Q_BLOCK
    nsteps = seq // rows
    att_dim = ATT_HEADS * ATT_HEAD_DIM
    return pl.pallas_call(
        _attn_kernel,
        grid=(batch, nsteps),
        in_specs=[
            pl.BlockSpec((rows, att_dim), lambda b, i: (b * nsteps + i, COL_Q // att_dim)),
            pl.BlockSpec((rows, IDX_HEADS * IDX_DIM),
                         lambda b, i: (b * nsteps + i, COL_QI // (IDX_HEADS * IDX_DIM))),
            pl.BlockSpec((rows, V7X_LANES), lambda b, i: (b * nsteps + i, COL_MISC // V7X_LANES)),
            pl.BlockSpec((seq, KV_LATENT), lambda b, i: (b, COL_CR // KV_LATENT)),
            pl.BlockSpec((seq, V7X_LANES), lambda b, i: (b, COL_MISC // V7X_LANES)),
            pl.BlockSpec((ATT_HEADS, KV_LATENT, ATT_HEAD_DIM), lambda b, i: (0, 0, 0)),
            pl.BlockSpec((ATT_HEADS, ATT_HEAD_DIM, KV_LATENT), lambda b, i: (0, 0, 0)),
            pl.BlockSpec((1, KV_LATENT), lambda b, i: (0, 0)),
            pl.BlockSpec((2, Q_BLOCK, ATT_HEADS * Q_BLOCK), lambda b, i: (0, 0, 0)),
        ],
        out_specs=pl.BlockSpec((rows, att_dim), lambda b, i: (b * nsteps + i, 0)),
        out_shape=jax.ShapeDtypeStruct((batch * seq, att_dim), BF16),
        scratch_shapes=[
            pltpu.VMEM((SUBS, seq + PAD_ROWS, 2 * KV_LATENT), BF16),
            pltpu.VMEM((PV_ROWS, seq + PAD_ROWS), BF16),
            pltpu.VMEM((seq + PAD_ROWS, IDX_DIM), BF16),
            pltpu.VMEM((2 * KV_LATENT, SUBS * ATT_HEADS * Q_BLOCK), BF16),
            pltpu.VMEM((SUBS, seq, Q_BLOCK), I32),
            pltpu.VMEM((1, Q_BLOCK), I32),
            pltpu.VMEM((seq, PAIR_LANES), F32),
            pltpu.VMEM((seq, PAIR_LANES), F32),
            pltpu.VMEM((seq, PAIR_LANES), BF16),
            pltpu.VMEM((seq, PAIR_LANES), BF16),
            pltpu.VMEM((KV_LATENT, SUBS * ATT_HEADS * Q_BLOCK), BF16),
            pltpu.VMEM((att_dim, rows), F32),
        ],
        compiler_params=pltpu.CompilerParams(
            dimension_semantics=("arbitrary", "arbitrary"), vmem_limit_bytes=VMEM_LIMIT),
        name="dsa_attn",
    )(proj, proj, proj, proj, proj, wukT, wuv, kv_norm, tab)


TAIL = 8
GROUP_LANES = SSM_HEADS_PER_GROUP * SSM_HEAD_DIM


def _expand_matrices():
    e64 = np.zeros((V7X_LANES, SSM_INNER), np.float32)
    e128 = np.zeros((V7X_LANES, SSM_HEADS * SSM_CHUNK), np.float32)
    for h in range(SSM_HEADS):
        for piece in range(3):
            lane = (MISC_DT + h + piece * SSM_HEADS) % V7X_LANES
            e64[lane, h * SSM_HEAD_DIM:(h + 1) * SSM_HEAD_DIM] = 1.0
            e128[lane, h * SSM_CHUNK:(h + 1) * SSM_CHUNK] = 1.0
    return jnp.asarray(e64, BF16), jnp.asarray(e128, BF16)


def _split3(v):
    hi = v.astype(BF16).astype(F32)
    r1 = v - hi
    mid = r1.astype(BF16).astype(F32)
    lo = r1 - mid
    packed = hi + pltpu.roll(mid, SSM_HEADS, axis=1) + pltpu.roll(lo, 2 * SSM_HEADS, axis=1)
    return packed.astype(BF16)


def _conv_silu(u_ref, tail_s, w_ref, b_ref):
    n = SSM_CHUNK
    u = u_ref[...]
    ext = jnp.concatenate([tail_s[...], u], axis=0)
    out = b_ref[...] + w_ref[SSM_CONV - 1:SSM_CONV, :] * u
    for k in range(1, SSM_CONV):
        out = out + w_ref[SSM_CONV - 1 - k:SSM_CONV - k, :] * pltpu.roll(ext, k, axis=0)[TAIL:, :]
    tail_s[...] = u[n - TAIL:n, :]
    return _silu(out)


SSD_CHUNKS_PER_STEP = 2
INPROJ_TM = SSD_CHUNKS_PER_STEP * SSM_CHUNK
INPROJ_TN = V7X_MXU_WIDTH
assert PROJ_COLS % INPROJ_TN == 0 and W_COLS % INPROJ_TN == 0


def _inproj_ssd_kernel(x_ref, g_ref, w_ref, *rest, tiles_per_seq):
    ssd_params = rest[:-9]
    proj_ref, yn_ref, h_s, ssd_in_s, misc_s, xtail_s, btail_s, ctail_s, state_s = rest[-9:]
    i = pl.program_id(0)

    @pl.when(i == 0)
    def _():
        ssd_in_s[...] = jnp.zeros_like(ssd_in_s)
        misc_s[...] = jnp.zeros_like(misc_s)

    @pl.when(lax.rem(jnp.maximum(i - 1, 0), tiles_per_seq) == 0)
    def _():
        state_s[...] = jnp.zeros_like(state_s)
        xtail_s[...] = jnp.zeros_like(xtail_s)
        btail_s[...] = jnp.zeros_like(btail_s)
        ctail_s[...] = jnp.zeros_like(ctail_s)

    def step(slot):
        prev = 1 - slot
        h_s[...] = _rms(x_ref[...], g_ref[...]).astype(BF16)
        todo = list(range(0, W_COLS, INPROJ_TN))

        def project(count=1):
            for _ in range(min(count, len(todo))):
                col = todo.pop(0)
                r = jnp.dot(h_s[...], w_ref[:, col:col + INPROJ_TN], preferred_element_type=F32)
                if col < PROJ_COLS:
                    proj_ref[:, col:col + INPROJ_TN] = r
                    if col <= COL_MISC < col + INPROJ_TN:
                        misc_s[slot] = r[:, COL_MISC - col:COL_MISC - col + V7X_LANES]
                else:
                    ssd_in_s[slot, :, col - PROJ_COLS:col - PROJ_COLS + INPROJ_TN] = r

        for c in range(SSD_CHUNKS_PER_STEP):
            rows = pl.ds(c * SSM_CHUNK, SSM_CHUNK)

            def cols(col0, width):
                return ssd_in_s.at[prev, rows, pl.ds(col0 - PROJ_COLS, width)]

            _ssd_chunk(cols(COL_Z, SSM_INNER), cols(COL_XS, SSM_INNER), cols(COL_BM, SSM_BC),
                       cols(COL_CM, SSM_BC), misc_s.at[prev, rows], *ssd_params, yn_ref.at[rows],
                       xtail_s, btail_s, ctail_s, state_s, between=project)
        project(len(todo))

    for parity in range(2):
        pl.when(lax.rem(i, 2) == parity)(functools.partial(step, parity))


def _ssd_chunk(z_ref, xs_ref, bm_ref, cm_ref, misc_ref, cwx_ref, cwb_ref, cwc_ref, cbx_ref, cbb_ref,
               cbc_ref, dtb_ref, alog_ref, dsk_ref, gn_ref, e64_ref, e128_ref, o_ref,
               xtail_s, btail_s, ctail_s, state_s, between=lambda count=1: None):
    n = SSM_CHUNK
    xs = _conv_silu(xs_ref, xtail_s, cwx_ref, cbx_ref)
    between(2)
    bm = _conv_silu(bm_ref, btail_s, cwb_ref, cbb_ref)
    cm = _conv_silu(cm_ref, ctail_s, cwc_ref, cbc_ref)
    between()

    lane = lax.broadcasted_iota(I32, (n, V7X_LANES), 1)
    row = lax.broadcasted_iota(I32, (n, V7X_LANES), 0)
    on_dt = (lane >= MISC_DT) & (lane < MISC_DT + SSM_HEADS)

    v = misc_ref[...] + dtb_ref[...]
    dt = jnp.maximum(v, 0.0) + jnp.log1p(jnp.exp(-jnp.abs(v)))
    dt = jnp.where(on_dt, dt, 0.0)
    a_neg = -jnp.exp(alog_ref[...])
    acum = dt * a_neg
    k = 1
    while k < n:
        acum = acum + jnp.where(row >= k, pltpu.roll(acum, k, axis=0), 0.0)
        k *= 2

    e64 = e64_ref[...]
    dt_e = jnp.dot(_split3(dt), e64, preferred_element_type=F32)
    acum_p = _split3(acum)
    acum_e = jnp.dot(acum_p, e64, preferred_element_type=F32)
    acum_cb = jnp.dot(acum_p, e128_ref[...], preferred_element_type=F32)
    acum_t = acum.T
    between()

    last = acum_e[n - 1:n, :]
    xdt = xs * dt_e
    xds_b = (xdt * jnp.exp(last - acum_e)).astype(BF16)
    chunk_decay = jnp.exp(last)
    ea_e = jnp.exp(acum_e)
    between()

    tri = row >= lane
    first_head = lane < SSM_HEAD_DIM
    y_groups = []
    for g in range(SSM_GROUPS):
        gl = slice(g * GROUP_LANES, (g + 1) * GROUP_LANES)
        bmg = bm[:, g * SSM_STATE:(g + 1) * SSM_STATE]
        cmg_b = cm[:, g * SSM_STATE:(g + 1) * SSM_STATE].astype(BF16)
        cb = lax.dot_general(cmg_b, bmg.astype(BF16), (((1,), (1,)), ((), ())),
                             preferred_element_type=F32)
        y_pairs = []
        for pr in range(SSM_HEADS_PER_GROUP // 2):
            h0 = g * SSM_HEADS_PER_GROUP + 2 * pr
            mats = []
            for h in (h0, h0 + 1):
                seg = acum_cb[:, h * n:(h + 1) * n] - acum_t[MISC_DT + h:MISC_DT + h + 1, :]
                mats.append((cb * jnp.exp(jnp.where(tri, seg, NEG_INF))).astype(BF16))
            xp = xdt[:, h0 * SSM_HEAD_DIM:(h0 + 2) * SSM_HEAD_DIM]
            rhs = jnp.concatenate([jnp.where(first_head, xp, 0.0), jnp.where(first_head, 0.0, xp)],
                                  axis=0).astype(BF16)
            y_pairs.append(jnp.dot(jnp.concatenate(mats, axis=1), rhs, preferred_element_type=F32))
            if pr % 2 == 1:
                between()
        prev = state_s[g]
        y_off = jnp.dot(cmg_b, prev.astype(BF16), preferred_element_type=F32) * ea_e[:, gl]
        state_s[g] = prev * chunk_decay[:, gl] + jnp.dot(bmg.T.astype(BF16), xds_b[:, gl],
                                                         preferred_element_type=F32)
        y_groups.append(jnp.concatenate(y_pairs, axis=1) + y_off)
        between()

    z = z_ref[...]
    gz = _silu(z)
    outs = []
    for g in range(SSM_GROUPS):
        gl = slice(g * GROUP_LANES, (g + 1) * GROUP_LANES)
        u = (y_groups[g] + dsk_ref[:, gl] * xs[:, gl]) * gz[:, gl]
        outs.append(u * lax.rsqrt(jnp.mean(u * u, axis=-1, keepdims=True) + EPS) * gn_ref[:, gl])
        if g % 2 == 1:
            between()
    o_ref[...] = jnp.concatenate(outs, axis=1).astype(BF16)


def _inproj_ssd(x2d, gain, w_packed, conv_w, conv_b, dt_bias, a_log, d_skip, ssm_norm, seq):
    t = x2d.shape[0]
    n = INPROJ_TM
    n_tiles = t // n
    cwx, cwb, cwc = (conv_w[:, :SSM_INNER], conv_w[:, SSM_INNER:SSM_INNER + SSM_BC],
                     conv_w[:, SSM_INNER + SSM_BC:])
    cbx, cbb, cbc = (conv_b[None, :SSM_INNER], conv_b[None, SSM_INNER:SSM_INNER + SSM_BC],
                     conv_b[None, SSM_INNER + SSM_BC:])
    pad_l, pad_r = MISC_DT, V7X_LANES - MISC_DT - SSM_HEADS
    dtb = jnp.pad(dt_bias, (pad_l, pad_r))[None, :]
    alog = jnp.pad(a_log, (pad_l, pad_r))[None, :]
    dsk = jnp.repeat(d_skip, SSM_HEAD_DIM)[None, :]
    gn = ssm_norm[None, :]
    e64, e128 = _expand_matrices()
    full = lambda shape, **kw: pl.BlockSpec(shape, lambda i: (0,) * len(shape), **kw)
    last = n_tiles - 1
    return pl.pallas_call(
        functools.partial(_inproj_ssd_kernel, tiles_per_seq=seq // n),
        grid=(n_tiles + 1,),
        in_specs=[
            pl.BlockSpec((n, D_MODEL), lambda i: (jnp.minimum(i, last), 0)),
            full((1, D_MODEL)),
            full((D_MODEL, W_COLS), pipeline_mode=pl.Buffered(1)),
            full((SSM_CONV, SSM_INNER)), full((SSM_CONV, SSM_BC)), full((SSM_CONV, SSM_BC)),
            full((1, SSM_INNER)), full((1, SSM_BC)), full((1, SSM_BC)),
            full((1, V7X_LANES)), full((1, V7X_LANES)),
            full((1, SSM_INNER)), full((1, SSM_INNER)),
            full((V7X_LANES, SSM_INNER)), full((V7X_LANES, SSM_HEADS * SSM_CHUNK)),
        ],
        out_specs=[pl.BlockSpec((n, PROJ_COLS), lambda i: (jnp.minimum(i, last), 0)),
                   pl.BlockSpec((n, SSM_INNER), lambda i: (jnp.maximum(i - 1, 0), 0))],
        out_shape=[jax.ShapeDtypeStruct((t, PROJ_COLS), F32),
                   jax.ShapeDtypeStruct((t, SSM_INNER), BF16)],
        scratch_shapes=[
            pltpu.VMEM((n, D_MODEL), BF16),
            pltpu.VMEM((2, n, SSD_COLS), F32),
            pltpu.VMEM((2, n, V7X_LANES), F32),
            pltpu.VMEM((TAIL, SSM_INNER), F32),
            pltpu.VMEM((TAIL, SSM_BC), F32),
            pltpu.VMEM((TAIL, SSM_BC), F32),
            pltpu.VMEM((SSM_GROUPS, SSM_STATE, GROUP_LANES), F32),
        ],
        compiler_params=pltpu.CompilerParams(
            dimension_semantics=("arbitrary",), vmem_limit_bytes=VMEM_LIMIT),
        name="inproj_ssd",
    )(x2d, gain, w_packed, cwx, cwb, cwc, cbx, cbb, cbc, dtb, alog, dsk, gn, e64, e128)


MERGE_TM = 512


def _merge_kernel(att_ref, yn_ref, ga_ref, gs_ref, x_ref, wao_ref, wso_ref, wo_ref, gf_ref,
                  x1_ref, h2_ref):
    y_att = jnp.dot(att_ref[...], wao_ref[...], preferred_element_type=F32)
    y_ssm = jnp.dot(yn_ref[...], wso_ref[...], preferred_element_type=F32)
    merged = _sigmoid(ga_ref[...]) * y_att + _sigmoid(gs_ref[...]) * y_ssm
    x1 = x_ref[...] + jnp.dot(merged.astype(BF16), wo_ref[...], preferred_element_type=F32)
    x1_ref[...] = x1
    h2_ref[...] = _rms(x1, gf_ref[...]).astype(BF16)


def _merge(att, yn, proj, x2d, wao, wso, wo, norm_ffn):
    t = x2d.shape[0]
    tm = MERGE_TM
    full = lambda shape: pl.BlockSpec(shape, lambda i: (0,) * len(shape))
    return pl.pallas_call(
        _merge_kernel,
        grid=(t // tm,),
        in_specs=[
            pl.BlockSpec((tm, D_MODEL), lambda i: (i, 0)),
            pl.BlockSpec((tm, SSM_INNER), lambda i: (i, 0)),
            pl.BlockSpec((tm, D_MODEL), lambda i: (i, COL_GA // D_MODEL)),
            pl.BlockSpec((tm, D_MODEL), lambda i: (i, COL_GS // D_MODEL)),
            pl.BlockSpec((tm, D_MODEL), lambda i: (i, 0)),
            full((D_MODEL, D_MODEL)), full((SSM_INNER, D_MODEL)), full((D_MODEL, D_MODEL)),
            full((1, D_MODEL)),
        ],
        out_specs=[pl.BlockSpec((tm, D_MODEL), lambda i: (i, 0)),
                   pl.BlockSpec((tm, D_MODEL), lambda i: (i, 0))],
        out_shape=[jax.ShapeDtypeStruct((t, D_MODEL), F32),
                   jax.ShapeDtypeStruct((t, D_MODEL), BF16)],
        compiler_params=pltpu.CompilerParams(
            dimension_semantics=("arbitrary",), vmem_limit_bytes=VMEM_LIMIT),
        name="merge",
    )(att, yn, proj, proj, x2d, wao, wso, wo, norm_ffn)


FFN_TM = 512
FFN_HALO = 16
FFN_FC = V7X_MXU_WIDTH


def _ffn_kernel(h_ref, halo_ref, x1_ref, wup_ref, cw_ref, cb_ref, wdn_ref, gf_ref, o_ref,
                acc_s, *, tiles_per_seq):
    keep = jnp.where(pl.program_id(0) % tiles_per_seq == 0, 0.0, 1.0)
    hcat = jnp.concatenate([halo_ref[...], h_ref[...]], axis=0)
    n_chunks = FFN_DIM // FFN_FC

    def up(c, base):
        return jnp.dot(hcat, wup_ref[:, base + c * FFN_FC:base + (c + 1) * FFN_FC],
                       preferred_element_type=F32)

    def conv(u, c, base):
        cols = slice(base + c * FFN_FC, base + (c + 1) * FFN_FC)
        u = jnp.concatenate([u[0:FFN_HALO, :] * keep, u[FFN_HALO:, :]], axis=0)
        out = cb_ref[:, cols] + cw_ref[FFN_CONV - 1:FFN_CONV, cols] * u[FFN_HALO:, :]
        for k in range(1, FFN_CONV):
            out = out + (cw_ref[FFN_CONV - 1 - k:FFN_CONV - k, cols]
                         * pltpu.roll(u, k, axis=0)[FFN_HALO:, :])
        return out

    ug, uv = up(0, 0), up(0, FFN_DIM)
    for c in range(n_chunks):
        last = c + 1 == n_chunks
        ug_next = None if last else up(c + 1, 0)
        gate = conv(ug, c, 0)
        uv_next = None if last else up(c + 1, FFN_DIM)
        val = conv(uv, c, FFN_DIM)
        ug, uv = ug_next, uv_next
        act = (_silu(gate) * val).astype(BF16)
        contrib = jnp.dot(act, wdn_ref[c * FFN_FC:(c + 1) * FFN_FC, :], preferred_element_type=F32)
        if c == 0:
            acc_s[...] = contrib
        else:
            acc_s[...] += contrib
    o_ref[...] = _rms(x1_ref[...] + acc_s[...], gf_ref[...])


def _ffn(h2, x1, wup, conv_w, conv_b, wdn, norm_final, seq):
    t = h2.shape[0]
    tm = FFN_TM
    halo_blocks = tm // FFN_HALO
    single = pl.Buffered(1)
    full = lambda shape, **kw: pl.BlockSpec(shape, lambda i: (0,) * len(shape), **kw)
    return pl.pallas_call(
        functools.partial(_ffn_kernel, tiles_per_seq=seq // tm),
        grid=(t // tm,),
        in_specs=[
            pl.BlockSpec((tm, D_MODEL), lambda i: (i, 0)),
            pl.BlockSpec((FFN_HALO, D_MODEL), lambda i: (jnp.maximum(i * halo_blocks - 1, 0), 0)),
            pl.BlockSpec((tm, D_MODEL), lambda i: (i, 0)),
            full((D_MODEL, 2 * FFN_DIM), pipeline_mode=single),
            full((FFN_CONV, 2 * FFN_DIM)),
            full((1, 2 * FFN_DIM)),
            full((FFN_DIM, D_MODEL), pipeline_mode=single),
            full((1, D_MODEL)),
        ],
        out_specs=pl.BlockSpec((tm, D_MODEL), lambda i: (i, 0)),
        out_shape=jax.ShapeDtypeStruct((t, D_MODEL), F32),
        scratch_shapes=[pltpu.VMEM((tm, D_MODEL), F32)],
        compiler_params=pltpu.CompilerParams(
            dimension_semantics=("arbitrary",), vmem_limit_bytes=VMEM_LIMIT),
        name="ffn",
    )(h2, h2, x1, wup, conv_w, conv_b, wdn, norm_final)


def kernel(x, rel_bias, norm_mix, w_in, kv_norm, w_uk, w_uv, conv_ssm_w, conv_ssm_b, dt_bias, a_log,
           d_skip, ssm_norm, w_att_out, w_ssm_out, w_out, norm_ffn, w_ffn_up, conv_ffn_w, conv_ffn_b,
           w_ffn_down, norm_final):
    batch, seq, _ = x.shape
    assert norm_mix.shape[0] == 1, "single layer"
    assert seq % (TILES_PER_CLASS * Q_BLOCK) == 0 and min(TOPK_MAX, seq // 4) == TOPK_MAX
    x2d = x.reshape(batch * seq, D_MODEL)

    tab = _bias_tiles(rel_bias)
    proj, yn = _inproj_ssd(x2d, norm_mix[0][None, :], _pack_w_in(w_in), conv_ssm_w[0], conv_ssm_b[0],
                           dt_bias[0], a_log[0], d_skip[0], ssm_norm[0], seq)
    att = _dsa_attention(proj, jnp.swapaxes(w_uk[0], 1, 2).astype(BF16), w_uv[0].astype(BF16),
                         kv_norm[0][None, :], tab, batch, seq)
    x1, h2 = _merge(att, yn, proj, x2d, w_att_out[0].astype(BF16), w_ssm_out[0].astype(BF16),
                    w_out[0].astype(BF16), norm_ffn[0][None, :])
    out = _ffn(h2, x1, w_ffn_up[0].astype(BF16), conv_ffn_w[0], conv_ffn_b[0][None, :],
               w_ffn_down[0].astype(BF16), norm_final[None, :], seq)
    return out.reshape(batch, seq, D_MODEL)
```

```python
import functools
import math

import numpy as np
import jax
import jax.numpy as jnp
from jax import lax
from jax.experimental import pallas as pl
from jax.experimental.pallas import tpu as pltpu

F32 = jnp.float32
BF16 = jnp.bfloat16
I32 = jnp.int32

D_MODEL = 1024
ATT_HEADS = 16
ATT_HEAD_DIM = 64
KV_LATENT = 128
IDX_HEADS = 8
IDX_DIM = 64
TOPK_MAX = 256
Q_BLOCK = 128
REL_BUCKETS = 32
REL_MAX_DIST = 128
SSM_INNER = 2 * D_MODEL
SSM_HEAD_DIM = 64
SSM_HEADS = SSM_INNER // SSM_HEAD_DIM
SSM_GROUPS = 4
SSM_HEADS_PER_GROUP = SSM_HEADS // SSM_GROUPS
SSM_STATE = 128
SSM_CONV = 4
SSM_CHUNK = 128
SSM_BC = SSM_GROUPS * SSM_STATE
FFN_DIM = 2816
FFN_CONV = 3
EPS = 1e-6

V7X_LANES = 128
V7X_MXU_WIDTH = 256
V7X_VMEM_BYTES = 64 * 1024 * 1024
VMEM_LIMIT = 56 * 1024 * 1024

INT_MIN = -(2 ** 31)
NEG_INF = float("-inf")
LOG2E = math.log2(math.e)

COL_Q = 0
COL_GA = COL_Q + D_MODEL
COL_GS = COL_GA + D_MODEL
COL_QI = COL_GS + D_MODEL
COL_CR = COL_QI + IDX_HEADS * IDX_DIM
COL_MISC = COL_CR + KV_LATENT
PROJ_COLS = COL_MISC + V7X_LANES
COL_Z = PROJ_COLS
COL_XS = COL_Z + SSM_INNER
COL_BM = COL_XS + SSM_INNER
COL_CM = COL_BM + SSM_BC
W_COLS = COL_CM + SSM_BC
SSD_COLS = W_COLS - PROJ_COLS
MISC_KIDX = 0
MISC_WIDX = IDX_DIM
MISC_DT = IDX_DIM + IDX_HEADS


def _pack_moves():
    sizes = (ATT_HEADS * ATT_HEAD_DIM, KV_LATENT, IDX_HEADS * IDX_DIM, IDX_DIM, IDX_HEADS,
             SSM_INNER, SSM_INNER, SSM_BC, SSM_BC, SSM_HEADS, D_MODEL, D_MODEL)
    src = np.cumsum((0,) + sizes)
    dst = (COL_Q, COL_CR, COL_QI, COL_MISC + MISC_KIDX, COL_MISC + MISC_WIDX, COL_Z, COL_XS, COL_BM,
           COL_CM, COL_MISC + MISC_DT, COL_GA, COL_GS)
    return [(int(s), int(d), int(n)) for s, d, n in zip(src[:-1], dst, sizes)]


PACK_ROWS = 128


def _pack_kernel(wt_ref, o_ref):
    misc = []
    for s, d, n in _pack_moves():
        if COL_MISC <= d < PROJ_COLS:
            assert d == COL_MISC + sum(p.shape[0] for p in misc)
            misc.append(wt_ref[s:s + n, :])
        else:
            o_ref[:, d:d + n] = wt_ref[s:s + n, :].T.astype(BF16)
    used = sum(p.shape[0] for p in misc)
    misc.append(jnp.zeros((V7X_LANES - used, PACK_ROWS), F32))
    o_ref[:, COL_MISC:PROJ_COLS] = jnp.concatenate(misc, axis=0).T.astype(BF16)


def _pack_w_in(w):
    _, k, cols = w.shape
    wt = jnp.swapaxes(w, 1, 2)
    assert all(s % 8 == 0 for s, _, _ in _pack_moves())
    return pl.pallas_call(
        _pack_kernel,
        grid=(k // PACK_ROWS,),
        in_specs=[pl.BlockSpec((None, cols, PACK_ROWS), lambda i: (0, 0, i))],
        out_specs=pl.BlockSpec((PACK_ROWS, W_COLS), lambda i: (i, 0)),
        out_shape=jax.ShapeDtypeStruct((k, W_COLS), BF16),
        name="pack_w_in",
    )(wt)


def _rms(x, gain):
    return x * lax.rsqrt(jnp.mean(x * x, axis=-1, keepdims=True) + EPS) * gain


def _sigmoid(x):
    return 0.5 + 0.5 * jnp.tanh(0.5 * x)


def _silu(x):
    h = 0.5 * x
    return h + h * jnp.tanh(h)


def _bias_tiles_kernel(rb_ref, o_ref):
    h = pl.program_id(0)
    far = rb_ref[REL_BUCKETS - 1, h]
    key = lax.broadcasted_iota(I32, (Q_BLOCK, Q_BLOCK), 0)
    qry = lax.broadcasted_iota(I32, (Q_BLOCK, Q_BLOCK), 1)
    max_exact = REL_BUCKETS // 2
    for delta in range(2):
        n = jnp.maximum(delta * Q_BLOCK + qry - key, 0)
        nf = jnp.maximum(n, 1).astype(F32)
        large = max_exact + (jnp.log(nf / max_exact) / math.log(REL_MAX_DIST / max_exact)
                             * (REL_BUCKETS - max_exact)).astype(I32)
        large = jnp.minimum(large, REL_BUCKETS - 1)
        bucket = jnp.where(n < max_exact, n, large)
        val = jnp.zeros((Q_BLOCK, Q_BLOCK), F32)
        for b in range(REL_BUCKETS):
            val = jnp.where(bucket == b, rb_ref[b, h] - far, val)
        o_ref[delta] = val * LOG2E


def _bias_tiles(rel_bias):
    return pl.pallas_call(
        _bias_tiles_kernel,
        grid=(ATT_HEADS,),
        in_specs=[pl.BlockSpec(memory_space=pltpu.SMEM)],
        out_specs=pl.BlockSpec((2, Q_BLOCK, Q_BLOCK), lambda h: (0, 0, h)),
        out_shape=jax.ShapeDtypeStruct((2, Q_BLOCK, ATT_HEADS * Q_BLOCK), F32),
        name="bias_tiles",
    )(rel_bias)


TILES_PER_CLASS = 2
PAD_TILES = TILES_PER_CLASS - 1
PAD_ROWS = PAD_TILES * Q_BLOCK
ROW_CHUNK = 256
COUNT_ROWS = 64
PAIR_LANES = 2 * Q_BLOCK
assert PAIR_LANES == V7X_MXU_WIDTH and (TILES_PER_CLASS * Q_BLOCK) % ROW_CHUNK == 0
ONES_ROWS = 16
PV_ROWS = KV_LATENT + ONES_ROWS
NEG_BIG = -1e30


SUBS = TILES_PER_CLASS


def _aligned(x, m):
    return x if isinstance(x, int) else pl.multiple_of(x, m)


def _attn_step(k, qiT, wT, tab_ref, cm_s, ckvT_s, kidx_s, qT_s, key_s, j_s, l0_s, l1_s, p0_s, p1_s,
               ot_s):
    n_tiles = TILES_PER_CLASS * (k + 1)
    nkp = n_tiles * Q_BLOCK
    n_chunks = nkp // ROW_CHUNK
    n_sel = float(TOPK_MAX)
    lane = lax.broadcasted_iota(I32, (ROW_CHUNK, Q_BLOCK), 1)
    row_in_chunk = lax.broadcasted_iota(I32, (ROW_CHUNK, Q_BLOCK), 0)

    for sub in range(SUBS):
        start = sub * Q_BLOCK
        key0 = (sub - PAD_TILES) * Q_BLOCK
        q_pos = (SUBS * k + sub) * Q_BLOCK + lane
        ql = slice(sub * Q_BLOCK, (sub + 1) * Q_BLOCK)
        for rc in range(n_chunks):
            rows = slice(rc * ROW_CHUNK, (rc + 1) * ROW_CHUNK)
            kc = kidx_s[start + rc * ROW_CHUNK:start + (rc + 1) * ROW_CHUNK, :]
            acc = jnp.zeros((ROW_CHUNK, Q_BLOCK), F32)
            for hp in range(IDX_HEADS // 2):
                rhs = jnp.concatenate(
                    [qiT[(2 * hp) * IDX_DIM:(2 * hp + 1) * IDX_DIM, ql],
                     qiT[(2 * hp + 1) * IDX_DIM:(2 * hp + 2) * IDX_DIM, ql]], axis=1)
                lg = jnp.dot(kc, rhs, preferred_element_type=F32)
                w0 = wT[MISC_WIDX + 2 * hp:MISC_WIDX + 2 * hp + 1, ql]
                w1 = wT[MISC_WIDX + 2 * hp + 1:MISC_WIDX + 2 * hp + 2, ql]
                acc = (acc + jnp.maximum(lg[:, :Q_BLOCK], 0.0) * w0
                       + jnp.maximum(lg[:, Q_BLOCK:], 0.0) * w1)
            bits = pltpu.bitcast(acc, I32)
            bits = jnp.where(bits == INT_MIN, 0, bits)
            key = jnp.where(bits < 0, bits ^ 0x7FFFFFFF, bits)
            key_pos = key0 + rc * ROW_CHUNK + row_in_chunk
            if key0 + rc * ROW_CHUNK < 0:
                key = jnp.where(key_pos >= 0, key, INT_MIN)
            if rc == n_chunks - 1:
                key = jnp.where(key_pos <= q_pos, key, INT_MIN)
            key_s[sub, rows, :] = key

    SEARCH_PASSES = 32

    def count_ones(ones):
        part = jnp.sum(ones.reshape(nkp // COUNT_ROWS, COUNT_ROWS, Q_BLOCK), axis=0)
        return jnp.sum(part, axis=0, keepdims=True)

    def count(pred):
        return count_ones(jnp.where(pred, 1.0, 0.0))

    def search_pass(sub, p, thr):
        cand = thr ^ jnp.left_shift(jnp.int32(1), SEARCH_PASSES - 1 - p)
        return jnp.where(count(key_s[sub, 0:nkp, :] >= cand) >= n_sel, cand, thr)

    thr_init = jnp.full((1, Q_BLOCK), INT_MIN, I32)
    row_in_tile = lax.broadcasted_iota(I32, (Q_BLOCK, Q_BLOCK), 0)

    def finish_selection(sub, thr):
        keys = key_s[sub, 0:nkp, :]
        need = n_sel - count(keys > thr)
        split = count(keys == thr) > need

        j_s[...] = jnp.full((1, Q_BLOCK), 2 * nkp, I32)

        @pl.when(jnp.max(jnp.where(split, 1.0, 0.0)) > 0.5)
        def _(sub=sub, thr=thr, need=need):
            row = lax.broadcasted_iota(I32, (nkp, Q_BLOCK), 0)
            nbits = (2 * nkp - 1).bit_length()

            def index_step(bit, jmax):
                cand = jmax | jnp.left_shift(jnp.int32(1), nbits - 1 - bit)
                f = count_ones(jnp.where(key_s[sub, 0:nkp, :] == thr,
                                         jnp.where(row < cand, 1.0, 0.0), 0.0))
                return jnp.where(f <= need, cand, jmax)

            j_s[...] = lax.fori_loop(0, nbits, index_step, jnp.zeros((1, Q_BLOCK), I32))

        jmax = j_s[...]
        for t in range(n_tiles):
            kt = key_s[sub, t * Q_BLOCK:(t + 1) * Q_BLOCK, :]
            row = t * Q_BLOCK + row_in_tile
            add = jnp.where(kt > thr, 0.0,
                            jnp.where(kt == thr, jnp.where(row < jmax, 0.0, NEG_BIG), NEG_BIG))
            add = jnp.where(kt == INT_MIN, NEG_BIG, add)
            cm_s[sub, (sub + t) * Q_BLOCK:(sub + t + 1) * Q_BLOCK,
                 KV_LATENT:2 * KV_LATENT] = add.astype(BF16)

    tiles_per_chunk = ROW_CHUNK // Q_BLOCK
    n_pairs = ATT_HEADS // 2
    n_calls = n_pairs + 2
    loop_trips = n_pairs // 2 - 1
    edge_passes = (SEARCH_PASSES - 4 * 2 * loop_trips) // 4
    assert 4 * edge_passes + 8 * loop_trips == SEARCH_PASSES and n_calls == 4 + 2 * loop_trips

    def attend(sub, search_sub):
        start = sub * Q_BLOCK

        def unit(logits=None, exps=None, values=None, search=None):
            if logits is not None:
                hp_a, lbuf_a = logits
                off_t = _aligned(hp_a * PAIR_LANES, PAIR_LANES)
                off_q = _aligned((sub * n_pairs + hp_a) * PAIR_LANES, PAIR_LANES)
                rhs = qT_s[:, pl.ds(off_q, PAIR_LANES)]
                m8 = jnp.full((8, PAIR_LANES), NEG_INF, F32)
            if exps is not None:
                lbuf_b, pbuf_b, m_b = exps
            if values is not None:
                hp_c, pbuf_c = values
                acc = None
            thr, first_pass, n_pass = search
            done = 0
            for rc in range(n_chunks):
                rows = slice(rc * ROW_CHUNK, (rc + 1) * ROW_CHUNK)
                win = slice(start + rc * ROW_CHUNK, start + (rc + 1) * ROW_CHUNK)
                if logits is not None:
                    lc = jnp.dot(cm_s[sub, win, :], rhs, preferred_element_type=F32)
                    for tt in range(tiles_per_chunk):
                        t = rc * tiles_per_chunk + tt
                        lt = lc[tt * Q_BLOCK:(tt + 1) * Q_BLOCK, :]
                        if t >= n_tiles - 2:
                            lt = lt + tab_ref[n_tiles - 1 - t, :, pl.ds(off_t, PAIR_LANES)]
                        lbuf_a[t * Q_BLOCK:(t + 1) * Q_BLOCK, :] = lt
                        m8 = jnp.maximum(m8, jnp.max(lt.reshape(Q_BLOCK // 8, 8, PAIR_LANES), axis=0))
                if exps is not None:
                    pbuf_b[rows, :] = jnp.exp2(lbuf_b[rows, :] - m_b).astype(BF16)
                if values is not None:
                    part = jnp.dot(ckvT_s[:, win], pbuf_c[rows, :], preferred_element_type=F32)
                    acc = part if acc is None else acc + part
                while done < ((rc + 1) * n_pass) // n_chunks:
                    thr = search_pass(search_sub, first_pass + done, thr)
                    done += 1
            if values is not None:
                off_o = _aligned((sub * n_pairs + hp_c) * PAIR_LANES, PAIR_LANES)
                ot_s[:, pl.ds(off_o, PAIR_LANES)] = (
                    acc[0:KV_LATENT, :] / acc[KV_LATENT:KV_LATENT + 1, :]).astype(BF16)
            return (jnp.max(m8, axis=0, keepdims=True) if logits is not None else None), thr

        def passes(thr, first, count):
            return thr, first, (count if search_sub is not None else 0)

        def two_units(jj, carry):
            m_odd, thr = carry
            hp = 2 * jj
            first = 2 * edge_passes + 8 * jj
            m_even, thr = unit(logits=(hp + 2, l0_s), exps=(l1_s, p1_s, m_odd), values=(hp, p0_s),
                               search=passes(thr, first, 4))
            return unit(logits=(hp + 3, l1_s), exps=(l0_s, p0_s, m_even), values=(hp + 1, p1_s),
                        search=passes(thr, first + 4, 4))

        thr = thr_init if search_sub is not None else jnp.zeros((1, Q_BLOCK), I32)
        m_even, thr = unit(logits=(0, l0_s), search=passes(thr, 0, edge_passes))
        m_odd, thr = unit(logits=(1, l1_s), exps=(l0_s, p0_s, m_even),
                          search=passes(thr, edge_passes, edge_passes))
        m_odd, thr = lax.fori_loop(0, loop_trips, two_units, (m_odd, thr))
        tail = SEARCH_PASSES - 2 * edge_passes
        _, thr = unit(exps=(l1_s, p1_s, m_odd), values=(n_pairs - 2, p0_s),
                      search=passes(thr, tail, edge_passes))
        _, thr = unit(values=(n_pairs - 1, p1_s), search=passes(thr, tail + edge_passes, edge_passes))
        return thr

    thr = lax.fori_loop(0, SEARCH_PASSES, functools.partial(search_pass, 0), thr_init)
    for sub in range(SUBS):
        finish_selection(sub, thr)
        thr = attend(sub, sub + 1 if sub + 1 < SUBS else None)


def _attn_kernel(q_ref, qi_ref, mq_ref, cr_ref, mk_ref, wukT_ref, wuv_ref, kvn_ref, tab_ref, o_ref,
                 cm_s, ckvT_s, kidx_s, qT_s, key_s, j_s, l0_s, l1_s, p0_s, p1_s, ot_s, attT_s):
    step = pl.program_id(1)
    seq = cr_ref.shape[0]

    @pl.when(step == 0)
    def _():
        c = _rms(cr_ref[...], kvn_ref[...])
        for sub in range(SUBS):
            cm_s[sub, 0:PAD_ROWS, :] = jnp.zeros((PAD_ROWS, 2 * KV_LATENT), BF16)
            cm_s[sub, PAD_ROWS:, 0:KV_LATENT] = c.astype(BF16)
        ckvT_s[0:KV_LATENT, 0:PAD_ROWS] = jnp.zeros((KV_LATENT, PAD_ROWS), BF16)
        ckvT_s[0:KV_LATENT, PAD_ROWS:] = c.T.astype(BF16)
        ckvT_s[KV_LATENT:, :] = jnp.ones((ONES_ROWS, seq + PAD_ROWS), BF16)
        kidx_s[0:PAD_ROWS, :] = jnp.zeros((PAD_ROWS, IDX_DIM), BF16)
        kidx_s[PAD_ROWS:, :] = mk_ref[:, MISC_KIDX:MISC_KIDX + IDX_DIM].astype(BF16)
        eye = jnp.where(lax.broadcasted_iota(I32, (Q_BLOCK, Q_BLOCK), 0)
                        == lax.broadcasted_iota(I32, (Q_BLOCK, Q_BLOCK), 1), 1.0, 0.0).astype(BF16)
        for blk in range(SUBS * ATT_HEADS):
            qT_s[KV_LATENT:, blk * Q_BLOCK:(blk + 1) * Q_BLOCK] = eye

    scale = ATT_HEAD_DIM ** -0.5 * LOG2E
    qb = q_ref[...].astype(BF16)
    for sub in range(SUBS):
        for h in range(ATT_HEADS):
            qh = qb[sub * Q_BLOCK:(sub + 1) * Q_BLOCK, h * ATT_HEAD_DIM:(h + 1) * ATT_HEAD_DIM]
            qt = lax.dot_general(wukT_ref[h], qh, (((1,), (1,)), ((), ())),
                                 preferred_element_type=F32)
            blk = sub * ATT_HEADS + h
            qT_s[0:KV_LATENT, blk * Q_BLOCK:(blk + 1) * Q_BLOCK] = (qt * scale).astype(BF16)

    qiT = qi_ref[...].T.astype(BF16)
    wT = mq_ref[...].T

    for k in range(seq // Q_BLOCK // SUBS):
        @pl.when(step == k)
        def _(k=k):
            _attn_step(k, qiT, wT, tab_ref, cm_s, ckvT_s, kidx_s, qT_s, key_s, j_s, l0_s, l1_s,
                       p0_s, p1_s, ot_s)

    for sub in range(SUBS):
        for h in range(ATT_HEADS):
            blk = sub * ATT_HEADS + h
            attT_s[h * ATT_HEAD_DIM:(h + 1) * ATT_HEAD_DIM, sub * Q_BLOCK:(sub + 1) * Q_BLOCK] = (
                jnp.dot(wuv_ref[h], ot_s[:, blk * Q_BLOCK:(blk + 1) * Q_BLOCK],
                        preferred_element_type=F32))
    o_ref[...] = attT_s[...].T.astype(BF16)


def _dsa_attention(proj, wukT, wuv, kv_norm, tab, batch, seq):
    rows = SUBS * Q_BLOCK
    nsteps = seq // rows
    att_dim = ATT_HEADS * ATT_HEAD_DIM
    return pl.pallas_call(
        _attn_kernel,
        grid=(batch, nsteps),
        in_specs=[
            pl.BlockSpec((rows, att_dim), lambda b, i: (b * nsteps + i, COL_Q // att_dim)),
            pl.BlockSpec((rows, IDX_HEADS * IDX_DIM),
                         lambda b, i: (b * nsteps + i, COL_QI // (IDX_HEADS * IDX_DIM))),
            pl.BlockSpec((rows, V7X_LANES), lambda b, i: (b * nsteps + i, COL_MISC // V7X_LANES)),
            pl.BlockSpec((seq, KV_LATENT), lambda b, i: (b, COL_CR // KV_LATENT)),
            pl.BlockSpec((seq, V7X_LANES), lambda b, i: (b, COL_MISC // V7X_LANES)),
            pl.BlockSpec((ATT_HEADS, KV_LATENT, ATT_HEAD_DIM), lambda b, i: (0, 0, 0)),
            pl.BlockSpec((ATT_HEADS, ATT_HEAD_DIM, KV_LATENT), lambda b, i: (0, 0, 0)),
            pl.BlockSpec((1, KV_LATENT), lambda b, i: (0, 0)),
            pl.BlockSpec((2, Q_BLOCK, ATT_HEADS * Q_BLOCK), lambda b, i: (0, 0, 0)),
        ],
        out_specs=pl.BlockSpec((rows, att_dim), lambda b, i: (b * nsteps + i, 0)),
        out_shape=jax.ShapeDtypeStruct((batch * seq, att_dim), BF16),
        scratch_shapes=[
            pltpu.VMEM((SUBS, seq + PAD_ROWS, 2 * KV_LATENT), BF16),
            pltpu.VMEM((PV_ROWS, seq + PAD_ROWS), BF16),
            pltpu.VMEM((seq + PAD_ROWS, IDX_DIM), BF16),
            pltpu.VMEM((2 * KV_LATENT, SUBS * ATT_HEADS * Q_BLOCK), BF16),
            pltpu.VMEM((SUBS, seq, Q_BLOCK), I32),
            pltpu.VMEM((1, Q_BLOCK), I32),
            pltpu.VMEM((seq, PAIR_LANES), F32),
            pltpu.VMEM((seq, PAIR_LANES), F32),
            pltpu.VMEM((seq, PAIR_LANES), BF16),
            pltpu.VMEM((seq, PAIR_LANES), BF16),
            pltpu.VMEM((KV_LATENT, SUBS * ATT_HEADS * Q_BLOCK), BF16),
            pltpu.VMEM((att_dim, rows), F32),
        ],
        compiler_params=pltpu.CompilerParams(
            dimension_semantics=("arbitrary", "arbitrary"), vmem_limit_bytes=VMEM_LIMIT),
        name="dsa_attn",
    )(proj, proj, proj, proj, proj, wukT, wuv, kv_norm, tab)


TAIL = 8
GROUP_LANES = SSM_HEADS_PER_GROUP * SSM_HEAD_DIM


def _expand_matrices():
    e64 = np.zeros((V7X_LANES, SSM_INNER), np.float32)
    e128 = np.zeros((V7X_LANES, SSM_HEADS * SSM_CHUNK), np.float32)
    for h in range(SSM_HEADS):
        for piece in range(3):
            lane = (MISC_DT + h + piece * SSM_HEADS) % V7X_LANES
            e64[lane, h * SSM_HEAD_DIM:(h + 1) * SSM_HEAD_DIM] = 1.0
            e128[lane, h * SSM_CHUNK:(h + 1) * SSM_CHUNK] = 1.0
    return jnp.asarray(e64, BF16), jnp.asarray(e128, BF16)


def _split3(v):
    hi = v.astype(BF16).astype(F32)
    r1 = v - hi
    mid = r1.astype(BF16).astype(F32)
    lo = r1 - mid
    packed = hi + pltpu.roll(mid, SSM_HEADS, axis=1) + pltpu.roll(lo, 2 * SSM_HEADS, axis=1)
    return packed.astype(BF16)


def _conv_silu(u_ref, tail_s, w_ref, b_ref):
    n = SSM_CHUNK
    u = u_ref[...]
    ext = jnp.concatenate([tail_s[...], u], axis=0)
    out = b_ref[...] + w_ref[SSM_CONV - 1:SSM_CONV, :] * u
    for k in range(1, SSM_CONV):
        out = out + w_ref[SSM_CONV - 1 - k:SSM_CONV - k, :] * pltpu.roll(ext, k, axis=0)[TAIL:, :]
    tail_s[...] = u[n - TAIL:n, :]
    return _silu(out)


SSD_CHUNKS_PER_STEP = 2
INPROJ_TM = SSD_CHUNKS_PER_STEP * SSM_CHUNK
INPROJ_TN = V7X_MXU_WIDTH
assert PROJ_COLS % INPROJ_TN == 0 and W_COLS % INPROJ_TN == 0


def _inproj_ssd_kernel(x_ref, g_ref, w_ref, *rest, tiles_per_seq):
    ssd_params = rest[:-9]
    proj_ref, yn_ref, h_s, ssd_in_s, misc_s, xtail_s, btail_s, ctail_s, state_s = rest[-9:]
    i = pl.program_id(0)

    @pl.when(i == 0)
    def _():
        ssd_in_s[...] = jnp.zeros_like(ssd_in_s)
        misc_s[...] = jnp.zeros_like(misc_s)

    @pl.when(lax.rem(jnp.maximum(i - 1, 0), tiles_per_seq) == 0)
    def _():
        state_s[...] = jnp.zeros_like(state_s)
        xtail_s[...] = jnp.zeros_like(xtail_s)
        btail_s[...] = jnp.zeros_like(btail_s)
        ctail_s[...] = jnp.zeros_like(ctail_s)

    def step(slot):
        prev = 1 - slot
        h_s[...] = _rms(x_ref[...], g_ref[...]).astype(BF16)
        todo = list(range(0, W_COLS, INPROJ_TN))

        def project(count=1):
            for _ in range(min(count, len(todo))):
                col = todo.pop(0)
                r = jnp.dot(h_s[...], w_ref[:, col:col + INPROJ_TN], preferred_element_type=F32)
                if col < PROJ_COLS:
                    proj_ref[:, col:col + INPROJ_TN] = r
                    if col <= COL_MISC < col + INPROJ_TN:
                        misc_s[slot] = r[:, COL_MISC - col:COL_MISC - col + V7X_LANES]
                else:
                    ssd_in_s[slot, :, col - PROJ_COLS:col - PROJ_COLS + INPROJ_TN] = r

        for c in range(SSD_CHUNKS_PER_STEP):
            rows = pl.ds(c * SSM_CHUNK, SSM_CHUNK)

            def cols(col0, width):
                return ssd_in_s.at[prev, rows, pl.ds(col0 - PROJ_COLS, width)]

            _ssd_chunk(cols(COL_Z, SSM_INNER), cols(COL_XS, SSM_INNER), cols(COL_BM, SSM_BC),
                       cols(COL_CM, SSM_BC), misc_s.at[prev, rows], *ssd_params, yn_ref.at[rows],
                       xtail_s, btail_s, ctail_s, state_s, between=project)
        project(len(todo))

    for parity in range(2):
        pl.when(lax.rem(i, 2) == parity)(functools.partial(step, parity))


def _ssd_chunk(z_ref, xs_ref, bm_ref, cm_ref, misc_ref, cwx_ref, cwb_ref, cwc_ref, cbx_ref, cbb_ref,
               cbc_ref, dtb_ref, alog_ref, dsk_ref, gn_ref, e64_ref, e128_ref, o_ref,
               xtail_s, btail_s, ctail_s, state_s, between=lambda count=1: None):
    n = SSM_CHUNK
    xs = _conv_silu(xs_ref, xtail_s, cwx_ref, cbx_ref)
    between(2)
    bm = _conv_silu(bm_ref, btail_s, cwb_ref, cbb_ref)
    cm = _conv_silu(cm_ref, ctail_s, cwc_ref, cbc_ref)
    between()

    lane = lax.broadcasted_iota(I32, (n, V7X_LANES), 1)
    row = lax.broadcasted_iota(I32, (n, V7X_LANES), 0)
    on_dt = (lane >= MISC_DT) & (lane < MISC_DT + SSM_HEADS)

    v = misc_ref[...] + dtb_ref[...]
    dt = jnp.maximum(v, 0.0) + jnp.log1p(jnp.exp(-jnp.abs(v)))
    dt = jnp.where(on_dt, dt, 0.0)
    a_neg = -jnp.exp(alog_ref[...])
    acum = dt * a_neg
    k = 1
    while k < n:
        acum = acum + jnp.where(row >= k, pltpu.roll(acum, k, axis=0), 0.0)
        k *= 2

    e64 = e64_ref[...]
    dt_e = jnp.dot(_split3(dt), e64, preferred_element_type=F32)
    acum_p = _split3(acum)
    acum_e = jnp.dot(acum_p, e64, preferred_element_type=F32)
    acum_cb = jnp.dot(acum_p, e128_ref[...], preferred_element_type=F32)
    acum_t = acum.T
    between()

    last = acum_e[n - 1:n, :]
    xdt = xs * dt_e
    xds_b = (xdt * jnp.exp(last - acum_e)).astype(BF16)
    chunk_decay = jnp.exp(last)
    ea_e = jnp.exp(acum_e)
    between()

    tri = row >= lane
    first_head = lane < SSM_HEAD_DIM
    y_groups = []
    for g in range(SSM_GROUPS):
        gl = slice(g * GROUP_LANES, (g + 1) * GROUP_LANES)
        bmg = bm[:, g * SSM_STATE:(g + 1) * SSM_STATE]
        cmg_b = cm[:, g * SSM_STATE:(g + 1) * SSM_STATE].astype(BF16)
        cb = lax.dot_general(cmg_b, bmg.astype(BF16), (((1,), (1,)), ((), ())),
                             preferred_element_type=F32)
        y_pairs = []
        for pr in range(SSM_HEADS_PER_GROUP // 2):
            h0 = g * SSM_HEADS_PER_GROUP + 2 * pr
            mats = []
            for h in (h0, h0 + 1):
                seg = acum_cb[:, h * n:(h + 1) * n] - acum_t[MISC_DT + h:MISC_DT + h + 1, :]
                mats.append((cb * jnp.exp(jnp.where(tri, seg, NEG_INF))).astype(BF16))
            xp = xdt[:, h0 * SSM_HEAD_DIM:(h0 + 2) * SSM_HEAD_DIM]
            rhs = jnp.concatenate([jnp.where(first_head, xp, 0.0), jnp.where(first_head, 0.0, xp)],
                                  axis=0).astype(BF16)
            y_pairs.append(jnp.dot(jnp.concatenate(mats, axis=1), rhs, preferred_element_type=F32))
            if pr % 2 == 1:
                between()
        prev = state_s[g]
        y_off = jnp.dot(cmg_b, prev.astype(BF16), preferred_element_type=F32) * ea_e[:, gl]
        state_s[g] = prev * chunk_decay[:, gl] + jnp.dot(bmg.T.astype(BF16), xds_b[:, gl],
                                                         preferred_element_type=F32)
        y_groups.append(jnp.concatenate(y_pairs, axis=1) + y_off)
        between()

    z = z_ref[...]
    gz = _silu(z)
    outs = []
    for g in range(SSM_GROUPS):
        gl = slice(g * GROUP_LANES, (g + 1) * GROUP_LANES)
        u = (y_groups[g] + dsk_ref[:, gl] * xs[:, gl]) * gz[:, gl]
        outs.append(u * lax.rsqrt(jnp.mean(u * u, axis=-1, keepdims=True) + EPS) * gn_ref[:, gl])
        if g % 2 == 1:
            between()
    o_ref[...] = jnp.concatenate(outs, axis=1).astype(BF16)


def _inproj_ssd(x2d, gain, w_packed, conv_w, conv_b, dt_bias, a_log, d_skip, ssm_norm, seq):
    t = x2d.shape[0]
    n = INPROJ_TM
    n_tiles = t // n
    cwx, cwb, cwc = (conv_w[:, :SSM_INNER], conv_w[:, SSM_INNER:SSM_INNER + SSM_BC],
                     conv_w[:, SSM_INNER + SSM_BC:])
    cbx, cbb, cbc = (conv_b[None, :SSM_INNER], conv_b[None, SSM_INNER:SSM_INNER + SSM_BC],
                     conv_b[None, SSM_INNER + SSM_BC:])
    pad_l, pad_r = MISC_DT, V7X_LANES - MISC_DT - SSM_HEADS
    dtb = jnp.pad(dt_bias, (pad_l, pad_r))[None, :]
    alog = jnp.pad(a_log, (pad_l, pad_r))[None, :]
    dsk = jnp.repeat(d_skip, SSM_HEAD_DIM)[None, :]
    gn = ssm_norm[None, :]
    e64, e128 = _expand_matrices()
    full = lambda shape, **kw: pl.BlockSpec(shape, lambda i: (0,) * len(shape), **kw)
    last = n_tiles - 1
    return pl.pallas_call(
        functools.partial(_inproj_ssd_kernel, tiles_per_seq=seq // n),
        grid=(n_tiles + 1,),
        in_specs=[
            pl.BlockSpec((n, D_MODEL), lambda i: (jnp.minimum(i, last), 0)),
            full((1, D_MODEL)),
            full((D_MODEL, W_COLS), pipeline_mode=pl.Buffered(1)),
            full((SSM_CONV, SSM_INNER)), full((SSM_CONV, SSM_BC)), full((SSM_CONV, SSM_BC)),
            full((1, SSM_INNER)), full((1, SSM_BC)), full((1, SSM_BC)),
            full((1, V7X_LANES)), full((1, V7X_LANES)),
            full((1, SSM_INNER)), full((1, SSM_INNER)),
            full((V7X_LANES, SSM_INNER)), full((V7X_LANES, SSM_HEADS * SSM_CHUNK)),
        ],
        out_specs=[pl.BlockSpec((n, PROJ_COLS), lambda i: (jnp.minimum(i, last), 0)),
                   pl.BlockSpec((n, SSM_INNER), lambda i: (jnp.maximum(i - 1, 0), 0))],
        out_shape=[jax.ShapeDtypeStruct((t, PROJ_COLS), F32),
                   jax.ShapeDtypeStruct((t, SSM_INNER), BF16)],
        scratch_shapes=[
            pltpu.VMEM((n, D_MODEL), BF16),
            pltpu.VMEM((2, n, SSD_COLS), F32),
            pltpu.VMEM((2, n, V7X_LANES), F32),
            pltpu.VMEM((TAIL, SSM_INNER), F32),
            pltpu.VMEM((TAIL, SSM_BC), F32),
            pltpu.VMEM((TAIL, SSM_BC), F32),
            pltpu.VMEM((SSM_GROUPS, SSM_STATE, GROUP_LANES), F32),
        ],
        compiler_params=pltpu.CompilerParams(
            dimension_semantics=("arbitrary",), vmem_limit_bytes=VMEM_LIMIT),
        name="inproj_ssd",
    )(x2d, gain, w_packed, cwx, cwb, cwc, cbx, cbb, cbc, dtb, alog, dsk, gn, e64, e128)


MERGE_TM = 512


def _merge_kernel(att_ref, yn_ref, ga_ref, gs_ref, x_ref, wao_ref, wso_ref, wo_ref, gf_ref,
                  x1_ref, h2_ref):
    y_att = jnp.dot(att_ref[...], wao_ref[...], preferred_element_type=F32)
    y_ssm = jnp.dot(yn_ref[...], wso_ref[...], preferred_element_type=F32)
    merged = _sigmoid(ga_ref[...]) * y_att + _sigmoid(gs_ref[...]) * y_ssm
    x1 = x_ref[...] + jnp.dot(merged.astype(BF16), wo_ref[...], preferred_element_type=F32)
    x1_ref[...] = x1
    h2_ref[...] = _rms(x1, gf_ref[...]).astype(BF16)


def _merge(att, yn, proj, x2d, wao, wso, wo, norm_ffn):
    t = x2d.shape[0]
    tm = MERGE_TM
    full = lambda shape: pl.BlockSpec(shape, lambda i: (0,) * len(shape))
    return pl.pallas_call(
        _merge_kernel,
        grid=(t // tm,),
        in_specs=[
            pl.BlockSpec((tm, D_MODEL), lambda i: (i, 0)),
            pl.BlockSpec((tm, SSM_INNER), lambda i: (i, 0)),
            pl.BlockSpec((tm, D_MODEL), lambda i: (i, COL_GA // D_MODEL)),
            pl.BlockSpec((tm, D_MODEL), lambda i: (i, COL_GS // D_MODEL)),
            pl.BlockSpec((tm, D_MODEL), lambda i: (i, 0)),
            full((D_MODEL, D_MODEL)), full((SSM_INNER, D_MODEL)), full((D_MODEL, D_MODEL)),
            full((1, D_MODEL)),
        ],
        out_specs=[pl.BlockSpec((tm, D_MODEL), lambda i: (i, 0)),
                   pl.BlockSpec((tm, D_MODEL), lambda i: (i, 0))],
        out_shape=[jax.ShapeDtypeStruct((t, D_MODEL), F32),
                   jax.ShapeDtypeStruct((t, D_MODEL), BF16)],
        compiler_params=pltpu.CompilerParams(
            dimension_semantics=("arbitrary",), vmem_limit_bytes=VMEM_LIMIT),
        name="merge",
    )(att, yn, proj, proj, x2d, wao, wso, wo, norm_ffn)


FFN_TM = 512
FFN_HALO = 16
FFN_FC = V7X_MXU_WIDTH


def _ffn_kernel(h_ref, halo_ref, x1_ref, wup_ref, cw_ref, cb_ref, wdn_ref, gf_ref, o_ref,
                acc_s, *, tiles_per_seq):
    keep = jnp.where(pl.program_id(0) % tiles_per_seq == 0, 0.0, 1.0)
    hcat = jnp.concatenate([halo_ref[...], h_ref[...]], axis=0)
    n_chunks = FFN_DIM // FFN_FC

    def up(c, base):
        return jnp.dot(hcat, wup_ref[:, base + c * FFN_FC:base + (c + 1) * FFN_FC],
                       preferred_element_type=F32)

    def conv(u, c, base):
        cols = slice(base + c * FFN_FC, base + (c + 1) * FFN_FC)
        u = jnp.concatenate([u[0:FFN_HALO, :] * keep, u[FFN_HALO:, :]], axis=0)
        out = cb_ref[:, cols] + cw_ref[FFN_CONV - 1:FFN_CONV, cols] * u[FFN_HALO:, :]
        for k in range(1, FFN_CONV):
            out = out + (cw_ref[FFN_CONV - 1 - k:FFN_CONV - k, cols]
                         * pltpu.roll(u, k, axis=0)[FFN_HALO:, :])
        return out

    ug, uv = up(0, 0), up(0, FFN_DIM)
    for c in range(n_chunks):
        last = c + 1 == n_chunks
        ug_next = None if last else up(c + 1, 0)
        gate = conv(ug, c, 0)
        uv_next = None if last else up(c + 1, FFN_DIM)
        val = conv(uv, c, FFN_DIM)
        ug, uv = ug_next, uv_next
        act = (_silu(gate) * val).astype(BF16)
        contrib = jnp.dot(act, wdn_ref[c * FFN_FC:(c + 1) * FFN_FC, :], preferred_element_type=F32)
        if c == 0:
            acc_s[...] = contrib
        else:
            acc_s[...] += contrib
    o_ref[...] = _rms(x1_ref[...] + acc_s[...], gf_ref[...])


def _ffn(h2, x1, wup, conv_w, conv_b, wdn, norm_final, seq):
    t = h2.shape[0]
    tm = FFN_TM
    halo_blocks = tm // FFN_HALO
    single = pl.Buffered(1)
    full = lambda shape, **kw: pl.BlockSpec(shape, lambda i: (0,) * len(shape), **kw)
    return pl.pallas_call(
        functools.partial(_ffn_kernel, tiles_per_seq=seq // tm),
        grid=(t // tm,),
        in_specs=[
            pl.BlockSpec((tm, D_MODEL), lambda i: (i, 0)),
            pl.BlockSpec((FFN_HALO, D_MODEL), lambda i: (jnp.maximum(i * halo_blocks - 1, 0), 0)),
            pl.BlockSpec((tm, D_MODEL), lambda i: (i, 0)),
            full((D_MODEL, 2 * FFN_DIM), pipeline_mode=single),
            full((FFN_CONV, 2 * FFN_DIM)),
            full((1, 2 * FFN_DIM)),
            full((FFN_DIM, D_MODEL), pipeline_mode=single),
            full((1, D_MODEL)),
        ],
        out_specs=pl.BlockSpec((tm, D_MODEL), lambda i: (i, 0)),
        out_shape=jax.ShapeDtypeStruct((t, D_MODEL), F32),
        scratch_shapes=[pltpu.VMEM((tm, D_MODEL), F32)],
        compiler_params=pltpu.CompilerParams(
            dimension_semantics=("arbitrary",), vmem_limit_bytes=VMEM_LIMIT),
        name="ffn",
    )(h2, h2, x1, wup, conv_w, conv_b, wdn, norm_final)


def kernel(x, rel_bias, norm_mix, w_in, kv_norm, w_uk, w_uv, conv_ssm_w, conv_ssm_b, dt_bias, a_log,
           d_skip, ssm_norm, w_att_out, w_ssm_out, w_out, norm_ffn, w_ffn_up, conv_ffn_w, conv_ffn_b,
           w_ffn_down, norm_final):
    batch, seq, _ = x.shape
    assert norm_mix.shape[0] == 1, "single layer"
    assert seq % (TILES_PER_CLASS * Q_BLOCK) == 0 and min(TOPK_MAX, seq // 4) == TOPK_MAX
    x2d = x.reshape(batch * seq, D_MODEL)

    tab = _bias_tiles(rel_bias)
    proj, yn = _inproj_ssd(x2d, norm_mix[0][None, :], _pack_w_in(w_in), conv_ssm_w[0], conv_ssm_b[0],
                           dt_bias[0], a_log[0], d_skip[0], ssm_norm[0], seq)
    att = _dsa_attention(proj, jnp.swapaxes(w_uk[0], 1, 2).astype(BF16), w_uv[0].astype(BF16),
                         kv_norm[0][None, :], tab, batch, seq)
    x1, h2 = _merge(att, yn, proj, x2d, w_att_out[0].astype(BF16), w_ssm_out[0].astype(BF16),
                    w_out[0].astype(BF16), norm_ffn[0][None, :])
    out = _ffn(h2, x1, w_ffn_up[0].astype(BF16), conv_ffn_w[0], conv_ffn_b[0][None, :],
               w_ffn_down[0].astype(BF16), norm_final[None, :], seq)
    return out.reshape(batch, seq, D_MODEL)
```

```python
import functools
import math

import numpy as np
import jax
import jax.numpy as jnp
from jax import lax
from jax.experimental import pallas as pl
from jax.experimental.pallas import tpu as pltpu

F32 = jnp.float32
BF16 = jnp.bfloat16
I32 = jnp.int32

D_MODEL = 1024
ATT_HEADS = 16
ATT_HEAD_DIM = 64
KV_LATENT = 128
IDX_HEADS = 8
IDX_DIM = 64
TOPK_MAX = 256
Q_BLOCK = 128
REL_BUCKETS = 32
REL_MAX_DIST = 128
SSM_INNER = 2 * D_MODEL
SSM_HEAD_DIM = 64
SSM_HEADS = SSM_INNER // SSM_HEAD_DIM
SSM_GROUPS = 4
SSM_HEADS_PER_GROUP = SSM_HEADS // SSM_GROUPS
SSM_STATE = 128
SSM_CONV = 4
SSM_CHUNK = 128
SSM_BC = SSM_GROUPS * SSM_STATE
FFN_DIM = 2816
FFN_CONV = 3
EPS = 1e-6

V7X_LANES = 128
V7X_MXU_WIDTH = 256
V7X_VMEM_BYTES = 64 * 1024 * 1024
VMEM_LIMIT = 56 * 1024 * 1024

INT_MIN = -(2 ** 31)
NEG_INF = float("-inf")
LOG2E = math.log2(math.e)

COL_Q = 0
COL_GA = COL_Q + D_MODEL
COL_GS = COL_GA + D_MODEL
COL_QI = COL_GS + D_MODEL
COL_CR = COL_QI + IDX_HEADS * IDX_DIM
COL_MISC = COL_CR + KV_LATENT
PROJ_COLS = COL_MISC + V7X_LANES
COL_Z = PROJ_COLS
COL_XS = COL_Z + SSM_INNER
COL_BM = COL_XS + SSM_INNER
COL_CM = COL_BM + SSM_BC
W_COLS = COL_CM + SSM_BC
SSD_COLS = W_COLS - PROJ_COLS
MISC_KIDX = 0
MISC_WIDX = IDX_DIM
MISC_DT = IDX_DIM + IDX_HEADS


def _pack_moves():
    sizes = (ATT_HEADS * ATT_HEAD_DIM, KV_LATENT, IDX_HEADS * IDX_DIM, IDX_DIM, IDX_HEADS,
             SSM_INNER, SSM_INNER, SSM_BC, SSM_BC, SSM_HEADS, D_MODEL, D_MODEL)
    src = np.cumsum((0,) + sizes)
    dst = (COL_Q, COL_CR, COL_QI, COL_MISC + MISC_KIDX, COL_MISC + MISC_WIDX, COL_Z, COL_XS, COL_BM,
           COL_CM, COL_MISC + MISC_DT, COL_GA, COL_GS)
    return [(int(s), int(d), int(n)) for s, d, n in zip(src[:-1], dst, sizes)]


PACK_ROWS = 128


def _pack_kernel(wt_ref, o_ref):
    misc = []
    for s, d, n in _pack_moves():
        if COL_MISC <= d < PROJ_COLS:
            assert d == COL_MISC + sum(p.shape[0] for p in misc)
            misc.append(wt_ref[s:s + n, :])
        else:
            o_ref[:, d:d + n] = wt_ref[s:s + n, :].T.astype(BF16)
    used = sum(p.shape[0] for p in misc)
    misc.append(jnp.zeros((V7X_LANES - used, PACK_ROWS), F32))
    o_ref[:, COL_MISC:PROJ_COLS] = jnp.concatenate(misc, axis=0).T.astype(BF16)


def _pack_w_in(w):
    _, k, cols = w.shape
    wt = jnp.swapaxes(w, 1, 2)
    assert all(s % 8 == 0 for s, _, _ in _pack_moves())
    return pl.pallas_call(
        _pack_kernel,
        grid=(k // PACK_ROWS,),
        in_specs=[pl.BlockSpec((None, cols, PACK_ROWS), lambda i: (0, 0, i))],
        out_specs=pl.BlockSpec((PACK_ROWS, W_COLS), lambda i: (i, 0)),
        out_shape=jax.ShapeDtypeStruct((k, W_COLS), BF16),
        name="pack_w_in",
    )(wt)


def _rms(x, gain):
    return x * lax.rsqrt(jnp.mean(x * x, axis=-1, keepdims=True) + EPS) * gain


def _sigmoid(x):
    return 0.5 + 0.5 * jnp.tanh(0.5 * x)


def _silu(x):
    h = 0.5 * x
    return h + h * jnp.tanh(h)


def _bias_tiles_kernel(rb_ref, o_ref):
    h = pl.program_id(0)
    far = rb_ref[REL_BUCKETS - 1, h]
    key = lax.broadcasted_iota(I32, (Q_BLOCK, Q_BLOCK), 0)
    qry = lax.broadcasted_iota(I32, (Q_BLOCK, Q_BLOCK), 1)
    max_exact = REL_BUCKETS // 2
    for delta in range(2):
        n = jnp.maximum(delta * Q_BLOCK + qry - key, 0)
        nf = jnp.maximum(n, 1).astype(F32)
        large = max_exact + (jnp.log(nf / max_exact) / math.log(REL_MAX_DIST / max_exact)
                             * (REL_BUCKETS - max_exact)).astype(I32)
        large = jnp.minimum(large, REL_BUCKETS - 1)
        bucket = jnp.where(n < max_exact, n, large)
        val = jnp.zeros((Q_BLOCK, Q_BLOCK), F32)
        for b in range(REL_BUCKETS):
            val = jnp.where(bucket == b, rb_ref[b, h] - far, val)
        o_ref[delta] = val * LOG2E


def _bias_tiles(rel_bias):
    return pl.pallas_call(
        _bias_tiles_kernel,
        grid=(ATT_HEADS,),
        in_specs=[pl.BlockSpec(memory_space=pltpu.SMEM)],
        out_specs=pl.BlockSpec((2, Q_BLOCK, Q_BLOCK), lambda h: (0, 0, h)),
        out_shape=jax.ShapeDtypeStruct((2, Q_BLOCK, ATT_HEADS * Q_BLOCK), F32),
        name="bias_tiles",
    )(rel_bias)


TILES_PER_CLASS = 2
PAD_TILES = TILES_PER_CLASS - 1
PAD_ROWS = PAD_TILES * Q_BLOCK
ROW_CHUNK = 256
COUNT_ROWS = 64
PAIR_LANES = 2 * Q_BLOCK
assert PAIR_LANES == V7X_MXU_WIDTH and (TILES_PER_CLASS * Q_BLOCK) % ROW_CHUNK == 0
ONES_ROWS = 16
PV_ROWS = KV_LATENT + ONES_ROWS
NEG_BIG = -1e30


SUBS = TILES_PER_CLASS


def _aligned(x, m):
    return x if isinstance(x, int) else pl.multiple_of(x, m)


def _attn_step(k, qiT, wT, tab_ref, cm_s, ckvT_s, kidx_s, qT_s, key_s, j_s, l0_s, l1_s, p0_s, p1_s,
               ot_s):
    n_tiles = TILES_PER_CLASS * (k + 1)
    nkp = n_tiles * Q_BLOCK
    n_chunks = nkp // ROW_CHUNK
    n_sel = float(TOPK_MAX)
    lane = lax.broadcasted_iota(I32, (ROW_CHUNK, Q_BLOCK), 1)
    row_in_chunk = lax.broadcasted_iota(I32, (ROW_CHUNK, Q_BLOCK), 0)

    for sub in range(SUBS):
        start = sub * Q_BLOCK
        key0 = (sub - PAD_TILES) * Q_BLOCK
        q_pos = (SUBS * k + sub) * Q_BLOCK + lane
        ql = slice(sub * Q_BLOCK, (sub + 1) * Q_BLOCK)
        for rc in range(n_chunks):
            rows = slice(rc * ROW_CHUNK, (rc + 1) * ROW_CHUNK)
            kc = kidx_s[start + rc * ROW_CHUNK:start + (rc + 1) * ROW_CHUNK, :]
            acc = jnp.zeros((ROW_CHUNK, Q_BLOCK), F32)
            for hp in range(IDX_HEADS // 2):
                rhs = jnp.concatenate(
                    [qiT[(2 * hp) * IDX_DIM:(2 * hp + 1) * IDX_DIM, ql],
                     qiT[(2 * hp + 1) * IDX_DIM:(2 * hp + 2) * IDX_DIM, ql]], axis=1)
                lg = jnp.dot(kc, rhs, preferred_element_type=F32)
                w0 = wT[MISC_WIDX + 2 * hp:MISC_WIDX + 2 * hp + 1, ql]
                w1 = wT[MISC_WIDX + 2 * hp + 1:MISC_WIDX + 2 * hp + 2, ql]
                acc = (acc + jnp.maximum(lg[:, :Q_BLOCK], 0.0) * w0
                       + jnp.maximum(lg[:, Q_BLOCK:], 0.0) * w1)
            bits = pltpu.bitcast(acc, I32)
            bits = jnp.where(bits == INT_MIN, 0, bits)
            key = jnp.where(bits < 0, bits ^ 0x7FFFFFFF, bits)
            key_pos = key0 + rc * ROW_CHUNK + row_in_chunk
            if key0 + rc * ROW_CHUNK < 0:
                key = jnp.where(key_pos >= 0, key, INT_MIN)
            if rc == n_chunks - 1:
                key = jnp.where(key_pos <= q_pos, key, INT_MIN)
            key_s[sub, rows, :] = key

    def count_ones(ones):
        part = jnp.sum(ones.reshape(nkp // COUNT_ROWS, COUNT_ROWS, Q_BLOCK), axis=0)
        return jnp.sum(part, axis=0, keepdims=True)

    def count(pred):
        return count_ones(jnp.where(pred, 1.0, 0.0))

    def value_step(bit, thrs):
        flip = jnp.left_shift(jnp.int32(1), 31 - bit)
        cands = [thr ^ flip for thr in thrs]
        hits = [count(key_s[sub, 0:nkp, :] >= cands[sub]) >= n_sel for sub in range(SUBS)]
        return tuple(jnp.where(hits[sub], cands[sub], thrs[sub]) for sub in range(SUBS))

    thrs = lax.fori_loop(0, 32, value_step,
                         tuple(jnp.full((1, Q_BLOCK), INT_MIN, I32) for _ in range(SUBS)))

    row_in_tile = lax.broadcasted_iota(I32, (Q_BLOCK, Q_BLOCK), 0)
    for sub in range(SUBS):
        thr = thrs[sub]
        keys = key_s[sub, 0:nkp, :]
        need = n_sel - count(keys > thr)
        split = count(keys == thr) > need

        j_s[...] = jnp.full((1, Q_BLOCK), 2 * nkp, I32)

        @pl.when(jnp.max(jnp.where(split, 1.0, 0.0)) > 0.5)
        def _(sub=sub, thr=thr, need=need):
            row = lax.broadcasted_iota(I32, (nkp, Q_BLOCK), 0)
            nbits = (2 * nkp - 1).bit_length()

            def index_step(bit, jmax):
                cand = jmax | jnp.left_shift(jnp.int32(1), nbits - 1 - bit)
                f = count_ones(jnp.where(key_s[sub, 0:nkp, :] == thr,
                                         jnp.where(row < cand, 1.0, 0.0), 0.0))
                return jnp.where(f <= need, cand, jmax)

            j_s[...] = lax.fori_loop(0, nbits, index_step, jnp.zeros((1, Q_BLOCK), I32))

        jmax = j_s[...]
        for t in range(n_tiles):
            kt = key_s[sub, t * Q_BLOCK:(t + 1) * Q_BLOCK, :]
            row = t * Q_BLOCK + row_in_tile
            add = jnp.where(kt > thr, 0.0,
                            jnp.where(kt == thr, jnp.where(row < jmax, 0.0, NEG_BIG), NEG_BIG))
            add = jnp.where(kt == INT_MIN, NEG_BIG, add)
            cm_s[sub, (sub + t) * Q_BLOCK:(sub + t + 1) * Q_BLOCK,
                 KV_LATENT:2 * KV_LATENT] = add.astype(BF16)

    tiles_per_chunk = ROW_CHUNK // Q_BLOCK
    n_pairs = ATT_HEADS // 2
    n_units = SUBS * n_pairs
    pair_bits = n_pairs.bit_length() - 1
    assert n_pairs == 1 << pair_bits

    def unit(logits=None, exps=None, values=None):
        if logits is not None:
            u_a, lbuf_a = logits
            sub_a = u_a >> pair_bits
            start_a = sub_a * Q_BLOCK
            off_t = _aligned((u_a & (n_pairs - 1)) * PAIR_LANES, PAIR_LANES)
            rhs = qT_s[:, pl.ds(_aligned(u_a * PAIR_LANES, PAIR_LANES), PAIR_LANES)]
            m8 = jnp.full((8, PAIR_LANES), NEG_INF, F32)
        if exps is not None:
            lbuf_b, pbuf_b, m_b = exps
        if values is not None:
            u_c, pbuf_c = values
            start_c = (u_c >> pair_bits) * Q_BLOCK
            acc = None
        for rc in range(n_chunks):
            rows = slice(rc * ROW_CHUNK, (rc + 1) * ROW_CHUNK)
            if logits is not None:
                win = pl.ds(_aligned(start_a + rc * ROW_CHUNK, Q_BLOCK), ROW_CHUNK)
                lc = jnp.dot(cm_s[sub_a, win, :], rhs, preferred_element_type=F32)
                for tt in range(tiles_per_chunk):
                    t = rc * tiles_per_chunk + tt
                    lt = lc[tt * Q_BLOCK:(tt + 1) * Q_BLOCK, :]
                    if t >= n_tiles - 2:
                        lt = lt + tab_ref[n_tiles - 1 - t, :, pl.ds(off_t, PAIR_LANES)]
                    lbuf_a[t * Q_BLOCK:(t + 1) * Q_BLOCK, :] = lt
                    m8 = jnp.maximum(m8, jnp.max(lt.reshape(Q_BLOCK // 8, 8, PAIR_LANES), axis=0))
            if exps is not None:
                pbuf_b[rows, :] = jnp.exp2(lbuf_b[rows, :] - m_b).astype(BF16)
            if values is not None:
                win = pl.ds(_aligned(start_c + rc * ROW_CHUNK, Q_BLOCK), ROW_CHUNK)
                part = jnp.dot(ckvT_s[:, win], pbuf_c[rows, :], preferred_element_type=F32)
                acc = part if acc is None else acc + part
        if values is not None:
            ot_s[:, pl.ds(_aligned(u_c * PAIR_LANES, PAIR_LANES), PAIR_LANES)] = (
                acc[0:KV_LATENT, :] / acc[KV_LATENT:KV_LATENT + 1, :]).astype(BF16)
        if logits is not None:
            return jnp.max(m8, axis=0, keepdims=True)

    def two_units(jj, m_odd):
        u = 2 * jj
        m_even = unit(logits=(u + 2, l0_s), exps=(l1_s, p1_s, m_odd), values=(u, p0_s))
        return unit(logits=(u + 3, l1_s), exps=(l0_s, p0_s, m_even), values=(u + 1, p1_s))

    m_even = unit(logits=(0, l0_s))
    m_odd = unit(logits=(1, l1_s), exps=(l0_s, p0_s, m_even))
    m_odd = lax.fori_loop(0, n_units // 2 - 1, two_units, m_odd)
    unit(exps=(l1_s, p1_s, m_odd), values=(n_units - 2, p0_s))
    unit(values=(n_units - 1, p1_s))


def _attn_kernel(q_ref, qi_ref, mq_ref, cr_ref, mk_ref, wukT_ref, wuv_ref, kvn_ref, tab_ref, o_ref,
                 cm_all, ckvT_all, kidx_all, qT_s, key_s, j_s, l0_s, l1_s, p0_s, p1_s, ot_s, attT_s):
    step = pl.program_id(0)
    b = pl.program_id(1)
    seq = cr_ref.shape[0]
    cm_s, ckvT_s, kidx_s = cm_all.at[b], ckvT_all.at[b], kidx_all.at[b]

    @pl.when(step == 0)
    def _():
        c = _rms(cr_ref[...], kvn_ref[...])
        for sub in range(SUBS):
            cm_s[sub, 0:PAD_ROWS, :] = jnp.zeros((PAD_ROWS, 2 * KV_LATENT), BF16)
            cm_s[sub, PAD_ROWS:, 0:KV_LATENT] = c.astype(BF16)
        ckvT_s[0:KV_LATENT, 0:PAD_ROWS] = jnp.zeros((KV_LATENT, PAD_ROWS), BF16)
        ckvT_s[0:KV_LATENT, PAD_ROWS:] = c.T.astype(BF16)
        ckvT_s[KV_LATENT:, :] = jnp.ones((ONES_ROWS, seq + PAD_ROWS), BF16)
        kidx_s[0:PAD_ROWS, :] = jnp.zeros((PAD_ROWS, IDX_DIM), BF16)
        kidx_s[PAD_ROWS:, :] = mk_ref[:, MISC_KIDX:MISC_KIDX + IDX_DIM].astype(BF16)
        eye = jnp.where(lax.broadcasted_iota(I32, (Q_BLOCK, Q_BLOCK), 0)
                        == lax.broadcasted_iota(I32, (Q_BLOCK, Q_BLOCK), 1), 1.0, 0.0).astype(BF16)
        for blk in range(SUBS * ATT_HEADS):
            qT_s[KV_LATENT:, blk * Q_BLOCK:(blk + 1) * Q_BLOCK] = eye

    scale = ATT_HEAD_DIM ** -0.5 * LOG2E
    qb = q_ref[...].astype(BF16)
    for sub in range(SUBS):
        for h in range(ATT_HEADS):
            qh = qb[sub * Q_BLOCK:(sub + 1) * Q_BLOCK, h * ATT_HEAD_DIM:(h + 1) * ATT_HEAD_DIM]
            qt = lax.dot_general(wukT_ref[h], qh, (((1,), (1,)), ((), ())),
                                 preferred_element_type=F32)
            blk = sub * ATT_HEADS + h
            qT_s[0:KV_LATENT, blk * Q_BLOCK:(blk + 1) * Q_BLOCK] = (qt * scale).astype(BF16)

    qiT = qi_ref[...].T.astype(BF16)
    wT = mq_ref[...].T

    for k in range(seq // Q_BLOCK // SUBS):
        @pl.when(step == k)
        def _(k=k):
            _attn_step(k, qiT, wT, tab_ref, cm_s, ckvT_s, kidx_s, qT_s, key_s, j_s, l0_s, l1_s,
                       p0_s, p1_s, ot_s)

    for sub in range(SUBS):
        for h in range(ATT_HEADS):
            blk = sub * ATT_HEADS + h
            attT_s[h * ATT_HEAD_DIM:(h + 1) * ATT_HEAD_DIM, sub * Q_BLOCK:(sub + 1) * Q_BLOCK] = (
                jnp.dot(wuv_ref[h], ot_s[:, blk * Q_BLOCK:(blk + 1) * Q_BLOCK],
                        preferred_element_type=F32))
    o_ref[...] = attT_s[...].T.astype(BF16)


def _dsa_attention(proj, wukT, wuv, kv_norm, tab, batch, seq):
    rows = SUBS * Q_BLOCK
    nsteps = seq // rows
    att_dim = ATT_HEADS * ATT_HEAD_DIM
    return pl.pallas_call(
        _attn_kernel,
        grid=(nsteps, batch),
        in_specs=[
            pl.BlockSpec((rows, att_dim), lambda i, b: (b * nsteps + i, COL_Q // att_dim)),
            pl.BlockSpec((rows, IDX_HEADS * IDX_DIM),
                         lambda i, b: (b * nsteps + i, COL_QI // (IDX_HEADS * IDX_DIM))),
            pl.BlockSpec((rows, V7X_LANES), lambda i, b: (b * nsteps + i, COL_MISC // V7X_LANES)),
            pl.BlockSpec((seq, KV_LATENT), lambda i, b: (b, COL_CR // KV_LATENT)),
            pl.BlockSpec((seq, V7X_LANES), lambda i, b: (b, COL_MISC // V7X_LANES)),
            pl.BlockSpec((ATT_HEADS, KV_LATENT, ATT_HEAD_DIM), lambda i, b: (0, 0, 0)),
            pl.BlockSpec((ATT_HEADS, ATT_HEAD_DIM, KV_LATENT), lambda i, b: (0, 0, 0)),
            pl.BlockSpec((1, KV_LATENT), lambda i, b: (0, 0)),
            pl.BlockSpec((2, Q_BLOCK, ATT_HEADS * Q_BLOCK), lambda i, b: (0, 0, 0)),
        ],
        out_specs=pl.BlockSpec((rows, att_dim), lambda i, b: (b * nsteps + i, 0)),
        out_shape=jax.ShapeDtypeStruct((batch * seq, att_dim), BF16),
        scratch_shapes=[
            pltpu.VMEM((batch, SUBS, seq + PAD_ROWS, 2 * KV_LATENT), BF16),
            pltpu.VMEM((batch, PV_ROWS, seq + PAD_ROWS), BF16),
            pltpu.VMEM((batch, seq + PAD_ROWS, IDX_DIM), BF16),
            pltpu.VMEM((2 * KV_LATENT, SUBS * ATT_HEADS * Q_BLOCK), BF16),
            pltpu.VMEM((SUBS, seq, Q_BLOCK), I32),
            pltpu.VMEM((1, Q_BLOCK), I32),
            pltpu.VMEM((seq, PAIR_LANES), F32),
            pltpu.VMEM((seq, PAIR_LANES), F32),
            pltpu.VMEM((seq, PAIR_LANES), BF16),
            pltpu.VMEM((seq, PAIR_LANES), BF16),
            pltpu.VMEM((KV_LATENT, SUBS * ATT_HEADS * Q_BLOCK), BF16),
            pltpu.VMEM((att_dim, rows), F32),
        ],
        compiler_params=pltpu.CompilerParams(
            dimension_semantics=("arbitrary", "arbitrary"), vmem_limit_bytes=VMEM_LIMIT),
        name="dsa_attn",
    )(proj, proj, proj, proj, proj, wukT, wuv, kv_norm, tab)


TAIL = 8
GROUP_LANES = SSM_HEADS_PER_GROUP * SSM_HEAD_DIM


def _expand_matrices():
    e64 = np.zeros((V7X_LANES, SSM_INNER), np.float32)
    e128 = np.zeros((V7X_LANES, SSM_HEADS * SSM_CHUNK), np.float32)
    for h in range(SSM_HEADS):
        for piece in range(3):
            lane = (MISC_DT + h + piece * SSM_HEADS) % V7X_LANES
            e64[lane, h * SSM_HEAD_DIM:(h + 1) * SSM_HEAD_DIM] = 1.0
            e128[lane, h * SSM_CHUNK:(h + 1) * SSM_CHUNK] = 1.0
    return jnp.asarray(e64, BF16), jnp.asarray(e128, BF16)


def _split3(v):
    hi = v.astype(BF16).astype(F32)
    r1 = v - hi
    mid = r1.astype(BF16).astype(F32)
    lo = r1 - mid
    packed = hi + pltpu.roll(mid, SSM_HEADS, axis=1) + pltpu.roll(lo, 2 * SSM_HEADS, axis=1)
    return packed.astype(BF16)


def _conv_silu(u_ref, tail_s, w_ref, b_ref):
    n = SSM_CHUNK
    u = u_ref[...]
    ext = jnp.concatenate([tail_s[...], u], axis=0)
    out = b_ref[...] + w_ref[SSM_CONV - 1:SSM_CONV, :] * u
    for k in range(1, SSM_CONV):
        out = out + w_ref[SSM_CONV - 1 - k:SSM_CONV - k, :] * pltpu.roll(ext, k, axis=0)[TAIL:, :]
    tail_s[...] = u[n - TAIL:n, :]
    return _silu(out)


SSD_CHUNKS_PER_STEP = 2
INPROJ_TM = SSD_CHUNKS_PER_STEP * SSM_CHUNK
INPROJ_TN = V7X_MXU_WIDTH
assert PROJ_COLS % INPROJ_TN == 0 and W_COLS % INPROJ_TN == 0


def _inproj_ssd_kernel(x_ref, g_ref, w_ref, *rest, tiles_per_seq):
    ssd_params = rest[:-9]
    proj_ref, yn_ref, h_s, ssd_in_s, misc_s, xtail_s, btail_s, ctail_s, state_s = rest[-9:]
    i = pl.program_id(0)

    @pl.when(i == 0)
    def _():
        ssd_in_s[...] = jnp.zeros_like(ssd_in_s)
        misc_s[...] = jnp.zeros_like(misc_s)

    @pl.when(lax.rem(jnp.maximum(i - 1, 0), tiles_per_seq) == 0)
    def _():
        state_s[...] = jnp.zeros_like(state_s)
        xtail_s[...] = jnp.zeros_like(xtail_s)
        btail_s[...] = jnp.zeros_like(btail_s)
        ctail_s[...] = jnp.zeros_like(ctail_s)

    def step(slot):
        prev = 1 - slot
        h_s[...] = _rms(x_ref[...], g_ref[...]).astype(BF16)
        todo = list(range(0, W_COLS, INPROJ_TN))

        def project(count=1):
            for _ in range(min(count, len(todo))):
                col = todo.pop(0)
                r = jnp.dot(h_s[...], w_ref[:, col:col + INPROJ_TN], preferred_element_type=F32)
                if col < PROJ_COLS:
                    proj_ref[:, col:col + INPROJ_TN] = r
                    if col <= COL_MISC < col + INPROJ_TN:
                        misc_s[slot] = r[:, COL_MISC - col:COL_MISC - col + V7X_LANES]
                else:
                    ssd_in_s[slot, :, col - PROJ_COLS:col - PROJ_COLS + INPROJ_TN] = r

        for c in range(SSD_CHUNKS_PER_STEP):
            rows = pl.ds(c * SSM_CHUNK, SSM_CHUNK)

            def cols(col0, width):
                return ssd_in_s.at[prev, rows, pl.ds(col0 - PROJ_COLS, width)]

            _ssd_chunk(cols(COL_Z, SSM_INNER), cols(COL_XS, SSM_INNER), cols(COL_BM, SSM_BC),
                       cols(COL_CM, SSM_BC), misc_s.at[prev, rows], *ssd_params, yn_ref.at[rows],
                       xtail_s, btail_s, ctail_s, state_s, between=project)
        project(len(todo))

    for parity in range(2):
        pl.when(lax.rem(i, 2) == parity)(functools.partial(step, parity))


def _ssd_chunk(z_ref, xs_ref, bm_ref, cm_ref, misc_ref, cwx_ref, cwb_ref, cwc_ref, cbx_ref, cbb_ref,
               cbc_ref, dtb_ref, alog_ref, dsk_ref, gn_ref, e64_ref, e128_ref, o_ref,
               xtail_s, btail_s, ctail_s, state_s, between=lambda count=1: None):
    n = SSM_CHUNK
    xs = _conv_silu(xs_ref, xtail_s, cwx_ref, cbx_ref)
    between(2)
    bm = _conv_silu(bm_ref, btail_s, cwb_ref, cbb_ref)
    cm = _conv_silu(cm_ref, ctail_s, cwc_ref, cbc_ref)
    between()

    lane = lax.broadcasted_iota(I32, (n, V7X_LANES), 1)
    row = lax.broadcasted_iota(I32, (n, V7X_LANES), 0)
    on_dt = (lane >= MISC_DT) & (lane < MISC_DT + SSM_HEADS)

    v = misc_ref[...] + dtb_ref[...]
    dt = jnp.maximum(v, 0.0) + jnp.log1p(jnp.exp(-jnp.abs(v)))
    dt = jnp.where(on_dt, dt, 0.0)
    a_neg = -jnp.exp(alog_ref[...])
    acum = dt * a_neg
    k = 1
    while k < n:
        acum = acum + jnp.where(row >= k, pltpu.roll(acum, k, axis=0), 0.0)
        k *= 2

    e64 = e64_ref[...]
    dt_e = jnp.dot(_split3(dt), e64, preferred_element_type=F32)
    acum_p = _split3(acum)
    acum_e = jnp.dot(acum_p, e64, preferred_element_type=F32)
    acum_cb = jnp.dot(acum_p, e128_ref[...], preferred_element_type=F32)
    acum_t = acum.T
    between()

    last = acum_e[n - 1:n, :]
    xdt = xs * dt_e
    xds_b = (xdt * jnp.exp(last - acum_e)).astype(BF16)
    chunk_decay = jnp.exp(last)
    ea_e = jnp.exp(acum_e)
    between()

    tri = row >= lane
    first_head = lane < SSM_HEAD_DIM
    y_groups = []
    for g in range(SSM_GROUPS):
        gl = slice(g * GROUP_LANES, (g + 1) * GROUP_LANES)
        bmg = bm[:, g * SSM_STATE:(g + 1) * SSM_STATE]
        cmg_b = cm[:, g * SSM_STATE:(g + 1) * SSM_STATE].astype(BF16)
        cb = lax.dot_general(cmg_b, bmg.astype(BF16), (((1,), (1,)), ((), ())),
                             preferred_element_type=F32)
        y_pairs = []
        for pr in range(SSM_HEADS_PER_GROUP // 2):
            h0 = g * SSM_HEADS_PER_GROUP + 2 * pr
            mats = []
            for h in (h0, h0 + 1):
                seg = acum_cb[:, h * n:(h + 1) * n] - acum_t[MISC_DT + h:MISC_DT + h + 1, :]
                mats.append((cb * jnp.exp(jnp.where(tri, seg, NEG_INF))).astype(BF16))
            xp = xdt[:, h0 * SSM_HEAD_DIM:(h0 + 2) * SSM_HEAD_DIM]
            rhs = jnp.concatenate([jnp.where(first_head, xp, 0.0), jnp.where(first_head, 0.0, xp)],
                                  axis=0).astype(BF16)
            y_pairs.append(jnp.dot(jnp.concatenate(mats, axis=1), rhs, preferred_element_type=F32))
            if pr % 2 == 1:
                between()
        prev = state_s[g]
        y_off = jnp.dot(cmg_b, prev.astype(BF16), preferred_element_type=F32) * ea_e[:, gl]
        state_s[g] = prev * chunk_decay[:, gl] + jnp.dot(bmg.T.astype(BF16), xds_b[:, gl],
                                                         preferred_element_type=F32)
        y_groups.append(jnp.concatenate(y_pairs, axis=1) + y_off)
        between()

    z = z_ref[...]
    gz = _silu(z)
    outs = []
    for g in range(SSM_GROUPS):
        gl = slice(g * GROUP_LANES, (g + 1) * GROUP_LANES)
        u = (y_groups[g] + dsk_ref[:, gl] * xs[:, gl]) * gz[:, gl]
        outs.append(u * lax.rsqrt(jnp.mean(u * u, axis=-1, keepdims=True) + EPS) * gn_ref[:, gl])
        if g % 2 == 1:
            between()
    o_ref[...] = jnp.concatenate(outs, axis=1).astype(BF16)


def _inproj_ssd(x2d, gain, w_packed, conv_w, conv_b, dt_bias, a_log, d_skip, ssm_norm, seq):
    t = x2d.shape[0]
    n = INPROJ_TM
    n_tiles = t // n
    cwx, cwb, cwc = (conv_w[:, :SSM_INNER], conv_w[:, SSM_INNER:SSM_INNER + SSM_BC],
                     conv_w[:, SSM_INNER + SSM_BC:])
    cbx, cbb, cbc = (conv_b[None, :SSM_INNER], conv_b[None, SSM_INNER:SSM_INNER + SSM_BC],
                     conv_b[None, SSM_INNER + SSM_BC:])
    pad_l, pad_r = MISC_DT, V7X_LANES - MISC_DT - SSM_HEADS
    dtb = jnp.pad(dt_bias, (pad_l, pad_r))[None, :]
    alog = jnp.pad(a_log, (pad_l, pad_r))[None, :]
    dsk = jnp.repeat(d_skip, SSM_HEAD_DIM)[None, :]
    gn = ssm_norm[None, :]
    e64, e128 = _expand_matrices()
    full = lambda shape, **kw: pl.BlockSpec(shape, lambda i: (0,) * len(shape), **kw)
    last = n_tiles - 1
    return pl.pallas_call(
        functools.partial(_inproj_ssd_kernel, tiles_per_seq=seq // n),
        grid=(n_tiles + 1,),
        in_specs=[
            pl.BlockSpec((n, D_MODEL), lambda i: (jnp.minimum(i, last), 0)),
            full((1, D_MODEL)),
            full((D_MODEL, W_COLS), pipeline_mode=pl.Buffered(1)),
            full((SSM_CONV, SSM_INNER)), full((SSM_CONV, SSM_BC)), full((SSM_CONV, SSM_BC)),
            full((1, SSM_INNER)), full((1, SSM_BC)), full((1, SSM_BC)),
            full((1, V7X_LANES)), full((1, V7X_LANES)),
            full((1, SSM_INNER)), full((1, SSM_INNER)),
            full((V7X_LANES, SSM_INNER)), full((V7X_LANES, SSM_HEADS * SSM_CHUNK)),
        ],
        out_specs=[pl.BlockSpec((n, PROJ_COLS), lambda i: (jnp.minimum(i, last), 0)),
                   pl.BlockSpec((n, SSM_INNER), lambda i: (jnp.maximum(i - 1, 0), 0))],
        out_shape=[jax.ShapeDtypeStruct((t, PROJ_COLS), F32),
                   jax.ShapeDtypeStruct((t, SSM_INNER), BF16)],
        scratch_shapes=[
            pltpu.VMEM((n, D_MODEL), BF16),
            pltpu.VMEM((2, n, SSD_COLS), F32),
            pltpu.VMEM((2, n, V7X_LANES), F32),
            pltpu.VMEM((TAIL, SSM_INNER), F32),
            pltpu.VMEM((TAIL, SSM_BC), F32),
            pltpu.VMEM((TAIL, SSM_BC), F32),
            pltpu.VMEM((SSM_GROUPS, SSM_STATE, GROUP_LANES), F32),
        ],
        compiler_params=pltpu.CompilerParams(
            dimension_semantics=("arbitrary",), vmem_limit_bytes=VMEM_LIMIT),
        name="inproj_ssd",
    )(x2d, gain, w_packed, cwx, cwb, cwc, cbx, cbb, cbc, dtb, alog, dsk, gn, e64, e128)


MERGE_TM = 512


def _merge_kernel(att_ref, yn_ref, ga_ref, gs_ref, x_ref, wao_ref, wso_ref, wo_ref, gf_ref,
                  x1_ref, h2_ref):
    y_att = jnp.dot(att_ref[...], wao_ref[...], preferred_element_type=F32)
    y_ssm = jnp.dot(yn_ref[...], wso_ref[...], preferred_element_type=F32)
    merged = _sigmoid(ga_ref[...]) * y_att + _sigmoid(gs_ref[...]) * y_ssm
    x1 = x_ref[...] + jnp.dot(merged.astype(BF16), wo_ref[...], preferred_element_type=F32)
    x1_ref[...] = x1
    h2_ref[...] = _rms(x1, gf_ref[...]).astype(BF16)


def _merge(att, yn, proj, x2d, wao, wso, wo, norm_ffn):
    t = x2d.shape[0]
    tm = MERGE_TM
    full = lambda shape: pl.BlockSpec(shape, lambda i: (0,) * len(shape))
    return pl.pallas_call(
        _merge_kernel,
        grid=(t // tm,),
        in_specs=[
            pl.BlockSpec((tm, D_MODEL), lambda i: (i, 0)),
            pl.BlockSpec((tm, SSM_INNER), lambda i: (i, 0)),
            pl.BlockSpec((tm, D_MODEL), lambda i: (i, COL_GA // D_MODEL)),
            pl.BlockSpec((tm, D_MODEL), lambda i: (i, COL_GS // D_MODEL)),
            pl.BlockSpec((tm, D_MODEL), lambda i: (i, 0)),
            full((D_MODEL, D_MODEL)), full((SSM_INNER, D_MODEL)), full((D_MODEL, D_MODEL)),
            full((1, D_MODEL)),
        ],
        out_specs=[pl.BlockSpec((tm, D_MODEL), lambda i: (i, 0)),
                   pl.BlockSpec((tm, D_MODEL), lambda i: (i, 0))],
        out_shape=[jax.ShapeDtypeStruct((t, D_MODEL), F32),
                   jax.ShapeDtypeStruct((t, D_MODEL), BF16)],
        compiler_params=pltpu.CompilerParams(
            dimension_semantics=("arbitrary",), vmem_limit_bytes=VMEM_LIMIT),
        name="merge",
    )(att, yn, proj, proj, x2d, wao, wso, wo, norm_ffn)


FFN_TM = 512
FFN_HALO = 16
FFN_FC = V7X_MXU_WIDTH


def _ffn_kernel(h_ref, halo_ref, x1_ref, wup_ref, cw_ref, cb_ref, wdn_ref, gf_ref, o_ref,
                acc_s, *, tiles_per_seq):
    keep = jnp.where(pl.program_id(0) % tiles_per_seq == 0, 0.0, 1.0)
    hcat = jnp.concatenate([halo_ref[...], h_ref[...]], axis=0)
    n_chunks = FFN_DIM // FFN_FC

    def up(c, base):
        return jnp.dot(hcat, wup_ref[:, base + c * FFN_FC:base + (c + 1) * FFN_FC],
                       preferred_element_type=F32)

    def conv(u, c, base):
        cols = slice(base + c * FFN_FC, base + (c + 1) * FFN_FC)
        u = jnp.concatenate([u[0:FFN_HALO, :] * keep, u[FFN_HALO:, :]], axis=0)
        out = cb_ref[:, cols] + cw_ref[FFN_CONV - 1:FFN_CONV, cols] * u[FFN_HALO:, :]
        for k in range(1, FFN_CONV):
            out = out + (cw_ref[FFN_CONV - 1 - k:FFN_CONV - k, cols]
                         * pltpu.roll(u, k, axis=0)[FFN_HALO:, :])
        return out

    ug, uv = up(0, 0), up(0, FFN_DIM)
    for c in range(n_chunks):
        last = c + 1 == n_chunks
        ug_next = None if last else up(c + 1, 0)
        gate = conv(ug, c, 0)
        uv_next = None if last else up(c + 1, FFN_DIM)
        val = conv(uv, c, FFN_DIM)
        ug, uv = ug_next, uv_next
        act = (_silu(gate) * val).astype(BF16)
        contrib = jnp.dot(act, wdn_ref[c * FFN_FC:(c + 1) * FFN_FC, :], preferred_element_type=F32)
        if c == 0:
            acc_s[...] = contrib
        else:
            acc_s[...] += contrib
    o_ref[...] = _rms(x1_ref[...] + acc_s[...], gf_ref[...])


def _ffn(h2, x1, wup, conv_w, conv_b, wdn, norm_final, seq):
    t = h2.shape[0]
    tm = FFN_TM
    halo_blocks = tm // FFN_HALO
    single = pl.Buffered(1)
    full = lambda shape, **kw: pl.BlockSpec(shape, lambda i: (0,) * len(shape), **kw)
    return pl.pallas_call(
        functools.partial(_ffn_kernel, tiles_per_seq=seq // tm),
        grid=(t // tm,),
        in_specs=[
            pl.BlockSpec((tm, D_MODEL), lambda i: (i, 0)),
            pl.BlockSpec((FFN_HALO, D_MODEL), lambda i: (jnp.maximum(i * halo_blocks - 1, 0), 0)),
            pl.BlockSpec((tm, D_MODEL), lambda i: (i, 0)),
            full((D_MODEL, 2 * FFN_DIM), pipeline_mode=single),
            full((FFN_CONV, 2 * FFN_DIM)),
            full((1, 2 * FFN_DIM)),
            full((FFN_DIM, D_MODEL), pipeline_mode=single),
            full((1, D_MODEL)),
        ],
        out_specs=pl.BlockSpec((tm, D_MODEL), lambda i: (i, 0)),
        out_shape=jax.ShapeDtypeStruct((t, D_MODEL), F32),
        scratch_shapes=[pltpu.VMEM((tm, D_MODEL), F32)],
        compiler_params=pltpu.CompilerParams(
            dimension_semantics=("arbitrary",), vmem_limit_bytes=VMEM_LIMIT),
        name="ffn",
    )(h2, h2, x1, wup, conv_w, conv_b, wdn, norm_final)


def kernel(x, rel_bias, norm_mix, w_in, kv_norm, w_uk, w_uv, conv_ssm_w, conv_ssm_b, dt_bias, a_log,
           d_skip, ssm_norm, w_att_out, w_ssm_out, w_out, norm_ffn, w_ffn_up, conv_ffn_w, conv_ffn_b,
           w_ffn_down, norm_final):
    batch, seq, _ = x.shape
    assert norm_mix.shape[0] == 1, "single layer"
    assert seq % (TILES_PER_CLASS * Q_BLOCK) == 0 and min(TOPK_MAX, seq // 4) == TOPK_MAX
    x2d = x.reshape(batch * seq, D_MODEL)

    tab = _bias_tiles(rel_bias)
    proj, yn = _inproj_ssd(x2d, norm_mix[0][None, :], _pack_w_in(w_in), conv_ssm_w[0], conv_ssm_b[0],
                           dt_bias[0], a_log[0], d_skip[0], ssm_norm[0], seq)
    att = _dsa_attention(proj, jnp.swapaxes(w_uk[0], 1, 2).astype(BF16), w_uv[0].astype(BF16),
                         kv_norm[0][None, :], tab, batch, seq)
    x1, h2 = _merge(att, yn, proj, x2d, w_att_out[0].astype(BF16), w_ssm_out[0].astype(BF16),
                    w_out[0].astype(BF16), norm_ffn[0][None, :])
    out = _ffn(h2, x1, w_ffn_up[0].astype(BF16), conv_ffn_w[0], conv_ffn_b[0][None, :],
               w_ffn_down[0].astype(BF16), norm_final[None, :], seq)
    return out.reshape(batch, seq, D_MODEL)
```

```python
import functools
import math

import numpy as np
import jax
import jax.numpy as jnp
from jax import lax
from jax.experimental import pallas as pl
from jax.experimental.pallas import tpu as pltpu

F32 = jnp.float32
BF16 = jnp.bfloat16
I32 = jnp.int32

D_MODEL = 1024
ATT_HEADS = 16
ATT_HEAD_DIM = 64
KV_LATENT = 128
IDX_HEADS = 8
IDX_DIM = 64
TOPK_MAX = 256
Q_BLOCK = 128
REL_BUCKETS = 32
REL_MAX_DIST = 128
SSM_INNER = 2 * D_MODEL
SSM_HEAD_DIM = 64
SSM_HEADS = SSM_INNER // SSM_HEAD_DIM
SSM_GROUPS = 4
SSM_HEADS_PER_GROUP = SSM_HEADS // SSM_GROUPS
SSM_STATE = 128
SSM_CONV = 4
SSM_CHUNK = 128
SSM_BC = SSM_GROUPS * SSM_STATE
FFN_DIM = 2816
FFN_CONV = 3
EPS = 1e-6

V7X_LANES = 128
V7X_MXU_WIDTH = 256
V7X_VMEM_BYTES = 64 * 1024 * 1024
VMEM_LIMIT = 56 * 1024 * 1024

INT_MIN = -(2 ** 31)
NEG_INF = float("-inf")
LOG2E = math.log2(math.e)

COL_Q = 0
COL_GA = COL_Q + D_MODEL
COL_GS = COL_GA + D_MODEL
COL_QI = COL_GS + D_MODEL
COL_CR = COL_QI + IDX_HEADS * IDX_DIM
COL_MISC = COL_CR + KV_LATENT
PROJ_COLS = COL_MISC + V7X_LANES
COL_Z = PROJ_COLS
COL_XS = COL_Z + SSM_INNER
COL_BM = COL_XS + SSM_INNER
COL_CM = COL_BM + SSM_BC
W_COLS = COL_CM + SSM_BC
SSD_COLS = W_COLS - PROJ_COLS
MISC_KIDX = 0
MISC_WIDX = IDX_DIM
MISC_DT = IDX_DIM + IDX_HEADS


def _pack_moves():
    sizes = (ATT_HEADS * ATT_HEAD_DIM, KV_LATENT, IDX_HEADS * IDX_DIM, IDX_DIM, IDX_HEADS,
             SSM_INNER, SSM_INNER, SSM_BC, SSM_BC, SSM_HEADS, D_MODEL, D_MODEL)
    src = np.cumsum((0,) + sizes)
    dst = (COL_Q, COL_CR, COL_QI, COL_MISC + MISC_KIDX, COL_MISC + MISC_WIDX, COL_Z, COL_XS, COL_BM,
           COL_CM, COL_MISC + MISC_DT, COL_GA, COL_GS)
    return [(int(s), int(d), int(n)) for s, d, n in zip(src[:-1], dst, sizes)]


PACK_ROWS = 128


def _pack_kernel(wt_ref, o_ref):
    misc = []
    for s, d, n in _pack_moves():
        if COL_MISC <= d < PROJ_COLS:
            assert d == COL_MISC + sum(p.shape[0] for p in misc)
            misc.append(wt_ref[s:s + n, :])
        else:
            o_ref[:, d:d + n] = wt_ref[s:s + n, :].T.astype(BF16)
    used = sum(p.shape[0] for p in misc)
    misc.append(jnp.zeros((V7X_LANES - used, PACK_ROWS), F32))
    o_ref[:, COL_MISC:PROJ_COLS] = jnp.concatenate(misc, axis=0).T.astype(BF16)


def _pack_w_in(w):
    _, k, cols = w.shape
    wt = jnp.swapaxes(w, 1, 2)
    assert all(s % 8 == 0 for s, _, _ in _pack_moves())
    return pl.pallas_call(
        _pack_kernel,
        grid=(k // PACK_ROWS,),
        in_specs=[pl.BlockSpec((None, cols, PACK_ROWS), lambda i: (0, 0, i))],
        out_specs=pl.BlockSpec((PACK_ROWS, W_COLS), lambda i: (i, 0)),
        out_shape=jax.ShapeDtypeStruct((k, W_COLS), BF16),
        name="pack_w_in",
    )(wt)


def _rms(x, gain):
    return x * lax.rsqrt(jnp.mean(x * x, axis=-1, keepdims=True) + EPS) * gain


def _sigmoid(x):
    return 0.5 + 0.5 * jnp.tanh(0.5 * x)


def _silu(x):
    h = 0.5 * x
    return h + h * jnp.tanh(h)


def _bias_tiles_kernel(rb_ref, o_ref):
    h = pl.program_id(0)
    far = rb_ref[REL_BUCKETS - 1, h]
    key = lax.broadcasted_iota(I32, (Q_BLOCK, Q_BLOCK), 0)
    qry = lax.broadcasted_iota(I32, (Q_BLOCK, Q_BLOCK), 1)
    max_exact = REL_BUCKETS // 2
    for delta in range(2):
        n = jnp.maximum(delta * Q_BLOCK + qry - key, 0)
        nf = jnp.maximum(n, 1).astype(F32)
        large = max_exact + (jnp.log(nf / max_exact) / math.log(REL_MAX_DIST / max_exact)
                             * (REL_BUCKETS - max_exact)).astype(I32)
        large = jnp.minimum(large, REL_BUCKETS - 1)
        bucket = jnp.where(n < max_exact, n, large)
        val = jnp.zeros((Q_BLOCK, Q_BLOCK), F32)
        for b in range(REL_BUCKETS):
            val = jnp.where(bucket == b, rb_ref[b, h] - far, val)
        o_ref[delta] = val * LOG2E


def _bias_tiles(rel_bias):
    return pl.pallas_call(
        _bias_tiles_kernel,
        grid=(ATT_HEADS,),
        in_specs=[pl.BlockSpec(memory_space=pltpu.SMEM)],
        out_specs=pl.BlockSpec((2, Q_BLOCK, Q_BLOCK), lambda h: (0, 0, h)),
        out_shape=jax.ShapeDtypeStruct((2, Q_BLOCK, ATT_HEADS * Q_BLOCK), F32),
        name="bias_tiles",
    )(rel_bias)


TILES_PER_CLASS = 2
PAD_TILES = TILES_PER_CLASS - 1
PAD_ROWS = PAD_TILES * Q_BLOCK
ROW_CHUNK = 256
COUNT_ROWS = 256
PAIR_LANES = 2 * Q_BLOCK
assert PAIR_LANES == V7X_MXU_WIDTH and (TILES_PER_CLASS * Q_BLOCK) % ROW_CHUNK == 0
ONES_ROWS = 16
PV_ROWS = KV_LATENT + ONES_ROWS
NEG_BIG = -1e30


SUBS = TILES_PER_CLASS


def _aligned(x, m):
    return x if isinstance(x, int) else pl.multiple_of(x, m)


def _attn_step(k, qiT, wT, tab_ref, cm_s, ckvT_s, kidx_s, qT_s, key_s, j_s, l0_s, l1_s, p0_s, p1_s,
               ot_s):
    n_tiles = TILES_PER_CLASS * (k + 1)
    nkp = n_tiles * Q_BLOCK
    n_chunks = nkp // ROW_CHUNK
    n_sel = float(TOPK_MAX)
    lane = lax.broadcasted_iota(I32, (ROW_CHUNK, Q_BLOCK), 1)
    row_in_chunk = lax.broadcasted_iota(I32, (ROW_CHUNK, Q_BLOCK), 0)

    for sub in range(SUBS):
        start = sub * Q_BLOCK
        key0 = (sub - PAD_TILES) * Q_BLOCK
        q_pos = (SUBS * k + sub) * Q_BLOCK + lane
        ql = slice(sub * Q_BLOCK, (sub + 1) * Q_BLOCK)
        for rc in range(n_chunks):
            rows = slice(rc * ROW_CHUNK, (rc + 1) * ROW_CHUNK)
            kc = kidx_s[start + rc * ROW_CHUNK:start + (rc + 1) * ROW_CHUNK, :]
            acc = jnp.zeros((ROW_CHUNK, Q_BLOCK), F32)
            for hp in range(IDX_HEADS // 2):
                rhs = jnp.concatenate(
                    [qiT[(2 * hp) * IDX_DIM:(2 * hp + 1) * IDX_DIM, ql],
                     qiT[(2 * hp + 1) * IDX_DIM:(2 * hp + 2) * IDX_DIM, ql]], axis=1)
                lg = jnp.dot(kc, rhs, preferred_element_type=F32)
                w0 = wT[MISC_WIDX + 2 * hp:MISC_WIDX + 2 * hp + 1, ql]
                w1 = wT[MISC_WIDX + 2 * hp + 1:MISC_WIDX + 2 * hp + 2, ql]
                acc = (acc + jnp.maximum(lg[:, :Q_BLOCK], 0.0) * w0
                       + jnp.maximum(lg[:, Q_BLOCK:], 0.0) * w1)
            bits = pltpu.bitcast(acc, I32)
            bits = jnp.where(bits == INT_MIN, 0, bits)
            key = jnp.where(bits < 0, bits ^ 0x7FFFFFFF, bits)
            key_pos = key0 + rc * ROW_CHUNK + row_in_chunk
            if key0 + rc * ROW_CHUNK < 0:
                key = jnp.where(key_pos >= 0, key, INT_MIN)
            if rc == n_chunks - 1:
                key = jnp.where(key_pos <= q_pos, key, INT_MIN)
            key_s[sub, rows, :] = key

    def count_ones(ones):
        part = jnp.sum(ones.reshape(nkp // COUNT_ROWS, COUNT_ROWS, Q_BLOCK), axis=0)
        return jnp.sum(part, axis=0, keepdims=True)

    def count(pred):
        return count_ones(jnp.where(pred, 1.0, 0.0))

    def value_step(bit, thrs):
        flip = jnp.left_shift(jnp.int32(1), 31 - bit)
        cands = [thr ^ flip for thr in thrs]
        hits = [count(key_s[sub, 0:nkp, :] >= cands[sub]) >= n_sel for sub in range(SUBS)]
        return tuple(jnp.where(hits[sub], cands[sub], thrs[sub]) for sub in range(SUBS))

    thrs = lax.fori_loop(0, 32, value_step,
                         tuple(jnp.full((1, Q_BLOCK), INT_MIN, I32) for _ in range(SUBS)))

    row_in_tile = lax.broadcasted_iota(I32, (Q_BLOCK, Q_BLOCK), 0)
    for sub in range(SUBS):
        thr = thrs[sub]
        keys = key_s[sub, 0:nkp, :]
        need = n_sel - count(keys > thr)
        split = count(keys == thr) > need

        j_s[...] = jnp.full((1, Q_BLOCK), 2 * nkp, I32)

        @pl.when(jnp.max(jnp.where(split, 1.0, 0.0)) > 0.5)
        def _(sub=sub, thr=thr, need=need):
            row = lax.broadcasted_iota(I32, (nkp, Q_BLOCK), 0)
            nbits = (2 * nkp - 1).bit_length()

            def index_step(bit, jmax):
                cand = jmax | jnp.left_shift(jnp.int32(1), nbits - 1 - bit)
                f = count_ones(jnp.where(key_s[sub, 0:nkp, :] == thr,
                                         jnp.where(row < cand, 1.0, 0.0), 0.0))
                return jnp.where(f <= need, cand, jmax)

            j_s[...] = lax.fori_loop(0, nbits, index_step, jnp.zeros((1, Q_BLOCK), I32))

        jmax = j_s[...]
        for t in range(n_tiles):
            kt = key_s[sub, t * Q_BLOCK:(t + 1) * Q_BLOCK, :]
            row = t * Q_BLOCK + row_in_tile
            add = jnp.where(kt > thr, 0.0,
                            jnp.where(kt == thr, jnp.where(row < jmax, 0.0, NEG_BIG), NEG_BIG))
            add = jnp.where(kt == INT_MIN, NEG_BIG, add)
            cm_s[sub, (sub + t) * Q_BLOCK:(sub + t + 1) * Q_BLOCK,
                 KV_LATENT:2 * KV_LATENT] = add.astype(BF16)

    tiles_per_chunk = ROW_CHUNK // Q_BLOCK
    n_pairs = ATT_HEADS // 2
    n_units = SUBS * n_pairs
    pair_bits = n_pairs.bit_length() - 1
    assert n_pairs == 1 << pair_bits

    def unit(logits=None, exps=None, values=None):
        if logits is not None:
            u_a, lbuf_a = logits
            sub_a = u_a >> pair_bits
            start_a = sub_a * Q_BLOCK
            off_t = _aligned((u_a & (n_pairs - 1)) * PAIR_LANES, PAIR_LANES)
            rhs = qT_s[:, pl.ds(_aligned(u_a * PAIR_LANES, PAIR_LANES), PAIR_LANES)]
            m8 = jnp.full((8, PAIR_LANES), NEG_INF, F32)
        if exps is not None:
            lbuf_b, pbuf_b, m_b = exps
        if values is not None:
            u_c, pbuf_c = values
            start_c = (u_c >> pair_bits) * Q_BLOCK
            acc = None
        for rc in range(n_chunks):
            rows = slice(rc * ROW_CHUNK, (rc + 1) * ROW_CHUNK)
            if logits is not None:
                win = pl.ds(_aligned(start_a + rc * ROW_CHUNK, Q_BLOCK), ROW_CHUNK)
                lc = jnp.dot(cm_s[sub_a, win, :], rhs, preferred_element_type=F32)
                for tt in range(tiles_per_chunk):
                    t = rc * tiles_per_chunk + tt
                    lt = lc[tt * Q_BLOCK:(tt + 1) * Q_BLOCK, :]
                    if t >= n_tiles - 2:
                        lt = lt + tab_ref[n_tiles - 1 - t, :, pl.ds(off_t, PAIR_LANES)]
                    lbuf_a[t * Q_BLOCK:(t + 1) * Q_BLOCK, :] = lt
                    m8 = jnp.maximum(m8, jnp.max(lt.reshape(Q_BLOCK // 8, 8, PAIR_LANES), axis=0))
            if exps is not None:
                pbuf_b[rows, :] = jnp.exp2(lbuf_b[rows, :] - m_b).astype(BF16)
            if values is not None:
                win = pl.ds(_aligned(start_c + rc * ROW_CHUNK, Q_BLOCK), ROW_CHUNK)
                part = jnp.dot(ckvT_s[:, win], pbuf_c[rows, :], preferred_element_type=F32)
                acc = part if acc is None else acc + part
        if values is not None:
            ot_s[:, pl.ds(_aligned(u_c * PAIR_LANES, PAIR_LANES), PAIR_LANES)] = (
                acc[0:KV_LATENT, :] / acc[KV_LATENT:KV_LATENT + 1, :]).astype(BF16)
        if logits is not None:
            return jnp.max(m8, axis=0, keepdims=True)

    def two_units(jj, m_odd):
        u = 2 * jj
        m_even = unit(logits=(u + 2, l0_s), exps=(l1_s, p1_s, m_odd), values=(u, p0_s))
        return unit(logits=(u + 3, l1_s), exps=(l0_s, p0_s, m_even), values=(u + 1, p1_s))

    m_even = unit(logits=(0, l0_s))
    m_odd = unit(logits=(1, l1_s), exps=(l0_s, p0_s, m_even))
    m_odd = lax.fori_loop(0, n_units // 2 - 1, two_units, m_odd)
    unit(exps=(l1_s, p1_s, m_odd), values=(n_units - 2, p0_s))
    unit(values=(n_units - 1, p1_s))


def _attn_kernel(q_ref, qi_ref, mq_ref, cr_ref, mk_ref, wukT_ref, wuv_ref, kvn_ref, tab_ref, o_ref,
                 cm_s, ckvT_s, kidx_s, qT_s, key_s, j_s, l0_s, l1_s, p0_s, p1_s, ot_s, attT_s):
    step = pl.program_id(1)
    seq = cr_ref.shape[0]

    @pl.when(step == 0)
    def _():
        c = _rms(cr_ref[...], kvn_ref[...])
        for sub in range(SUBS):
            cm_s[sub, 0:PAD_ROWS, :] = jnp.zeros((PAD_ROWS, 2 * KV_LATENT), BF16)
            cm_s[sub, PAD_ROWS:, 0:KV_LATENT] = c.astype(BF16)
        ckvT_s[0:KV_LATENT, 0:PAD_ROWS] = jnp.zeros((KV_LATENT, PAD_ROWS), BF16)
        ckvT_s[0:KV_LATENT, PAD_ROWS:] = c.T.astype(BF16)
        ckvT_s[KV_LATENT:, :] = jnp.ones((ONES_ROWS, seq + PAD_ROWS), BF16)
        kidx_s[0:PAD_ROWS, :] = jnp.zeros((PAD_ROWS, IDX_DIM), BF16)
        kidx_s[PAD_ROWS:, :] = mk_ref[:, MISC_KIDX:MISC_KIDX + IDX_DIM].astype(BF16)
        eye = jnp.where(lax.broadcasted_iota(I32, (Q_BLOCK, Q_BLOCK), 0)
                        == lax.broadcasted_iota(I32, (Q_BLOCK, Q_BLOCK), 1), 1.0, 0.0).astype(BF16)
        for blk in range(SUBS * ATT_HEADS):
            qT_s[KV_LATENT:, blk * Q_BLOCK:(blk + 1) * Q_BLOCK] = eye

    scale = ATT_HEAD_DIM ** -0.5 * LOG2E
    qb = q_ref[...].astype(BF16)
    for sub in range(SUBS):
        for h in range(ATT_HEADS):
            qh = qb[sub * Q_BLOCK:(sub + 1) * Q_BLOCK, h * ATT_HEAD_DIM:(h + 1) * ATT_HEAD_DIM]
            qt = lax.dot_general(wukT_ref[h], qh, (((1,), (1,)), ((), ())),
                                 preferred_element_type=F32)
            blk = sub * ATT_HEADS + h
            qT_s[0:KV_LATENT, blk * Q_BLOCK:(blk + 1) * Q_BLOCK] = (qt * scale).astype(BF16)

    qiT = qi_ref[...].T.astype(BF16)
    wT = mq_ref[...].T

    for k in range(seq // Q_BLOCK // SUBS):
        @pl.when(step == k)
        def _(k=k):
            _attn_step(k, qiT, wT, tab_ref, cm_s, ckvT_s, kidx_s, qT_s, key_s, j_s, l0_s, l1_s,
                       p0_s, p1_s, ot_s)

    for sub in range(SUBS):
        for h in range(ATT_HEADS):
            blk = sub * ATT_HEADS + h
            attT_s[h * ATT_HEAD_DIM:(h + 1) * ATT_HEAD_DIM, sub * Q_BLOCK:(sub + 1) * Q_BLOCK] = (
                jnp.dot(wuv_ref[h], ot_s[:, blk * Q_BLOCK:(blk + 1) * Q_BLOCK],
                        preferred_element_type=F32))
    o_ref[...] = attT_s[...].T.astype(BF16)


def _dsa_attention(proj, wukT, wuv, kv_norm, tab, batch, seq):
    rows = SUBS * Q_BLOCK
    nsteps = seq // rows
    att_dim = ATT_HEADS * ATT_HEAD_DIM
    return pl.pallas_call(
        _attn_kernel,
        grid=(batch, nsteps),
        in_specs=[
            pl.BlockSpec((rows, att_dim), lambda b, i: (b * nsteps + i, COL_Q // att_dim)),
            pl.BlockSpec((rows, IDX_HEADS * IDX_DIM),
                         lambda b, i: (b * nsteps + i, COL_QI // (IDX_HEADS * IDX_DIM))),
            pl.BlockSpec((rows, V7X_LANES), lambda b, i: (b * nsteps + i, COL_MISC // V7X_LANES)),
            pl.BlockSpec((seq, KV_LATENT), lambda b, i: (b, COL_CR // KV_LATENT)),
            pl.BlockSpec((seq, V7X_LANES), lambda b, i: (b, COL_MISC // V7X_LANES)),
            pl.BlockSpec((ATT_HEADS, KV_LATENT, ATT_HEAD_DIM), lambda b, i: (0, 0, 0)),
            pl.BlockSpec((ATT_HEADS, ATT_HEAD_DIM, KV_LATENT), lambda b, i: (0, 0, 0)),
            pl.BlockSpec((1, KV_LATENT), lambda b, i: (0, 0)),
            pl.BlockSpec((2, Q_BLOCK, ATT_HEADS * Q_BLOCK), lambda b, i: (0, 0, 0)),
        ],
        out_specs=pl.BlockSpec((rows, att_dim), lambda b, i: (b * nsteps + i, 0)),
        out_shape=jax.ShapeDtypeStruct((batch * seq, att_dim), BF16),
        scratch_shapes=[
            pltpu.VMEM((SUBS, seq + PAD_ROWS, 2 * KV_LATENT), BF16),
            pltpu.VMEM((PV_ROWS, seq + PAD_ROWS), BF16),
            pltpu.VMEM((seq + PAD_ROWS, IDX_DIM), BF16),
            pltpu.VMEM((2 * KV_LATENT, SUBS * ATT_HEADS * Q_BLOCK), BF16),
            pltpu.VMEM((SUBS, seq, Q_BLOCK), I32),
            pltpu.VMEM((1, Q_BLOCK), I32),
            pltpu.VMEM((seq, PAIR_LANES), F32),
            pltpu.VMEM((seq, PAIR_LANES), F32),
            pltpu.VMEM((seq, PAIR_LANES), BF16),
            pltpu.VMEM((seq, PAIR_LANES), BF16),
            pltpu.VMEM((KV_LATENT, SUBS * ATT_HEADS * Q_BLOCK), BF16),
            pltpu.VMEM((att_dim, rows), F32),
        ],
        compiler_params=pltpu.CompilerParams(
            dimension_semantics=("arbitrary", "arbitrary"), vmem_limit_bytes=VMEM_LIMIT),
        name="dsa_attn",
    )(proj, proj, proj, proj, proj, wukT, wuv, kv_norm, tab)


TAIL = 8
GROUP_LANES = SSM_HEADS_PER_GROUP * SSM_HEAD_DIM


def _expand_matrices():
    e64 = np.zeros((V7X_LANES, SSM_INNER), np.float32)
    e128 = np.zeros((V7X_LANES, SSM_HEADS * SSM_CHUNK), np.float32)
    for h in range(SSM_HEADS):
        for piece in range(3):
            lane = (MISC_DT + h + piece * SSM_HEADS) % V7X_LANES
            e64[lane, h * SSM_HEAD_DIM:(h + 1) * SSM_HEAD_DIM] = 1.0
            e128[lane, h * SSM_CHUNK:(h + 1) * SSM_CHUNK] = 1.0
    return jnp.asarray(e64, BF16), jnp.asarray(e128, BF16)


def _split3(v):
    hi = v.astype(BF16).astype(F32)
    r1 = v - hi
    mid = r1.astype(BF16).astype(F32)
    lo = r1 - mid
    packed = hi + pltpu.roll(mid, SSM_HEADS, axis=1) + pltpu.roll(lo, 2 * SSM_HEADS, axis=1)
    return packed.astype(BF16)


def _conv_silu(u_ref, tail_s, w_ref, b_ref):
    n = SSM_CHUNK
    u = u_ref[...]
    ext = jnp.concatenate([tail_s[...], u], axis=0)
    out = b_ref[...] + w_ref[SSM_CONV - 1:SSM_CONV, :] * u
    for k in range(1, SSM_CONV):
        out = out + w_ref[SSM_CONV - 1 - k:SSM_CONV - k, :] * pltpu.roll(ext, k, axis=0)[TAIL:, :]
    tail_s[...] = u[n - TAIL:n, :]
    return _silu(out)


SSD_CHUNKS_PER_STEP = 2
INPROJ_TM = SSD_CHUNKS_PER_STEP * SSM_CHUNK
INPROJ_TN = V7X_MXU_WIDTH
assert PROJ_COLS % INPROJ_TN == 0 and W_COLS % INPROJ_TN == 0


def _inproj_ssd_kernel(x_ref, g_ref, w_ref, *rest, tiles_per_seq):
    ssd_params = rest[:-9]
    proj_ref, yn_ref, h_s, ssd_in_s, misc_s, xtail_s, btail_s, ctail_s, state_s = rest[-9:]
    i = pl.program_id(0)

    @pl.when(i == 0)
    def _():
        ssd_in_s[...] = jnp.zeros_like(ssd_in_s)
        misc_s[...] = jnp.zeros_like(misc_s)

    @pl.when(lax.rem(jnp.maximum(i - 1, 0), tiles_per_seq) == 0)
    def _():
        state_s[...] = jnp.zeros_like(state_s)
        xtail_s[...] = jnp.zeros_like(xtail_s)
        btail_s[...] = jnp.zeros_like(btail_s)
        ctail_s[...] = jnp.zeros_like(ctail_s)

    def step(slot):
        prev = 1 - slot
        h_s[...] = _rms(x_ref[...], g_ref[...]).astype(BF16)
        todo = list(range(0, W_COLS, INPROJ_TN))

        def project(count=1):
            for _ in range(min(count, len(todo))):
                col = todo.pop(0)
                r = jnp.dot(h_s[...], w_ref[:, col:col + INPROJ_TN], preferred_element_type=F32)
                if col < PROJ_COLS:
                    proj_ref[:, col:col + INPROJ_TN] = r
                    if col <= COL_MISC < col + INPROJ_TN:
                        misc_s[slot] = r[:, COL_MISC - col:COL_MISC - col + V7X_LANES]
                else:
                    ssd_in_s[slot, :, col - PROJ_COLS:col - PROJ_COLS + INPROJ_TN] = r

        for c in range(SSD_CHUNKS_PER_STEP):
            rows = pl.ds(c * SSM_CHUNK, SSM_CHUNK)

            def cols(col0, width):
                return ssd_in_s.at[prev, rows, pl.ds(col0 - PROJ_COLS, width)]

            _ssd_chunk(cols(COL_Z, SSM_INNER), cols(COL_XS, SSM_INNER), cols(COL_BM, SSM_BC),
                       cols(COL_CM, SSM_BC), misc_s.at[prev, rows], *ssd_params, yn_ref.at[rows],
                       xtail_s, btail_s, ctail_s, state_s, between=project)
        project(len(todo))

    for parity in range(2):
        pl.when(lax.rem(i, 2) == parity)(functools.partial(step, parity))


def _ssd_chunk(z_ref, xs_ref, bm_ref, cm_ref, misc_ref, cwx_ref, cwb_ref, cwc_ref, cbx_ref, cbb_ref,
               cbc_ref, dtb_ref, alog_ref, dsk_ref, gn_ref, e64_ref, e128_ref, o_ref,
               xtail_s, btail_s, ctail_s, state_s, between=lambda count=1: None):
    n = SSM_CHUNK
    xs = _conv_silu(xs_ref, xtail_s, cwx_ref, cbx_ref)
    between(2)
    bm = _conv_silu(bm_ref, btail_s, cwb_ref, cbb_ref)
    cm = _conv_silu(cm_ref, ctail_s, cwc_ref, cbc_ref)
    between()

    lane = lax.broadcasted_iota(I32, (n, V7X_LANES), 1)
    row = lax.broadcasted_iota(I32, (n, V7X_LANES), 0)
    on_dt = (lane >= MISC_DT) & (lane < MISC_DT + SSM_HEADS)

    v = misc_ref[...] + dtb_ref[...]
    dt = jnp.maximum(v, 0.0) + jnp.log1p(jnp.exp(-jnp.abs(v)))
    dt = jnp.where(on_dt, dt, 0.0)
    a_neg = -jnp.exp(alog_ref[...])
    acum = dt * a_neg
    k = 1
    while k < n:
        acum = acum + jnp.where(row >= k, pltpu.roll(acum, k, axis=0), 0.0)
        k *= 2

    e64 = e64_ref[...]
    dt_e = jnp.dot(_split3(dt), e64, preferred_element_type=F32)
    acum_p = _split3(acum)
    acum_e = jnp.dot(acum_p, e64, preferred_element_type=F32)
    acum_cb = jnp.dot(acum_p, e128_ref[...], preferred_element_type=F32)
    acum_t = acum.T
    between()

    last = acum_e[n - 1:n, :]
    xdt = xs * dt_e
    xds_b = (xdt * jnp.exp(last - acum_e)).astype(BF16)
    chunk_decay = jnp.exp(last)
    ea_e = jnp.exp(acum_e)
    between()

    tri = row >= lane
    first_head = lane < SSM_HEAD_DIM
    y_groups = []
    for g in range(SSM_GROUPS):
        gl = slice(g * GROUP_LANES, (g + 1) * GROUP_LANES)
        bmg = bm[:, g * SSM_STATE:(g + 1) * SSM_STATE]
        cmg_b = cm[:, g * SSM_STATE:(g + 1) * SSM_STATE].astype(BF16)
        cb = lax.dot_general(cmg_b, bmg.astype(BF16), (((1,), (1,)), ((), ())),
                             preferred_element_type=F32)
        y_pairs = []
        for pr in range(SSM_HEADS_PER_GROUP // 2):
            h0 = g * SSM_HEADS_PER_GROUP + 2 * pr
            mats = []
            for h in (h0, h0 + 1):
                seg = acum_cb[:, h * n:(h + 1) * n] - acum_t[MISC_DT + h:MISC_DT + h + 1, :]
                mats.append((cb * jnp.exp(jnp.where(tri, seg, NEG_INF))).astype(BF16))
            xp = xdt[:, h0 * SSM_HEAD_DIM:(h0 + 2) * SSM_HEAD_DIM]
            rhs = jnp.concatenate([jnp.where(first_head, xp, 0.0), jnp.where(first_head, 0.0, xp)],
                                  axis=0).astype(BF16)
            y_pairs.append(jnp.dot(jnp.concatenate(mats, axis=1), rhs, preferred_element_type=F32))
            if pr % 2 == 1:
                between()
        prev = state_s[g]
        y_off = jnp.dot(cmg_b, prev.astype(BF16), preferred_element_type=F32) * ea_e[:, gl]
        state_s[g] = prev * chunk_decay[:, gl] + jnp.dot(bmg.T.astype(BF16), xds_b[:, gl],
                                                         preferred_element_type=F32)
        y_groups.append(jnp.concatenate(y_pairs, axis=1) + y_off)
        between()

    z = z_ref[...]
    gz = _silu(z)
    outs = []
    for g in range(SSM_GROUPS):
        gl = slice(g * GROUP_LANES, (g + 1) * GROUP_LANES)
        u = (y_groups[g] + dsk_ref[:, gl] * xs[:, gl]) * gz[:, gl]
        outs.append(u * lax.rsqrt(jnp.mean(u * u, axis=-1, keepdims=True) + EPS) * gn_ref[:, gl])
        if g % 2 == 1:
            between()
    o_ref[...] = jnp.concatenate(outs, axis=1).astype(BF16)


def _inproj_ssd(x2d, gain, w_packed, conv_w, conv_b, dt_bias, a_log, d_skip, ssm_norm, seq):
    t = x2d.shape[0]
    n = INPROJ_TM
    n_tiles = t // n
    cwx, cwb, cwc = (conv_w[:, :SSM_INNER], conv_w[:, SSM_INNER:SSM_INNER + SSM_BC],
                     conv_w[:, SSM_INNER + SSM_BC:])
    cbx, cbb, cbc = (conv_b[None, :SSM_INNER], conv_b[None, SSM_INNER:SSM_INNER + SSM_BC],
                     conv_b[None, SSM_INNER + SSM_BC:])
    pad_l, pad_r = MISC_DT, V7X_LANES - MISC_DT - SSM_HEADS
    dtb = jnp.pad(dt_bias, (pad_l, pad_r))[None, :]
    alog = jnp.pad(a_log, (pad_l, pad_r))[None, :]
    dsk = jnp.repeat(d_skip, SSM_HEAD_DIM)[None, :]
    gn = ssm_norm[None, :]
    e64, e128 = _expand_matrices()
    full = lambda shape, **kw: pl.BlockSpec(shape, lambda i: (0,) * len(shape), **kw)
    last = n_tiles - 1
    return pl.pallas_call(
        functools.partial(_inproj_ssd_kernel, tiles_per_seq=seq // n),
        grid=(n_tiles + 1,),
        in_specs=[
            pl.BlockSpec((n, D_MODEL), lambda i: (jnp.minimum(i, last), 0)),
            full((1, D_MODEL)),
            full((D_MODEL, W_COLS), pipeline_mode=pl.Buffered(1)),
            full((SSM_CONV, SSM_INNER)), full((SSM_CONV, SSM_BC)), full((SSM_CONV, SSM_BC)),
            full((1, SSM_INNER)), full((1, SSM_BC)), full((1, SSM_BC)),
            full((1, V7X_LANES)), full((1, V7X_LANES)),
            full((1, SSM_INNER)), full((1, SSM_INNER)),
            full((V7X_LANES, SSM_INNER)), full((V7X_LANES, SSM_HEADS * SSM_CHUNK)),
        ],
        out_specs=[pl.BlockSpec((n, PROJ_COLS), lambda i: (jnp.minimum(i, last), 0)),
                   pl.BlockSpec((n, SSM_INNER), lambda i: (jnp.maximum(i - 1, 0), 0))],
        out_shape=[jax.ShapeDtypeStruct((t, PROJ_COLS), F32),
                   jax.ShapeDtypeStruct((t, SSM_INNER), BF16)],
        scratch_shapes=[
            pltpu.VMEM((n, D_MODEL), BF16),
            pltpu.VMEM((2, n, SSD_COLS), F32),
            pltpu.VMEM((2, n, V7X_LANES), F32),
            pltpu.VMEM((TAIL, SSM_INNER), F32),
            pltpu.VMEM((TAIL, SSM_BC), F32),
            pltpu.VMEM((TAIL, SSM_BC), F32),
            pltpu.VMEM((SSM_GROUPS, SSM_STATE, GROUP_LANES), F32),
        ],
        compiler_params=pltpu.CompilerParams(
            dimension_semantics=("arbitrary",), vmem_limit_bytes=VMEM_LIMIT),
        name="inproj_ssd",
    )(x2d, gain, w_packed, cwx, cwb, cwc, cbx, cbb, cbc, dtb, alog, dsk, gn, e64, e128)


MERGE_TM = 512


def _merge_kernel(att_ref, yn_ref, ga_ref, gs_ref, x_ref, wao_ref, wso_ref, wo_ref, gf_ref,
                  x1_ref, h2_ref):
    y_att = jnp.dot(att_ref[...], wao_ref[...], preferred_element_type=F32)
    y_ssm = jnp.dot(yn_ref[...], wso_ref[...], preferred_element_type=F32)
    merged = _sigmoid(ga_ref[...]) * y_att + _sigmoid(gs_ref[...]) * y_ssm
    x1 = x_ref[...] + jnp.dot(merged.astype(BF16), wo_ref[...], preferred_element_type=F32)
    x1_ref[...] = x1
    h2_ref[...] = _rms(x1, gf_ref[...]).astype(BF16)


def _merge(att, yn, proj, x2d, wao, wso, wo, norm_ffn):
    t = x2d.shape[0]
    tm = MERGE_TM
    full = lambda shape: pl.BlockSpec(shape, lambda i: (0,) * len(shape))
    return pl.pallas_call(
        _merge_kernel,
        grid=(t // tm,),
        in_specs=[
            pl.BlockSpec((tm, D_MODEL), lambda i: (i, 0)),
            pl.BlockSpec((tm, SSM_INNER), lambda i: (i, 0)),
            pl.BlockSpec((tm, D_MODEL), lambda i: (i, COL_GA // D_MODEL)),
            pl.BlockSpec((tm, D_MODEL), lambda i: (i, COL_GS // D_MODEL)),
            pl.BlockSpec((tm, D_MODEL), lambda i: (i, 0)),
            full((D_MODEL, D_MODEL)), full((SSM_INNER, D_MODEL)), full((D_MODEL, D_MODEL)),
            full((1, D_MODEL)),
        ],
        out_specs=[pl.BlockSpec((tm, D_MODEL), lambda i: (i, 0)),
                   pl.BlockSpec((tm, D_MODEL), lambda i: (i, 0))],
        out_shape=[jax.ShapeDtypeStruct((t, D_MODEL), F32),
                   jax.ShapeDtypeStruct((t, D_MODEL), BF16)],
        compiler_params=pltpu.CompilerParams(
            dimension_semantics=("arbitrary",), vmem_limit_bytes=VMEM_LIMIT),
        name="merge",
    )(att, yn, proj, proj, x2d, wao, wso, wo, norm_ffn)


FFN_TM = 512
FFN_HALO = 16
FFN_FC = V7X_MXU_WIDTH


def _ffn_kernel(h_ref, halo_ref, x1_ref, wup_ref, cw_ref, cb_ref, wdn_ref, gf_ref, o_ref,
                acc_s, *, tiles_per_seq):
    keep = jnp.where(pl.program_id(0) % tiles_per_seq == 0, 0.0, 1.0)
    hcat = jnp.concatenate([halo_ref[...], h_ref[...]], axis=0)
    n_chunks = FFN_DIM // FFN_FC

    def up(c, base):
        return jnp.dot(hcat, wup_ref[:, base + c * FFN_FC:base + (c + 1) * FFN_FC],
                       preferred_element_type=F32)

    def conv(u, c, base):
        cols = slice(base + c * FFN_FC, base + (c + 1) * FFN_FC)
        u = jnp.concatenate([u[0:FFN_HALO, :] * keep, u[FFN_HALO:, :]], axis=0)
        out = cb_ref[:, cols] + cw_ref[FFN_CONV - 1:FFN_CONV, cols] * u[FFN_HALO:, :]
        for k in range(1, FFN_CONV):
            out = out + (cw_ref[FFN_CONV - 1 - k:FFN_CONV - k, cols]
                         * pltpu.roll(u, k, axis=0)[FFN_HALO:, :])
        return out

    ug, uv = up(0, 0), up(0, FFN_DIM)
    for c in range(n_chunks):
        last = c + 1 == n_chunks
        ug_next = None if last else up(c + 1, 0)
        gate = conv(ug, c, 0)
        uv_next = None if last else up(c + 1, FFN_DIM)
        val = conv(uv, c, FFN_DIM)
        ug, uv = ug_next, uv_next
        act = (_silu(gate) * val).astype(BF16)
        contrib = jnp.dot(act, wdn_ref[c * FFN_FC:(c + 1) * FFN_FC, :], preferred_element_type=F32)
        if c == 0:
            acc_s[...] = contrib
        else:
            acc_s[...] += contrib
    o_ref[...] = _rms(x1_ref[...] + acc_s[...], gf_ref[...])


def _ffn(h2, x1, wup, conv_w, conv_b, wdn, norm_final, seq):
    t = h2.shape[0]
    tm = FFN_TM
    halo_blocks = tm // FFN_HALO
    single = pl.Buffered(1)
    full = lambda shape, **kw: pl.BlockSpec(shape, lambda i: (0,) * len(shape), **kw)
    return pl.pallas_call(
        functools.partial(_ffn_kernel, tiles_per_seq=seq // tm),
        grid=(t // tm,),
        in_specs=[
            pl.BlockSpec((tm, D_MODEL), lambda i: (i, 0)),
            pl.BlockSpec((FFN_HALO, D_MODEL), lambda i: (jnp.maximum(i * halo_blocks - 1, 0), 0)),
            pl.BlockSpec((tm, D_MODEL), lambda i: (i, 0)),
            full((D_MODEL, 2 * FFN_DIM), pipeline_mode=single),
            full((FFN_CONV, 2 * FFN_DIM)),
            full((1, 2 * FFN_DIM)),
            full((FFN_DIM, D_MODEL), pipeline_mode=single),
            full((1, D_MODEL)),
        ],
        out_specs=pl.BlockSpec((tm, D_MODEL), lambda i: (i, 0)),
        out_shape=jax.ShapeDtypeStruct((t, D_MODEL), F32),
        scratch_shapes=[pltpu.VMEM((tm, D_MODEL), F32)],
        compiler_params=pltpu.CompilerParams(
            dimension_semantics=("arbitrary",), vmem_limit_bytes=VMEM_LIMIT),
        name="ffn",
    )(h2, h2, x1, wup, conv_w, conv_b, wdn, norm_final)


def kernel(x, rel_bias, norm_mix, w_in, kv_norm, w_uk, w_uv, conv_ssm_w, conv_ssm_b, dt_bias, a_log,
           d_skip, ssm_norm, w_att_out, w_ssm_out, w_out, norm_ffn, w_ffn_up, conv_ffn_w, conv_ffn_b,
           w_ffn_down, norm_final):
    batch, seq, _ = x.shape
    assert norm_mix.shape[0] == 1, "single layer"
    assert seq % (TILES_PER_CLASS * Q_BLOCK) == 0 and min(TOPK_MAX, seq // 4) == TOPK_MAX
    x2d = x.reshape(batch * seq, D_MODEL)

    tab = _bias_tiles(rel_bias)
    proj, yn = _inproj_ssd(x2d, norm_mix[0][None, :], _pack_w_in(w_in), conv_ssm_w[0], conv_ssm_b[0],
                           dt_bias[0], a_log[0], d_skip[0], ssm_norm[0], seq)
    att = _dsa_attention(proj, jnp.swapaxes(w_uk[0], 1, 2).astype(BF16), w_uv[0].astype(BF16),
                         kv_norm[0][None, :], tab, batch, seq)
    x1, h2 = _merge(att, yn, proj, x2d, w_att_out[0].astype(BF16), w_ssm_out[0].astype(BF16),
                    w_out[0].astype(BF16), norm_ffn[0][None, :])
    out = _ffn(h2, x1, w_ffn_up[0].astype(BF16), conv_ffn_w[0], conv_ffn_b[0][None, :],
               w_ffn_down[0].astype(BF16), norm_final[None, :], seq)
    return out.reshape(batch, seq, D_MODEL)
```
